```python
import jax, jax.numpy as jnp
from jax import lax
import numpy as np

D_MODEL = 2048
BATCH = 4
SEQ = 2048
DEPTH = 2
DEC_BATCH = 128
DEC_SEQ = 1
PAST_LEN = 2048
PAGE_SIZE = 128

N_MIXERS = 2
N_POOL_LAYERS = (DEPTH + 1) // 2
N_NSA_LAYERS = DEPTH // 2
D_FF = 5632
POOL_WINDOWS = (2, 4, 8, 16)
N_POOL_GROUPS = len(POOL_WINDOWS)
POOL_GROUP_DIM = D_MODEL // N_POOL_GROUPS
POOL_MAX_WIN = max(POOL_WINDOWS)
POOL_STATE_LEN = POOL_MAX_WIN - 1
N_HEADS = 16
HEAD_DIM = D_MODEL // N_HEADS
N_KV_HEADS = 4
GQA_GROUP = N_HEADS // N_KV_HEADS
Q_DIM = N_HEADS * HEAD_DIM
KV_DIM = N_KV_HEADS * HEAD_DIM
N_BRANCHES = 3
IN_COLS = Q_DIM + 6 * KV_DIM + N_BRANCHES * N_HEADS
CMP_BLOCK = 32
CMP_STRIDE = 16
CMP_RATIO = CMP_BLOCK // CMP_STRIDE
SEL_BLOCK = 64
SEL_TOPK = 16
WINDOW = 512
Q_BLOCK = 64
SCALE = HEAD_DIM ** -0.5
ROPE_THETA = 10000.0
RMS_EPS = 1e-6
NEG_INF = -1e30
FORCE_SCORE = 1e9
TINY = 1e-30

kernel_name = "hybrid_pool_nsa_macaron_decode_step"


def rms_norm(x, g):
    xf = x.astype(jnp.float32)
    y = xf * lax.rsqrt(jnp.mean(xf * xf, axis=-1, keepdims=True) + RMS_EPS)
    return (y * g.astype(jnp.float32)).astype(x.dtype)


def swiglu(x, w_gate, w_up, w_down):
    return (jax.nn.silu(x @ w_gate) * (x @ w_up)) @ w_down


def rope(x, pos):
    half = HEAD_DIM // 2
    inv_freq = jnp.power(ROPE_THETA, -jnp.arange(half, dtype=jnp.float32) / half)
    ang = pos.astype(jnp.float32)[:, None] * inv_freq[None, :]
    cos = jnp.cos(ang)[None, :, None, :]
    sin = jnp.sin(ang)[None, :, None, :]
    xf = x.astype(jnp.float32)
    x1, x2 = xf[..., :half], xf[..., half:]
    return jnp.concatenate([x1 * cos - x2 * sin, x2 * cos + x1 * sin], axis=-1).astype(x.dtype)


def masked_softmax(s, mask):
    s = jnp.where(mask, s, NEG_INF)
    e = jnp.where(mask, jnp.exp(s - jnp.max(s, axis=-1, keepdims=True)), 0.0)
    return e / jnp.maximum(jnp.sum(e, axis=-1, keepdims=True), TINY)


def pool_mixer(xn, prefix, start_pos, w_pool, scale):
    B, S, _ = xn.shape
    P = prefix.shape[1]
    rows = jnp.concatenate([prefix.astype(xn.dtype), xn], axis=1)
    u = rows.astype(jnp.float32)
    cs = jnp.concatenate([jnp.zeros((B, POOL_MAX_WIN, D_MODEL), jnp.float32),
                          jnp.cumsum(u, axis=1)], axis=1)
    pos = start_pos + jnp.arange(S)
    hi = cs[:, POOL_MAX_WIN + P: POOL_MAX_WIN + P + S]
    outs = []
    for g, w in enumerate(POOL_WINDOWS):
        sl = slice(g * POOL_GROUP_DIM, (g + 1) * POOL_GROUP_DIM)
        lo = cs[:, POOL_MAX_WIN + P - w: POOL_MAX_WIN + P - w + S, sl]
        cnt = jnp.minimum(w, pos + 1).astype(jnp.float32)[None, :, None]
        outs.append((hi[..., sl] - lo) / cnt - u[:, P:, sl])
    d = jnp.stack(outs, axis=2).astype(xn.dtype)
    y = jnp.einsum('bsgc,gce->bsge', d, w_pool).reshape(B, S, D_MODEL)
    return y * scale, rows[:, -POOL_STATE_LEN:]


def compress(rows, pe, w1, w2):
    B, T = rows.shape[:2]
    n_sub = T // CMP_STRIDE
    nc = n_sub - CMP_RATIO + 1
    sub = rows[:, :n_sub * CMP_STRIDE].reshape(B, n_sub, CMP_STRIDE, N_KV_HEADS, HEAD_DIM)
    pe_r = pe.reshape(CMP_RATIO, CMP_STRIDE, HEAD_DIM)
    w1_r = w1.reshape(CMP_RATIO, CMP_STRIDE, HEAD_DIM, HEAD_DIM)
    parts = [jnp.einsum('bnshd,sde->bnhe', sub + pe_r[i][None, None, :, None, :], w1_r[i])[:, i:i + nc]
             for i in range(CMP_RATIO)]
    h = parts[0]
    for p in parts[1:]:
        h = h + p
    return jax.nn.gelu(h) @ w2


def to_blocks(rows, n_sel):
    B, T = rows.shape[:2]
    rows = jnp.pad(rows, ((0, 0), (0, n_sel * SEL_BLOCK - T), (0, 0), (0, 0)))
    return rows.reshape(B, n_sel, SEL_BLOCK, N_KV_HEADS, HEAD_DIM)


def gather_blocks(blk, idx):
    take = jax.vmap(lambda b, i: b[i], in_axes=(2, 0))
    return jax.vmap(take)(blk, idx)


def nsa_project(xn, w_in, pos):
    B, S, _ = xn.shape
    z = xn @ w_in
    cuts = [Q_DIM + i * KV_DIM for i in range(7)]
    q, kc, vc, ks, vs, kw, vw, g = jnp.split(z, cuts, axis=-1)
    kv = lambda t: t.reshape(B, S, N_KV_HEADS, HEAD_DIM)
    grp = lambda t: t.reshape(B, S, N_KV_HEADS, GQA_GROUP, HEAD_DIM)
    q = q.reshape(B, S, N_HEADS, HEAD_DIM)
    gates = jax.nn.sigmoid(g.astype(jnp.float32)).astype(xn.dtype).reshape(B, S, N_HEADS, N_BRANCHES)
    return (grp(q), grp(rope(q, pos)), kv(kc), kv(vc), rope(kv(ks), pos), kv(vs),
            rope(kv(kw), pos), kv(vw), gates)


def nsa_branches(q_raw, q_rot, q_pos, kc, vc, ks_blk, vs_blk, kw, vw, kw_pos):
    n_cmp = kc.shape[1]
    n_sel = ks_blk.shape[1]
    c_start = jnp.arange(n_cmp) * CMP_STRIDE
    cmask = (c_start + CMP_BLOCK - 1)[None, :] <= q_pos[:, None]
    s_c = jnp.einsum('bqhgd,bnhd->bhgqn', q_raw, kc, preferred_element_type=jnp.float32) * SCALE
    p_c = masked_softmax(s_c, cmask)
    o_cmp = jnp.einsum('bhgqn,bnhd->bqhgd', p_c.astype(vc.dtype), vc)
    s_start = jnp.arange(n_sel) * SEL_BLOCK
    overlap = ((c_start[:, None] <= s_start[None, :] + SEL_BLOCK - 1)
               & (c_start[:, None] + CMP_BLOCK - 1 >= s_start[None, :])).astype(jnp.float32)
    imp = jnp.einsum('bhgqn,nj->bhqj', p_c, overlap)
    blk_id = jnp.arange(n_sel)[None, :]
    cur = (q_pos // SEL_BLOCK)[:, None]
    valid = s_start[None, :] <= q_pos[:, None]
    forced = (blk_id == 0) | (blk_id == cur) | (blk_id == cur - 1)
    score = jnp.where(valid, jnp.where(forced, FORCE_SCORE, imp), NEG_INF)
    top_val, top_idx = lax.top_k(score, min(SEL_TOPK, n_sel))
    k_g = gather_blocks(ks_blk, top_idx)
    v_g = gather_blocks(vs_blk, top_idx)
    B, _, Q, K = top_idx.shape
    tok_pos = top_idx[..., None] * SEL_BLOCK + jnp.arange(SEL_BLOCK)
    smask = (top_val > 0.5 * NEG_INF)[..., None] & (tok_pos <= q_pos[None, None, :, None, None])
    s_s = jnp.einsum('bqhgd,bhqksd->bhgqks', q_rot, k_g, preferred_element_type=jnp.float32) * SCALE
    p_s = masked_softmax(s_s.reshape(B, N_KV_HEADS, GQA_GROUP, Q, K * SEL_BLOCK),
                         smask.reshape(B, N_KV_HEADS, 1, Q, K * SEL_BLOCK))
    o_sel = jnp.einsum('bhgqks,bhqksd->bqhgd', p_s.reshape(s_s.shape).astype(v_g.dtype), v_g)
    wmask = ((kw_pos[None, :] <= q_pos[:, None]) & (kw_pos[None, :] > q_pos[:, None] - WINDOW)
             & (kw_pos[None, :] >= 0))
    s_w = jnp.einsum('bqhgd,bkhd->bhgqk', q_rot, kw, preferred_element_type=jnp.float32) * SCALE
    p_w = masked_softmax(s_w, wmask)
    o_win = jnp.einsum('bhgqk,bkhd->bqhgd', p_w.astype(vw.dtype), vw)
    return o_cmp, o_sel, o_win


def nsa_combine(o_cmp, o_sel, o_win, gates, w_out):
    B, Q = gates.shape[:2]
    g = gates.reshape(B, Q, N_KV_HEADS, GQA_GROUP, N_BRANCHES)
    o = g[..., 0:1] * o_cmp + g[..., 1:2] * o_sel + g[..., 2:3] * o_win
    return o.reshape(B, Q, Q_DIM) @ w_out


def nsa_prompt(h, w_in, w_out, pe_k, w1_k, w2_k, pe_v, w1_v, w2_v):
    B, S, _ = h.shape
    pos = jnp.arange(S)
    q_raw, q_rot, kc, vc, ks, vs, kw, vw, gates = nsa_project(h, w_in, pos)
    kc_c = compress(kc, pe_k, w1_k, w2_k)
    vc_c = compress(vc, pe_v, w1_v, w2_v)
    n_sel = -(-S // SEL_BLOCK)
    ks_blk = to_blocks(ks, n_sel)
    vs_blk = to_blocks(vs, n_sel)
    pad = ((0, 0), (WINDOW, 0), (0, 0), (0, 0))
    kw_pad = jnp.pad(kw, pad)
    vw_pad = jnp.pad(vw, pad)
    nqb = S // Q_BLOCK
    blocks = lambda t: t.reshape(B, nqb, Q_BLOCK, N_KV_HEADS, GQA_GROUP, HEAD_DIM).swapaxes(0, 1)

    def body(args):
        c, qr, qo = args
        start = c * Q_BLOCK
        q_pos = start + jnp.arange(Q_BLOCK)
        kwb = lax.dynamic_slice_in_dim(kw_pad, start, WINDOW + Q_BLOCK, axis=1)
        vwb = lax.dynamic_slice_in_dim(vw_pad, start, WINDOW + Q_BLOCK, axis=1)
        kw_pos = start - WINDOW + jnp.arange(WINDOW + Q_BLOCK)
        return nsa_branches(qr, qo, q_pos, kc_c, vc_c, ks_blk, vs_blk, kwb, vwb, kw_pos)

    outs = lax.map(body, (jnp.arange(nqb), blocks(q_raw), blocks(q_rot)))
    unblock = lambda o: o.swapaxes(0, 1).reshape(B, S, N_KV_HEADS, GQA_GROUP, HEAD_DIM)
    y = nsa_combine(unblock(outs[0]), unblock(outs[1]), unblock(outs[2]), gates, w_out)
    keep = min(WINDOW, S)
    return y, (kc, vc, ks, vs, kw[:, -keep:], vw[:, -keep:])


def nsa_sample(h, past_ck, past_cv, past_sk, past_sv, win_k, win_v,
               w_in, w_out, pe_k, w1_k, w2_k, pe_v, w1_v, w2_v):
    B, S, _ = h.shape
    past = past_ck.shape[1]
    pos = past + jnp.arange(S)
    q_raw, q_rot, kc, vc, ks, vs, kw, vw, gates = nsa_project(h, w_in, pos)
    kc_c = compress(jnp.concatenate([past_ck, kc], axis=1), pe_k, w1_k, w2_k)
    vc_c = compress(jnp.concatenate([past_cv, vc], axis=1), pe_v, w1_v, w2_v)
    n_sel = -(-(past + S) // SEL_BLOCK)
    ks_blk = to_blocks(jnp.concatenate([past_sk, ks], axis=1), n_sel)
    vs_blk = to_blocks(jnp.concatenate([past_sv, vs], axis=1), n_sel)
    kw_all = jnp.concatenate([win_k, kw], axis=1)
    vw_all = jnp.concatenate([win_v, vw], axis=1)
    kw_pos = past - win_k.shape[1] + jnp.arange(kw_all.shape[1])
    o_cmp, o_sel, o_win = nsa_branches(q_raw, q_rot, pos, kc_c, vc_c, ks_blk, vs_blk, kw_all, vw_all, kw_pos)
    y = nsa_combine(o_cmp, o_sel, o_win, gates, w_out)
    keep = min(WINDOW, kw_all.shape[1])
    return y, (kc, vc, ks, vs, kw_all[:, -keep:], vw_all[:, -keep:])


def setup_inputs(seed: int = 0) -> dict:
    key = jax.random.key(seed)
    k = jax.random.split(key, 32)
    f32 = jnp.float32
    n_pages = PAST_LEN // PAGE_SIZE
    n_pool_pages = (DEC_BATCH * n_pages * 5 + 3) // 4
    win_buf = min(WINDOW, PAST_LEN)
    nrm = lambda kk, shape, s: s * jax.random.normal(kk, shape, f32)
    gain = lambda kk, shape: 1.0 + 0.02 * jax.random.normal(kk, shape, f32)
    paged = (N_NSA_LAYERS, n_pool_pages, PAGE_SIZE, N_KV_HEADS, HEAD_DIM)
    win = (N_NSA_LAYERS, DEC_BATCH, win_buf, N_KV_HEADS, HEAD_DIM)
    page_table = jax.random.permutation(k[9], n_pool_pages)[:DEC_BATCH * n_pages]
    return {
        "x_prompt": nrm(k[0], (BATCH, SEQ, D_MODEL), 1.0),
        "x_sample": nrm(k[1], (DEC_BATCH, DEC_SEQ, D_MODEL), 1.0),
        "state_pool": nrm(k[2], (N_POOL_LAYERS, DEC_BATCH, POOL_STATE_LEN, D_MODEL), 1.0),
        "cache_cmp_k": nrm(k[3], paged, 1.0),
        "cache_cmp_v": nrm(k[4], paged, 1.0),
        "cache_sel_k": nrm(k[5], paged, 1.0),
        "cache_sel_v": nrm(k[6], paged, 1.0),
        "state_win_k": nrm(k[7], win, 1.0),
        "state_win_v": nrm(k[8], win, 1.0),
        "page_table": page_table.reshape(DEC_BATCH, n_pages).astype(jnp.int32),
        "norm_ffn1": gain(k[10], (DEPTH, D_MODEL)),
        "norm_mix": gain(k[11], (DEPTH, D_MODEL)),
        "norm_ffn2": gain(k[12], (DEPTH, D_MODEL)),
        "norm_final": gain(k[13], (D_MODEL,)),
        "w_ffn_gate": nrm(k[14], (DEPTH, 2, D_MODEL, D_FF), D_MODEL ** -0.5),
        "w_ffn_up": nrm(k[15], (DEPTH, 2, D_MODEL, D_FF), D_MODEL ** -0.5),
        "w_ffn_down": nrm(k[16], (DEPTH, 2, D_FF, D_MODEL), D_FF ** -0.5),
        "w_pool": nrm(k[17], (N_POOL_LAYERS, N_POOL_GROUPS, POOL_GROUP_DIM, POOL_GROUP_DIM), POOL_GROUP_DIM ** -0.5),
        "pool_scale": gain(k[18], (N_POOL_LAYERS, D_MODEL)),
        "w_nsa_in": nrm(k[19], (N_NSA_LAYERS, D_MODEL, IN_COLS), D_MODEL ** -0.5),
        "w_nsa_out": nrm(k[20], (N_NSA_LAYERS, Q_DIM, D_MODEL), Q_DIM ** -0.5),
        "cmp_pe_k": nrm(k[21], (N_NSA_LAYERS, CMP_BLOCK, HEAD_DIM), 0.5),
        "cmp_w1_k": nrm(k[22], (N_NSA_LAYERS, CMP_BLOCK, HEAD_DIM, HEAD_DIM), (CMP_BLOCK * HEAD_DIM) ** -0.5),
        "cmp_w2_k": nrm(k[23], (N_NSA_LAYERS, HEAD_DIM, HEAD_DIM), HEAD_DIM ** -0.5),
        "cmp_pe_v": nrm(k[24], (N_NSA_LAYERS, CMP_BLOCK, HEAD_DIM), 0.5),
        "cmp_w1_v": nrm(k[25], (N_NSA_LAYERS, CMP_BLOCK, HEAD_DIM, HEAD_DIM), (CMP_BLOCK * HEAD_DIM) ** -0.5),
        "cmp_w2_v": nrm(k[26], (N_NSA_LAYERS, HEAD_DIM, HEAD_DIM), HEAD_DIM ** -0.5),
    }


def reference(x_prompt, x_sample, state_pool, cache_cmp_k, cache_cmp_v, cache_sel_k, cache_sel_v,
              state_win_k, state_win_v, page_table, norm_ffn1, norm_mix, norm_ffn2, norm_final,
              w_ffn_gate, w_ffn_up, w_ffn_down, w_pool, pool_scale, w_nsa_in, w_nsa_out,
              cmp_pe_k, cmp_w1_k, cmp_w2_k, cmp_pe_v, cmp_w1_v, cmp_w2_v):
    past_len = page_table.shape[1] * PAGE_SIZE

    def paged_rows(cache, li):
        return cache[li, page_table].reshape(page_table.shape[0], past_len, N_KV_HEADS, HEAD_DIM)

    def trunk(x, prompt):
        pool_states, nsa_states = [], []
        for i in range(DEPTH):
            x = x + 0.5 * swiglu(rms_norm(x, norm_ffn1[i]), w_ffn_gate[i, 0], w_ffn_up[i, 0], w_ffn_down[i, 0])
            h = rms_norm(x, norm_mix[i])
            li = i // N_MIXERS
            if i % N_MIXERS == 0:
                if prompt:
                    y, st = pool_mixer(h, h[:, :0], 0, w_pool[li], pool_scale[li])
                else:
                    y, st = pool_mixer(h, state_pool[li], past_len, w_pool[li], pool_scale[li])
                pool_states.append(st)
            else:
                wts = (w_nsa_in[li], w_nsa_out[li], cmp_pe_k[li], cmp_w1_k[li], cmp_w2_k[li],
                       cmp_pe_v[li], cmp_w1_v[li], cmp_w2_v[li])
                if prompt:
                    y, st = nsa_prompt(h, *wts)
                else:
                    y, st = nsa_sample(h, paged_rows(cache_cmp_k, li), paged_rows(cache_cmp_v, li),
                                       paged_rows(cache_sel_k, li), paged_rows(cache_sel_v, li),
                                       state_win_k[li], state_win_v[li], *wts)
                nsa_states.append(st)
            x = x + y
            x = x + 0.5 * swiglu(rms_norm(x, norm_ffn2[i]), w_ffn_gate[i, 1], w_ffn_up[i, 1], w_ffn_down[i, 1])
        return rms_norm(x, norm_final), pool_states, nsa_states

    y_prompt, pool_list_p, nsa_list_p = trunk(x_prompt, True)
    y_sample, pool_list_s, nsa_list_s = trunk(x_sample, False)
    pool_p = jnp.stack(pool_list_p)
    pool_s = jnp.stack(pool_list_s)
    cmp_k_p, cmp_v_p, sel_k_p, sel_v_p, win_k_p, win_v_p = [jnp.stack([s[j] for s in nsa_list_p]) for j in range(6)]
    cmp_k_s, cmp_v_s, sel_k_s, sel_v_s, win_k_s, win_v_s = [jnp.stack([s[j] for s in nsa_list_s]) for j in range(6)]
    return (y_prompt, y_sample, pool_p, pool_s, cmp_k_p, cmp_k_s, cmp_v_p, cmp_v_s,
            sel_k_p, sel_k_s, sel_v_p, sel_v_s, win_k_p, win_k_s, win_v_p, win_v_s)
```

```python
import functools

import jax
import jax.numpy as jnp
from jax import lax
from jax.experimental import pallas as pl
from jax.experimental.pallas import tpu as pltpu

F32 = jnp.float32
BF16 = jnp.bfloat16

D_MODEL = 2048
D_FF = 5632
POOL_WINDOWS = (2, 4, 8, 16)
POOL_GROUP_DIM = D_MODEL // len(POOL_WINDOWS)
POOL_STATE_LEN = max(POOL_WINDOWS) - 1
N_HEADS = 16
HEAD_DIM = 128
N_KV_HEADS = 4
GQA_GROUP = N_HEADS // N_KV_HEADS
Q_DIM = N_HEADS * HEAD_DIM
KV_DIM = N_KV_HEADS * HEAD_DIM
N_BRANCHES = 3
CMP_BLOCK = 32
CMP_STRIDE = 16
SEL_BLOCK = 64
SEL_TOPK = 16
WINDOW = 512
PAGE_SIZE = 128
SCALE = HEAD_DIM ** -0.5
ROPE_THETA = 10000.0
RMS_EPS = 1e-6
NEG_INF = -1e30
FORCE_SCORE = 1e9
TINY = 1e-30
M_INIT = -1e29

LANES = 128
SUBLANES = 8
VMEM_LIMIT = 56 * 1024 * 1024


def _cparams(*sem):
    return pltpu.CompilerParams(dimension_semantics=sem, vmem_limit_bytes=VMEM_LIMIT)


def _rms(x, g):
    y = x * lax.rsqrt(jnp.mean(x * x, axis=-1, keepdims=True) + RMS_EPS)
    return y * g


def _dot(a, b):
    return jnp.dot(a, b, preferred_element_type=F32)


def _dot_nt(a, b):
    return lax.dot_general(a, b, (((1,), (1,)), ((), ())), preferred_element_type=F32)


def _ffn_body(x_ref, g_ref, wg_ref, wu_ref, wd_ref, *rest, final_norm):
    if final_norm:
        gf_ref, o_ref, xn_ref = rest
    else:
        o_ref, xn_ref = rest
    f = pl.program_id(1)

    @pl.when(f == 0)
    def _():
        x = x_ref[...]
        xn_ref[...] = _rms(x, g_ref[...]).astype(BF16)
        o_ref[...] = x

    xn = xn_ref[...]
    gate = _dot(xn, wg_ref[...])
    up = _dot(xn, wu_ref[...])
    h = (gate * jax.nn.sigmoid(gate)) * up
    o_ref[...] += _dot((0.5 * h).astype(BF16), wd_ref[...])

    if final_norm:
        @pl.when(f == pl.num_programs(1) - 1)
        def _():
            o_ref[...] = _rms(o_ref[...], gf_ref[...])


def _ffn(x, g, wg, wu, wd, layer, which, g_final=None, tf=512):
    m = x.shape[0]
    tm = min(m, 1024)
    final_norm = g_final is not None
    row = lambda i, f: (i, 0)
    in_specs = [
        pl.BlockSpec((tm, D_MODEL), row),
        pl.BlockSpec((1, D_MODEL), lambda i, f: (0, 0)),
        pl.BlockSpec((None, None, D_MODEL, tf), lambda i, f: (layer, which, 0, f)),
        pl.BlockSpec((None, None, D_MODEL, tf), lambda i, f: (layer, which, 0, f)),
        pl.BlockSpec((None, None, tf, D_MODEL), lambda i, f: (layer, which, f, 0)),
    ]
    args = [x, g, wg, wu, wd]
    if final_norm:
        in_specs.append(pl.BlockSpec((1, D_MODEL), lambda i, f: (0, 0)))
        args.append(g_final)
    return pl.pallas_call(
        functools.partial(_ffn_body, final_norm=final_norm),
        grid=(m // tm, D_FF // tf),
        in_specs=in_specs,
        out_specs=pl.BlockSpec((tm, D_MODEL), row),
        out_shape=jax.ShapeDtypeStruct((m, D_MODEL), F32),
        scratch_shapes=[pltpu.VMEM((tm, D_MODEL), BF16)],
        compiler_params=_cparams("parallel", "arbitrary"),
        name="ffn",
    )(*args)


def _pool_groups(h_cols, x, w_ref, sc_ref, o_ref, window_sum, inv_cnt):
    for gi, w in enumerate(POOL_WINDOWS):
        c0 = gi * POOL_GROUP_DIM
        cols = slice(c0, c0 + POOL_GROUP_DIM)
        hc = h_cols(cols)
        d = window_sum(w, cols, hc) * inv_cnt(w) - hc
        y = _dot(d.astype(BF16), w_ref[gi])
        o_ref[:, cols] = x[:, cols] + y * sc_ref[:, cols]


def _pool_prompt_body(x_ref, g_ref, w_ref, sc_ref, o_ref, st_ref, hbuf, *, ts):
    s = pl.program_id(1)
    halo = 2 * SUBLANES

    @pl.when(s == 0)
    def _():
        hbuf[0:halo, :] = jnp.zeros((halo, D_MODEL), F32)

    @pl.when(s > 0)
    def _():
        hbuf[0:halo, :] = hbuf[ts:ts + halo, :]

    x = x_ref[...]
    hbuf[halo:halo + ts, :] = _rms(x, g_ref[...])
    pos = s * ts + lax.broadcasted_iota(jnp.int32, (ts, 1), 0)

    def window_sum(w, cols, hc):
        acc = hc
        for k in range(1, w):
            acc = acc + hbuf[halo - k:halo - k + ts, cols]
        return acc

    _pool_groups(lambda cols: hbuf[halo:halo + ts, cols], x, w_ref, sc_ref, o_ref, window_sum,
                 lambda w: 1.0 / jnp.minimum(w, pos + 1).astype(F32))

    @pl.when(s == pl.num_programs(1) - 1)
    def _():
        st_ref[0] = hbuf[halo + ts - POOL_STATE_LEN:halo + ts, :]


def _pool_prompt(x, g, w_pool, scale, batch, seq, ts=256):
    nt = seq // ts
    return pl.pallas_call(
        functools.partial(_pool_prompt_body, ts=ts),
        grid=(batch, nt),
        in_specs=[
            pl.BlockSpec((ts, D_MODEL), lambda b, s: (b * nt + s, 0)),
            pl.BlockSpec((1, D_MODEL), lambda b, s: (0, 0)),
            pl.BlockSpec((len(POOL_WINDOWS), POOL_GROUP_DIM, POOL_GROUP_DIM), lambda b, s: (0, 0, 0)),
            pl.BlockSpec((1, D_MODEL), lambda b, s: (0, 0)),
        ],
        out_specs=[
            pl.BlockSpec((ts, D_MODEL), lambda b, s: (b * nt + s, 0)),
            pl.BlockSpec((1, POOL_STATE_LEN, D_MODEL), lambda b, s: (b, 0, 0)),
        ],
        out_shape=[
            jax.ShapeDtypeStruct((batch * seq, D_MODEL), F32),
            jax.ShapeDtypeStruct((batch, POOL_STATE_LEN, D_MODEL), F32),
        ],
        scratch_shapes=[pltpu.VMEM((ts + 2 * SUBLANES, D_MODEL), F32)],
        compiler_params=_cparams("arbitrary", "arbitrary"),
        name="pool_prompt",
    )(x, g, w_pool, scale)


def _pool_sample_body(x_ref, st_ref, g_ref, w_ref, sc_ref, o_ref, h_ref):
    x = x_ref[...]
    h_ref[...] = _rms(x, g_ref[...])

    def window_sum(w, cols, hc):
        acc = hc
        for k in range(1, w):
            acc = acc + st_ref[POOL_STATE_LEN - k, :, cols]
        return acc

    _pool_groups(lambda cols: h_ref[:, cols], x, w_ref, sc_ref, o_ref, window_sum, lambda w: 1.0 / w)


def _pool_sample(x, state_t, g, w_pool, scale, tb=32):
    b = x.shape[0]
    return pl.pallas_call(
        _pool_sample_body,
        grid=(b // tb,),
        in_specs=[
            pl.BlockSpec((tb, D_MODEL), lambda i: (i, 0)),
            pl.BlockSpec((POOL_STATE_LEN, tb, D_MODEL), lambda i: (0, i, 0)),
            pl.BlockSpec((1, D_MODEL), lambda i: (0, 0)),
            pl.BlockSpec((len(POOL_WINDOWS), POOL_GROUP_DIM, POOL_GROUP_DIM), lambda i: (0, 0, 0)),
            pl.BlockSpec((1, D_MODEL), lambda i: (0, 0)),
        ],
        out_specs=[pl.BlockSpec((tb, D_MODEL), lambda i: (i, 0)), pl.BlockSpec((tb, D_MODEL), lambda i: (i, 0))],
        out_shape=[jax.ShapeDtypeStruct((b, D_MODEL), F32), jax.ShapeDtypeStruct((b, D_MODEL), F32)],
        compiler_params=_cparams("parallel"),
        name="pool_sample",
    )(x, state_t, g, w_pool, scale)


N_PROJ_TILES = (Q_DIM + 6 * KV_DIM) // KV_DIM


def _proj_body(x_ref, g_ref, w_ref, wg_ref, cos_ref, sin_ref,
               qraw_ref, qrot_ref, kc_ref, vc_ref, ks_ref, vs_ref, kw_ref, vw_ref,
               kshm_ref, vshm_ref, kwhm_ref, vwhm_ref, gates_ref, xn_ref, *, tm):
    j = pl.program_id(1)

    @pl.when(j == 0)
    def _():
        xn = _rms(x_ref[...], g_ref[...]).astype(BF16)
        xn_ref[...] = xn
        for kvh in range(N_KV_HEADS):
            gates_ref[kvh] = jax.nn.sigmoid(_dot(xn, wg_ref[kvh]))

    z = _dot(xn_ref[...], w_ref[...])
    heads = lambda a: [a[:, h * HEAD_DIM:(h + 1) * HEAD_DIM] for h in range(N_KV_HEADS)]

    def rope(slabs):
        return [s * cos_ref[...] + pltpu.roll(s, HEAD_DIM // 2, 1) * sin_ref[...] for s in slabs]

    @pl.when(j < Q_DIM // KV_DIM)
    def _():
        qraw_ref[...] = z.astype(BF16)
        for h, r in enumerate(rope(heads(z))):
            qrot_ref[:, h * HEAD_DIM:(h + 1) * HEAD_DIM] = r.astype(BF16)

    def store(ref, hm_ref, slabs):
        for h, s in enumerate(slabs):
            ref[pl.ds(h, tm, stride=N_KV_HEADS), :] = s
            if hm_ref is not None:
                hm_ref[h] = s.astype(BF16)

    first = Q_DIM // KV_DIM
    for off, (ref, hm_ref, rot) in enumerate([(kc_ref, None, False), (vc_ref, None, False), (ks_ref, kshm_ref, True),
                                              (vs_ref, vshm_ref, False), (kw_ref, kwhm_ref, True),
                                              (vw_ref, vwhm_ref, False)]):
        @pl.when(j == first + off)
        def _(ref=ref, hm_ref=hm_ref, rot=rot):
            slabs = heads(z)
            store(ref, hm_ref, rope(slabs) if rot else slabs)


def _nsa_project(x, g, w_main, w_gates, cos, sin, rope_period_tiles, tm):
    m = x.shape[0]
    nq = Q_DIM // KV_DIM
    row = lambda i, j: (i, 0)
    kv_sds = jax.ShapeDtypeStruct((m * N_KV_HEADS, HEAD_DIM), F32)
    hm_sds = jax.ShapeDtypeStruct((N_KV_HEADS, m, HEAD_DIM), BF16)
    q_sds = jax.ShapeDtypeStruct((m, Q_DIM), BF16)
    kv_spec = pl.BlockSpec((tm * N_KV_HEADS, HEAD_DIM), row)
    hm_spec = pl.BlockSpec((N_KV_HEADS, tm, HEAD_DIM), lambda i, j: (0, i, 0))
    q_spec = pl.BlockSpec((tm, KV_DIM), lambda i, j: (i, jnp.minimum(j, nq - 1)))
    return pl.pallas_call(
        functools.partial(_proj_body, tm=tm),
        grid=(m // tm, N_PROJ_TILES),
        in_specs=[
            pl.BlockSpec((tm, D_MODEL), row),
            pl.BlockSpec((1, D_MODEL), lambda i, j: (0, 0)),
            pl.BlockSpec((D_MODEL, KV_DIM), lambda i, j: (0, j)),
            pl.BlockSpec((N_KV_HEADS, D_MODEL, LANES), lambda i, j: (0, 0, 0)),
            pl.BlockSpec((tm, HEAD_DIM), lambda i, j: (i % rope_period_tiles, 0)),
            pl.BlockSpec((tm, HEAD_DIM), lambda i, j: (i % rope_period_tiles, 0)),
        ],
        out_specs=[q_spec, q_spec] + [kv_spec] * 6 + [hm_spec] * 4
        + [pl.BlockSpec((N_KV_HEADS, tm, LANES), lambda i, j: (0, i, 0))],
        out_shape=[q_sds, q_sds] + [kv_sds] * 6 + [hm_sds] * 4
        + [jax.ShapeDtypeStruct((N_KV_HEADS, m, LANES), F32)],
        scratch_shapes=[pltpu.VMEM((tm, D_MODEL), BF16)],
        compiler_params=_cparams("parallel", "arbitrary"),
        name="nsa_project",
    )(x, g, w_main, w_gates, cos, sin)


N_PAIR = CMP_STRIDE // 2


def _compress(get_rows, n_rows, wpair_ref, pepair_ref, w2_ref):
    even = (lax.broadcasted_iota(jnp.int32, (n_rows, 1), 0) & (SUBLANES - 1)) < N_KV_HEADS
    acc = jnp.zeros((n_rows, 2 * HEAD_DIM), F32)
    bias = jnp.zeros((SUBLANES, 2 * HEAD_DIM), F32)
    for k in range(N_PAIR):
        v = get_rows(k)
        a2 = jnp.concatenate([jnp.where(even, v, 0.0), jnp.where(even, 0.0, v)], axis=1).astype(BF16)
        acc = acc + _dot(a2, wpair_ref[k])
        bias = bias + _dot(pepair_ref[k].astype(BF16), wpair_ref[k])
    tot = acc + pltpu.roll(acc, N_KV_HEADS, 0)
    first = tot[:, :HEAD_DIM] + bias[0:1, :HEAD_DIM]
    second = tot[:, HEAD_DIM:] + bias[1:2, HEAD_DIM:]
    pre = first + pltpu.roll(second, n_rows - SUBLANES, 0)
    hid = jax.nn.gelu(pre, approximate=True).astype(BF16)
    return _dot(hid, w2_ref[...])


def _compress_body(x_ref, wpair_ref, pepair_ref, w2_ref, o_ref, res_ref, *, n_sub):
    n_rows = n_sub * SUBLANES
    get_rows = lambda k: x_ref[:, k * SUBLANES:(k + 1) * SUBLANES, :].reshape(n_rows, HEAD_DIM)
    res_ref[...] = _compress(get_rows, n_rows, wpair_ref, pepair_ref, w2_ref)
    for h in range(N_KV_HEADS):
        o_ref[0, h] = res_ref[pl.ds(N_KV_HEADS + h, n_sub, stride=SUBLANES), :].astype(BF16)


def _compress_prompt(rows, wpair, pepair, w2, batch, seq):
    n_sub = seq // CMP_STRIDE
    chunk = CMP_STRIDE * N_KV_HEADS
    x3 = rows.reshape(batch * n_sub, chunk, HEAD_DIM)
    return pl.pallas_call(
        functools.partial(_compress_body, n_sub=n_sub),
        grid=(batch,),
        in_specs=[
            pl.BlockSpec((n_sub, chunk, HEAD_DIM), lambda b: (b, 0, 0)),
            pl.BlockSpec((N_PAIR, 2 * HEAD_DIM, 2 * HEAD_DIM), lambda b: (0, 0, 0)),
            pl.BlockSpec((N_PAIR, SUBLANES, 2 * HEAD_DIM), lambda b: (0, 0, 0)),
            pl.BlockSpec((HEAD_DIM, HEAD_DIM), lambda b: (0, 0)),
        ],
        out_specs=pl.BlockSpec((1, N_KV_HEADS, n_sub, HEAD_DIM), lambda b: (b, 0, 0, 0)),
        out_shape=jax.ShapeDtypeStruct((batch, N_KV_HEADS, n_sub, HEAD_DIM), BF16),
        scratch_shapes=[pltpu.VMEM((n_sub * SUBLANES, HEAD_DIM), F32)],
        compiler_params=_cparams("parallel"),
        name="compress_prompt",
    )(x3, wpair, pepair, w2)


def _compress_weights(pe, w1, w2):
    w1r = w1.reshape(2, N_PAIR, 2, HEAD_DIM, HEAD_DIM)
    wpair = jnp.transpose(w1r, (1, 2, 3, 0, 4)).reshape(N_PAIR, 2 * HEAD_DIM, 2 * HEAD_DIM).astype(BF16)
    per = pe.reshape(2, N_PAIR, 2 * HEAD_DIM)
    pepair = jnp.zeros((N_PAIR, SUBLANES, 2 * HEAD_DIM), F32).at[:, 0:2, :].set(jnp.transpose(per, (1, 0, 2)))
    return wpair, pepair, w2.astype(BF16)


def _topk_select(score, valid, n_blk):
    lane = lax.broadcasted_iota(jnp.int32, score.shape, 1)
    rank = jnp.zeros(score.shape, F32)
    for i in range(n_blk):
        ci = score[:, i:i + 1]
        beats = (ci > score) | ((ci == score) & (lane > i))
        rank = rank + jnp.where(beats, 1.0, 0.0)
    return (rank < SEL_TOPK) & valid


def _split_dot(p, w):
    hi = p.astype(BF16)
    lo = (p - hi.astype(F32)).astype(BF16)
    return _dot(hi, w) + _dot(lo, w)


def _lanes(a, n):
    return a if n == LANES else jnp.concatenate([a] * (n // LANES), axis=1)


def _attn_prompt_body(qraw_ref, qrot_ref, kcc_ref, vcc_ref, ks_ref, vs_ref, kw_ref, vw_ref, gates_ref, ov_ref,
                      ex_ref, o_ref, bias_ref, m_ref, l_ref, acc_ref, *, tq, kt, ktw, seq):
    qi = pl.program_id(2)
    q0 = qi * tq
    n_blk = seq // SEL_BLOCK
    rows = GQA_GROUP * tq
    qpos = q0 + lax.broadcasted_iota(jnp.int32, (tq, 1), 0)
    qpos4 = jnp.concatenate([qpos] * GQA_GROUP, axis=0)
    stack = lambda ref: jnp.concatenate([ref[:, g * HEAD_DIM:(g + 1) * HEAD_DIM] for g in range(GQA_GROUP)], axis=0)
    q_raw = stack(qraw_ref)
    q_rot = stack(qrot_ref)

    n_cmp = seq // CMP_STRIDE
    sc = _dot_nt(q_raw, kcc_ref[0, 0]) * SCALE
    c_end = lax.broadcasted_iota(jnp.int32, (1, n_cmp), 1) * CMP_STRIDE + (CMP_BLOCK - 1)
    cmask = c_end <= qpos4
    scm = jnp.where(cmask, sc, NEG_INF)
    e = jnp.where(cmask, jnp.exp(scm - jnp.max(scm, axis=-1, keepdims=True)), 0.0)
    pc = e / jnp.maximum(jnp.sum(e, axis=-1, keepdims=True), TINY)
    o_cmp = _dot(pc.astype(BF16), vcc_ref[0, 0])

    psum = pc[0:tq] + pc[tq:2 * tq] + pc[2 * tq:3 * tq] + pc[3 * tq:4 * tq]
    imp = _split_dot(psum, ov_ref[...])
    blk = lax.broadcasted_iota(jnp.int32, (tq, LANES), 1)
    cur = lax.shift_right_logical(qpos, 6)
    valid = (blk * SEL_BLOCK <= qpos) & (blk < n_blk)
    forced = (blk == 0) | (blk == cur) | (blk == cur - 1)
    score = jnp.where(valid, jnp.where(forced, FORCE_SCORE, imp), NEG_INF)
    sel = jnp.where(_topk_select(score, valid, n_blk), 1.0, 0.0).astype(BF16)

    for c in range(seq // kt):
        chosen = _dot(sel, ex_ref[:, c * kt:(c + 1) * kt])
        kpos = c * kt + lax.broadcasted_iota(jnp.int32, (1, kt), 1)
        bias_ref[c] = jnp.where((chosen > 0.5) & (kpos <= qpos), 0.0, NEG_INF)

    def reset():
        m_ref[...] = jnp.full((rows, LANES), M_INIT, F32)
        l_ref[...] = jnp.zeros((rows, LANES), F32)
        acc_ref[...] = jnp.zeros((rows, HEAD_DIM), F32)

    def online_step(k, v, bias, width):
        s = _dot_nt(q_rot, k) * SCALE + jnp.concatenate([bias] * GQA_GROUP, axis=0)
        m_prev = m_ref[...]
        m_new = jnp.maximum(m_prev, jnp.max(s, axis=-1, keepdims=True))
        alpha = jnp.exp(m_prev - m_new)
        p = jnp.exp(s - _lanes(m_new, width))
        l_ref[...] = alpha * l_ref[...] + jnp.sum(p, axis=-1, keepdims=True)
        acc_ref[...] = alpha * acc_ref[...] + _dot(p.astype(BF16), v)
        m_ref[...] = m_new

    reset()

    def sel_step(c, carry):
        start = pl.multiple_of(c * kt, kt)
        online_step(ks_ref[0, pl.ds(start, kt), :], vs_ref[0, pl.ds(start, kt), :], bias_ref[c], kt)
        return carry

    lax.fori_loop(0, (q0 + tq + kt - 1) // kt, sel_step, 0)
    o_sel = acc_ref[...] / l_ref[...]

    reset()

    def win_step(c, carry):
        start = pl.multiple_of(c * ktw, ktw)
        kpos = start + lax.broadcasted_iota(jnp.int32, (1, ktw), 1)
        bias = jnp.where((kpos <= qpos) & (kpos > qpos - WINDOW), 0.0, NEG_INF)
        online_step(kw_ref[0, pl.ds(start, ktw), :], vw_ref[0, pl.ds(start, ktw), :], bias, ktw)
        return carry

    lax.fori_loop(jnp.maximum(q0 - WINDOW, 0) // ktw, (q0 + tq) // ktw, win_step, 0)
    o_win = acc_ref[...] / l_ref[...]

    gates = gates_ref[0]
    for g in range(GQA_GROUP):
        r = slice(g * tq, (g + 1) * tq)
        gate = lambda br: gates[:, br * GQA_GROUP + g:br * GQA_GROUP + g + 1]
        o = gate(0) * o_cmp[r] + gate(1) * o_sel[r] + gate(2) * o_win[r]
        o_ref[:, g * HEAD_DIM:(g + 1) * HEAD_DIM] = o.astype(BF16)


def _attn_prompt(q_raw, q_rot, kcc, vcc, ks_hm, vs_hm, kw_hm, vw_hm, gates, ov, ex, batch, seq, tq=128, kt=256,
                 ktw=128):
    nq = seq // tq
    n_cmp = seq // CMP_STRIDE
    qspec = pl.BlockSpec((tq, KV_DIM), lambda b, h, i: (b * nq + i, h))
    cspec = pl.BlockSpec((1, 1, n_cmp, HEAD_DIM), lambda b, h, i: (b, h, 0, 0))
    kvspec = pl.BlockSpec((1, seq, HEAD_DIM), lambda b, h, i: (h, b, 0))
    return pl.pallas_call(
        functools.partial(_attn_prompt_body, tq=tq, kt=kt, ktw=ktw, seq=seq),
        grid=(batch, N_KV_HEADS, nq),
        in_specs=[qspec, qspec, cspec, cspec, kvspec, kvspec, kvspec, kvspec,
                  pl.BlockSpec((1, tq, LANES), lambda b, h, i: (h, b * nq + i, 0)),
                  pl.BlockSpec((n_cmp, LANES), lambda b, h, i: (0, 0)),
                  pl.BlockSpec((LANES, seq), lambda b, h, i: (0, 0))],
        out_specs=qspec,
        out_shape=jax.ShapeDtypeStruct((batch * seq, Q_DIM), BF16),
        scratch_shapes=[pltpu.VMEM((seq // kt, tq, kt), F32), pltpu.VMEM((GQA_GROUP * tq, LANES), F32),
                        pltpu.VMEM((GQA_GROUP * tq, LANES), F32), pltpu.VMEM((GQA_GROUP * tq, HEAD_DIM), F32)],
        compiler_params=_cparams("parallel", "parallel", "arbitrary"),
        name="attn_prompt",
    )(q_raw, q_rot, kcc, vcc, ks_hm, vs_hm, kw_hm, vw_hm, gates, ov, ex)


def _matmul_res_body(a_ref, w_ref, r_ref, o_ref):
    o_ref[...] = r_ref[...] + _dot(a_ref[...], w_ref[...])


def _matmul_residual(a, w, res, tm):
    m, k = a.shape
    n = w.shape[1]
    return pl.pallas_call(
        _matmul_res_body,
        grid=(m // tm,),
        in_specs=[pl.BlockSpec((tm, k), lambda i: (i, 0)), pl.BlockSpec((k, n), lambda i: (0, 0)),
                  pl.BlockSpec((tm, n), lambda i: (i, 0))],
        out_specs=pl.BlockSpec((tm, n), lambda i: (i, 0)),
        out_shape=jax.ShapeDtypeStruct((m, n), F32),
        compiler_params=_cparams("parallel"),
        name="out_proj",
    )(a, w, res)


def _attn_sample_body(pt_ref, *refs, n_pages, past):
    del pt_ref
    pages = refs[:4 * n_pages]
    (wink_ref, winv_ref, qraw_ref, qrot_ref, kse_ref, vse_ref, kwe_ref, vwe_ref, gates_ref,
     wpk_ref, pek_ref, w2k_ref, wpv_ref, pev_ref, w2v_ref, ov_ref, gs_ref, o_ref, s_ref) = refs[4 * n_pages:]
    ck, cv, sk, sv = (pages[i * n_pages:(i + 1) * n_pages] for i in range(4))
    n_sub = past // CMP_STRIDE
    n_rows = n_sub * SUBLANES
    sub_pp = PAGE_SIZE // CMP_STRIDE
    page_rows = PAGE_SIZE * N_KV_HEADS

    def rows_of(pg):
        return lambda k: jnp.concatenate(
            [r[0, :, k * SUBLANES:(k + 1) * SUBLANES, :].reshape(sub_pp * SUBLANES, HEAD_DIM) for r in pg], axis=0)

    kcc = _compress(rows_of(ck), n_rows, wpk_ref, pek_ref, w2k_ref).astype(BF16)
    vcc = _compress(rows_of(cv), n_rows, wpv_ref, pev_ref, w2v_ref).astype(BF16)

    q_raw = qraw_ref[0]
    q_rot = qrot_ref[0]
    q_rot32 = q_rot.astype(F32)
    hrow = lax.shift_right_logical(lax.broadcasted_iota(jnp.int32, (N_HEADS, 1), 0), 2)

    def softmax_parts(s_old, s_new):
        m = jnp.maximum(jnp.max(s_old, axis=-1, keepdims=True), s_new)
        p_old = jnp.exp(s_old - m)
        p_new = jnp.exp(s_new - m)
        return p_old, p_new, jnp.sum(p_old, axis=-1, keepdims=True) + p_new

    col = lax.broadcasted_iota(jnp.int32, (1, n_rows), 1)
    cmask = ((col & (SUBLANES - 1)) == hrow + N_KV_HEADS) & (
        lax.shift_right_logical(col, 3) * CMP_STRIDE + (CMP_BLOCK - 1) <= past)
    scm = jnp.where(cmask, _dot_nt(q_raw, kcc) * SCALE, NEG_INF)
    e = jnp.where(cmask, jnp.exp(scm - jnp.max(scm, axis=-1, keepdims=True)), 0.0)
    pc = e / jnp.maximum(jnp.sum(e, axis=-1, keepdims=True), TINY)
    o_cmp = _dot(pc.astype(BF16), vcc)

    imp16 = _split_dot(pc, ov_ref[...])
    hi = imp16.astype(BF16)
    lo = (imp16 - hi.astype(F32)).astype(BF16)
    imp = _dot(gs_ref[...], hi) + _dot(gs_ref[...], lo)
    n_blk = past // SEL_BLOCK + 1
    cur = past // SEL_BLOCK
    blk = lax.broadcasted_iota(jnp.int32, (N_HEADS, LANES), 1)
    valid = blk < n_blk
    forced = (blk == 0) | (blk == cur) | (blk == cur - 1)
    score = jnp.where(valid, jnp.where(forced, FORCE_SCORE, imp), NEG_INF)
    sel = jnp.where(_topk_select(score, valid, n_blk), 1.0, 0.0)

    pcol = lax.broadcasted_iota(jnp.int32, (1, page_rows), 1)
    head_ok = (pcol & (N_KV_HEADS - 1)) == hrow
    upper = pcol >= page_rows // 2
    for p in range(n_pages):
        s = _dot_nt(q_rot, sk[p][0].reshape(page_rows, HEAD_DIM).astype(BF16)) * SCALE
        chosen = jnp.where(upper, sel[:, 2 * p + 1:2 * p + 2], sel[:, 2 * p:2 * p + 1])
        s_ref[:, p * page_rows:(p + 1) * page_rows] = jnp.where(head_ok & (chosen > 0.5), s, NEG_INF)
    s_new = jnp.sum(q_rot32 * kse_ref[0], axis=-1, keepdims=True) * SCALE
    p_old, p_new, denom = softmax_parts(s_ref[...], s_new)
    acc = p_new * vse_ref[0]
    for p in range(n_pages):
        acc = acc + _dot(p_old[:, p * page_rows:(p + 1) * page_rows].astype(BF16),
                         sv[p][0].reshape(page_rows, HEAD_DIM).astype(BF16))
    o_sel = acc / denom

    n_win = wink_ref.shape[1]
    wcol = lax.broadcasted_iota(jnp.int32, (1, n_win), 1)
    wmask = ((wcol & (N_KV_HEADS - 1)) == hrow) & (lax.shift_right_logical(wcol, 2) >= n_win // N_KV_HEADS + 1 - WINDOW)
    sw = jnp.where(wmask, _dot_nt(q_rot, wink_ref[0].astype(BF16)) * SCALE, NEG_INF)
    sw_new = jnp.sum(q_rot32 * kwe_ref[0], axis=-1, keepdims=True) * SCALE
    p_old, p_new, denom = softmax_parts(sw, sw_new)
    o_win = (p_new * vwe_ref[0] + _dot(p_old.astype(BF16), winv_ref[0].astype(BF16))) / denom

    gates = gates_ref[0]
    o = gates[:, 0:1] * o_cmp + gates[:, 1:2] * o_sel + gates[:, 2:3] * o_win
    o_ref[0] = o.astype(BF16)


def _attn_sample(page_table, caches, win_k, win_v, q_raw, q_rot, new_rows, gates, cmp_k_w, cmp_v_w, ov, gs):
    batch, n_pages = page_table.shape
    past = n_pages * PAGE_SIZE
    chunk = CMP_STRIDE * N_KV_HEADS
    page_specs = [pl.BlockSpec((1, PAGE_SIZE // CMP_STRIDE, chunk, HEAD_DIM), lambda b, pt, p=p: (pt[b, p], 0, 0, 0))
                  for p in range(n_pages)]
    per_b = lambda a: pl.BlockSpec((1,) + a.shape[1:], lambda b, pt: (b,) + (0,) * (a.ndim - 1))
    whole = lambda a: pl.BlockSpec(a.shape, lambda b, pt: (0,) * a.ndim)
    ins, specs = [], []
    for c in caches:
        ins += [c] * n_pages
        specs += page_specs
    for a in (win_k, win_v, q_raw, q_rot, *new_rows, gates):
        ins.append(a)
        specs.append(per_b(a))
    for a in (*cmp_k_w, *cmp_v_w, ov, gs):
        ins.append(a)
        specs.append(whole(a))
    return pl.pallas_call(
        functools.partial(_attn_sample_body, n_pages=n_pages, past=past),
        grid_spec=pltpu.PrefetchScalarGridSpec(
            num_scalar_prefetch=1, grid=(batch,), in_specs=specs,
            out_specs=pl.BlockSpec((1, N_HEADS, HEAD_DIM), lambda b, pt: (b, 0, 0)),
            scratch_shapes=[pltpu.VMEM((N_HEADS, past * N_KV_HEADS), F32)]),
        out_shape=jax.ShapeDtypeStruct((batch, N_HEADS, HEAD_DIM), BF16),
        compiler_params=_cparams("arbitrary"),
        name="attn_sample",
    )(page_table, *ins)


def _rope_tables(pos):
    half = HEAD_DIM // 2
    inv_freq = jnp.power(ROPE_THETA, -jnp.arange(half, dtype=F32) / half)
    ang = pos.astype(F32)[:, None] * inv_freq[None, :]
    cos, sin = jnp.cos(ang), jnp.sin(ang)
    return jnp.concatenate([cos, cos], axis=1), jnp.concatenate([-sin, sin], axis=1)


def _overlap(n_sub, n_cols):
    n = jnp.arange(n_sub)[:, None]
    j = jnp.arange(LANES)[None, :]
    c_start, s_start = n * CMP_STRIDE, j * SEL_BLOCK
    hit = (c_start <= s_start + SEL_BLOCK - 1) & (c_start + CMP_BLOCK - 1 >= s_start)
    return (hit & (n < n_sub - 1) & (j < n_cols)).astype(BF16)


def kernel(x_prompt, x_sample, state_pool, cache_cmp_k, cache_cmp_v, cache_sel_k, cache_sel_v, state_win_k, state_win_v, page_table, norm_ffn1, norm_mix, norm_ffn2, norm_final, w_ffn_gate, w_ffn_up, w_ffn_down, w_pool, pool_scale, w_nsa_in, w_nsa_out, cmp_pe_k, cmp_w1_k, cmp_w2_k, cmp_pe_v, cmp_w1_v, cmp_w2_v):
    batch, seq, _ = x_prompt.shape
    dec_batch, dec_seq, _ = x_sample.shape
    past = page_table.shape[1] * PAGE_SIZE
    win_buf = state_win_k.shape[2]
    assert dec_seq == 1 and past % SEL_BLOCK == 0 and win_buf == WINDOW and seq % 256 == 0
    assert norm_ffn1.shape[0] == 2 and w_pool.shape[0] == 1 and w_nsa_in.shape[0] == 1

    vec = lambda v: v.reshape(1, D_MODEL)
    wg, wu, wd = w_ffn_gate.astype(BF16), w_ffn_up.astype(BF16), w_ffn_down.astype(BF16)
    wpool = w_pool[0].astype(BF16)
    n_main = Q_DIM + 6 * KV_DIM
    w_main = w_nsa_in[0][:, :n_main].astype(BF16)
    wgt = w_nsa_in[0][:, n_main:].reshape(D_MODEL, N_KV_HEADS, GQA_GROUP, N_BRANCHES)
    wgt = jnp.transpose(wgt, (1, 0, 3, 2)).reshape(N_KV_HEADS, D_MODEL, N_BRANCHES * GQA_GROUP)
    wgt = jnp.pad(wgt, ((0, 0), (0, 0), (0, LANES - N_BRANCHES * GQA_GROUP))).astype(BF16)
    w_out = w_nsa_out[0].astype(BF16)
    cmp_k_w = _compress_weights(cmp_pe_k[0], cmp_w1_k[0], cmp_w2_k[0])
    cmp_v_w = _compress_weights(cmp_pe_v[0], cmp_w1_v[0], cmp_w2_v[0])
    ffn = lambda x, g, layer, which, gf=None: _ffn(x, vec(g), wg, wu, wd, layer, which, g_final=gf)

    xp = x_prompt.reshape(batch * seq, D_MODEL)
    xp = ffn(xp, norm_ffn1[0], 0, 0)
    xp, pool_p = _pool_prompt(xp, vec(norm_mix[0]), wpool, vec(pool_scale[0]), batch, seq)
    xp = ffn(xp, norm_ffn2[0], 0, 1)
    xp = ffn(xp, norm_ffn1[1], 1, 0)
    tm = 512
    cos_p, sin_p = _rope_tables(jnp.arange(seq))
    (q_raw, q_rot, kc_p, vc_p, ks_p, vs_p, kw_p, vw_p, ks_hm, vs_hm, kw_hm, vw_hm, gates_p) = _nsa_project(
        xp, vec(norm_mix[1]), w_main, wgt, cos_p, sin_p, seq // tm, tm)
    kcc = _compress_prompt(kc_p, *cmp_k_w, batch, seq)
    vcc = _compress_prompt(vc_p, *cmp_v_w, batch, seq)
    n_sub = seq // CMP_STRIDE
    expand = (jnp.arange(LANES)[:, None] == jnp.arange(seq)[None, :] // SEL_BLOCK).astype(BF16)
    o_p = _attn_prompt(q_raw, q_rot, kcc, vcc, ks_hm, vs_hm, kw_hm, vw_hm, gates_p,
                       _overlap(n_sub, seq // SEL_BLOCK), expand, batch, seq)
    xp = _matmul_residual(o_p, w_out, xp, tm)
    y_prompt = ffn(xp, norm_ffn2[1], 1, 1, vec(norm_final)).reshape(batch, seq, D_MODEL)

    xs = x_sample.reshape(dec_batch, D_MODEL)
    xs = ffn(xs, norm_ffn1[0], 0, 0)
    xs, h_s = _pool_sample(xs, jnp.transpose(state_pool[0], (1, 0, 2)), vec(norm_mix[0]), wpool, vec(pool_scale[0]))
    pool_s = jnp.concatenate([state_pool[0][:, 1:], h_s[:, None]], axis=1)
    xs = ffn(xs, norm_ffn2[0], 0, 1)
    xs = ffn(xs, norm_ffn1[1], 1, 0)
    cos_s, sin_s = _rope_tables(jnp.full((dec_batch,), past))
    (q_raw_s, q_rot_s, kc_s, vc_s, ks_s, vs_s, kw_s, vw_s, _, _, _, _, gates_s) = _nsa_project(
        xs, vec(norm_mix[1]), w_main, wgt, cos_s, sin_s, 1, dec_batch)
    per_head = lambda a: jnp.repeat(a.reshape(dec_batch, N_KV_HEADS, HEAD_DIM), GQA_GROUP, axis=1)
    g3 = gates_s[:, :, :N_BRANCHES * GQA_GROUP].reshape(N_KV_HEADS, dec_batch, N_BRANCHES, GQA_GROUP)
    g3 = jnp.transpose(g3, (1, 0, 3, 2)).reshape(dec_batch, N_HEADS, N_BRANCHES)
    chunk = CMP_STRIDE * N_KV_HEADS
    paged = lambda c: c[0].reshape(c.shape[1], PAGE_SIZE // CMP_STRIDE, chunk, HEAD_DIM)
    win_rows = lambda s: s[0].reshape(dec_batch, win_buf * N_KV_HEADS, HEAD_DIM)
    gs = (jnp.arange(N_HEADS)[:, None] // GQA_GROUP == jnp.arange(N_HEADS)[None, :] // GQA_GROUP).astype(BF16)
    n_sub_s = past // CMP_STRIDE
    ov_s = jnp.repeat(_overlap(n_sub_s, past // SEL_BLOCK + 1), SUBLANES, axis=0)
    o_s = _attn_sample(page_table, [paged(c) for c in (cache_cmp_k, cache_cmp_v, cache_sel_k, cache_sel_v)],
                       win_rows(state_win_k), win_rows(state_win_v),
                       q_raw_s.reshape(dec_batch, N_HEADS, HEAD_DIM), q_rot_s.reshape(dec_batch, N_HEADS, HEAD_DIM),
                       [per_head(a) for a in (ks_s, vs_s, kw_s, vw_s)], g3, cmp_k_w, cmp_v_w, ov_s, gs)
    xs = _matmul_residual(o_s.reshape(dec_batch, Q_DIM), w_out, xs, dec_batch)
    y_sample = ffn(xs, norm_ffn2[1], 1, 1, vec(norm_final)).reshape(dec_batch, 1, D_MODEL)

    kv5 = lambda a, b, s: a.reshape(1, b, s, N_KV_HEADS, HEAD_DIM)
    keep = min(WINDOW, seq)
    new_win = lambda state, row: jnp.concatenate(
        [state[0][:, 1:], row.reshape(dec_batch, 1, N_KV_HEADS, HEAD_DIM)], axis=1)[None]
    return (y_prompt, y_sample, pool_p[None], pool_s[None],
            kv5(kc_p, batch, seq), kv5(kc_s, dec_batch, 1), kv5(vc_p, batch, seq), kv5(vc_s, dec_batch, 1),
            kv5(ks_p, batch, seq), kv5(ks_s, dec_batch, 1), kv5(vs_p, batch, seq), kv5(vs_s, dec_batch, 1),
            kv5(kw_p, batch, seq)[:, :, seq - keep:], new_win(state_win_k, kw_s),
            kv5(vw_p, batch, seq)[:, :, seq - keep:], new_win(state_win_v, vw_s))
```

```python
import functools

import jax
import jax.numpy as jnp
from jax import lax
from jax.experimental import pallas as pl
from jax.experimental.pallas import tpu as pltpu

F32 = jnp.float32
BF16 = jnp.bfloat16

D_MODEL = 2048
D_FF = 5632
POOL_WINDOWS = (2, 4, 8, 16)
POOL_GROUP_DIM = D_MODEL // len(POOL_WINDOWS)
POOL_STATE_LEN = max(POOL_WINDOWS) - 1
N_HEADS = 16
HEAD_DIM = 128
N_KV_HEADS = 4
GQA_GROUP = N_HEADS // N_KV_HEADS
Q_DIM = N_HEADS * HEAD_DIM
KV_DIM = N_KV_HEADS * HEAD_DIM
N_BRANCHES = 3
CMP_BLOCK = 32
CMP_STRIDE = 16
SEL_BLOCK = 64
SEL_TOPK = 16
WINDOW = 512
PAGE_SIZE = 128
SCALE = HEAD_DIM ** -0.5
LOG2E = 1.4426950408889634
ROPE_THETA = 10000.0
RMS_EPS = 1e-6
NEG_INF = -1e30
FORCE_SCORE = 1e9
TINY = 1e-30

LANES = 128
SUBLANES = 8
VMEM_LIMIT = 56 * 1024 * 1024


def _cparams(*sem):
    return pltpu.CompilerParams(dimension_semantics=sem, vmem_limit_bytes=VMEM_LIMIT)


def _rms(x, g):
    y = x * lax.rsqrt(jnp.mean(x * x, axis=-1, keepdims=True) + RMS_EPS)
    return y * g


def _dot(a, b):
    return jnp.dot(a, b, preferred_element_type=F32)


def _dot_nt(a, b):
    return lax.dot_general(a, b, (((1,), (1,)), ((), ())), preferred_element_type=F32)


def _ffn_body(x_ref, g_ref, wg_ref, wu_ref, wd_ref, *rest, final_norm):
    if final_norm:
        gf_ref, o_ref, xn_ref = rest
    else:
        o_ref, xn_ref = rest
    f = pl.program_id(1)

    @pl.when(f == 0)
    def _():
        x = x_ref[...]
        xn_ref[...] = _rms(x, g_ref[...]).astype(BF16)
        o_ref[...] = x

    xn = xn_ref[...]
    gate = _dot(xn, wg_ref[...].astype(BF16))
    up = _dot(xn, wu_ref[...].astype(BF16))
    h = (gate * jax.nn.sigmoid(gate)) * up
    o_ref[...] += _dot((0.5 * h).astype(BF16), wd_ref[...].astype(BF16))

    if final_norm:
        @pl.when(f == pl.num_programs(1) - 1)
        def _():
            o_ref[...] = _rms(o_ref[...], gf_ref[...])


def _ffn(x, g, wg, wu, wd, layer, which, g_final=None, tf=256):
    m = x.shape[0]
    tm = min(m, 1024)
    final_norm = g_final is not None
    row = lambda i, f: (i, 0)
    in_specs = [
        pl.BlockSpec((tm, D_MODEL), row),
        pl.BlockSpec((1, D_MODEL), lambda i, f: (0, 0)),
        pl.BlockSpec((None, None, D_MODEL, tf), lambda i, f: (layer, which, 0, f)),
        pl.BlockSpec((None, None, D_MODEL, tf), lambda i, f: (layer, which, 0, f)),
        pl.BlockSpec((None, None, tf, D_MODEL), lambda i, f: (layer, which, f, 0)),
    ]
    args = [x, g, wg, wu, wd]
    if final_norm:
        in_specs.append(pl.BlockSpec((1, D_MODEL), lambda i, f: (0, 0)))
        args.append(g_final)
    return pl.pallas_call(
        functools.partial(_ffn_body, final_norm=final_norm),
        grid=(m // tm, D_FF // tf),
        in_specs=in_specs,
        out_specs=pl.BlockSpec((tm, D_MODEL), row),
        out_shape=jax.ShapeDtypeStruct((m, D_MODEL), F32),
        scratch_shapes=[pltpu.VMEM((tm, D_MODEL), BF16)],
        compiler_params=_cparams("parallel", "arbitrary"),
        name="ffn",
    )(*args)


def _pool_groups(h_cols, x, w_ref, sc_ref, o_ref, window_sum, inv_cnt):
    for gi, w in enumerate(POOL_WINDOWS):
        c0 = gi * POOL_GROUP_DIM
        cols = slice(c0, c0 + POOL_GROUP_DIM)
        hc = h_cols(cols)
        d = window_sum(w, cols, hc) * inv_cnt(w) - hc
        y = _dot(d.astype(BF16), w_ref[gi])
        o_ref[:, cols] = x[:, cols] + y * sc_ref[:, cols]


def _pool_prompt_body(x_ref, g_ref, w_ref, sc_ref, o_ref, st_ref, hbuf, *, ts):
    s = pl.program_id(1)
    halo = 2 * SUBLANES

    @pl.when(s == 0)
    def _():
        hbuf[0:halo, :] = jnp.zeros((halo, D_MODEL), F32)

    @pl.when(s > 0)
    def _():
        hbuf[0:halo, :] = hbuf[ts:ts + halo, :]

    x = x_ref[...]
    hbuf[halo:halo + ts, :] = _rms(x, g_ref[...])
    pos = s * ts + lax.broadcasted_iota(jnp.int32, (ts, 1), 0)

    def window_sum(w, cols, hc):
        acc = hc
        for k in range(1, w):
            acc = acc + hbuf[halo - k:halo - k + ts, cols]
        return acc

    _pool_groups(lambda cols: hbuf[halo:halo + ts, cols], x, w_ref, sc_ref, o_ref, window_sum,
                 lambda w: 1.0 / jnp.minimum(w, pos + 1).astype(F32))

    @pl.when(s == pl.num_programs(1) - 1)
    def _():
        st_ref[0] = hbuf[halo + ts - POOL_STATE_LEN:halo + ts, :]


def _pool_prompt(x, g, w_pool, scale, batch, seq, ts=256):
    nt = seq // ts
    return pl.pallas_call(
        functools.partial(_pool_prompt_body, ts=ts),
        grid=(batch, nt),
        in_specs=[
            pl.BlockSpec((ts, D_MODEL), lambda b, s: (b * nt + s, 0)),
            pl.BlockSpec((1, D_MODEL), lambda b, s: (0, 0)),
            pl.BlockSpec((len(POOL_WINDOWS), POOL_GROUP_DIM, POOL_GROUP_DIM), lambda b, s: (0, 0, 0)),
            pl.BlockSpec((1, D_MODEL), lambda b, s: (0, 0)),
        ],
        out_specs=[
            pl.BlockSpec((ts, D_MODEL), lambda b, s: (b * nt + s, 0)),
            pl.BlockSpec((1, POOL_STATE_LEN, D_MODEL), lambda b, s: (b, 0, 0)),
        ],
        out_shape=[
            jax.ShapeDtypeStruct((batch * seq, D_MODEL), F32),
            jax.ShapeDtypeStruct((batch, POOL_STATE_LEN, D_MODEL), F32),
        ],
        scratch_shapes=[pltpu.VMEM((ts + 2 * SUBLANES, D_MODEL), F32)],
        compiler_params=_cparams("arbitrary", "arbitrary"),
        name="pool_prompt",
    )(x, g, w_pool, scale)


def _pool_sample_body(x_ref, st_ref, g_ref, w_ref, sc_ref, o_ref, h_ref):
    x = x_ref[...]
    h_ref[...] = _rms(x, g_ref[...])

    def window_sum(w, cols, hc):
        acc = hc
        for k in range(1, w):
            acc = acc + st_ref[POOL_STATE_LEN - k, :, cols]
        return acc

    _pool_groups(lambda cols: h_ref[:, cols], x, w_ref, sc_ref, o_ref, window_sum, lambda w: 1.0 / w)


def _pool_sample(x, state_t, g, w_pool, scale, tb=32):
    b = x.shape[0]
    return pl.pallas_call(
        _pool_sample_body,
        grid=(b // tb,),
        in_specs=[
            pl.BlockSpec((tb, D_MODEL), lambda i: (i, 0)),
            pl.BlockSpec((POOL_STATE_LEN, tb, D_MODEL), lambda i: (0, i, 0)),
            pl.BlockSpec((1, D_MODEL), lambda i: (0, 0)),
            pl.BlockSpec((len(POOL_WINDOWS), POOL_GROUP_DIM, POOL_GROUP_DIM), lambda i: (0, 0, 0)),
            pl.BlockSpec((1, D_MODEL), lambda i: (0, 0)),
        ],
        out_specs=[pl.BlockSpec((tb, D_MODEL), lambda i: (i, 0)), pl.BlockSpec((tb, D_MODEL), lambda i: (i, 0))],
        out_shape=[jax.ShapeDtypeStruct((b, D_MODEL), F32), jax.ShapeDtypeStruct((b, D_MODEL), F32)],
        compiler_params=_cparams("parallel"),
        name="pool_sample",
    )(x, state_t, g, w_pool, scale)


N_PROJ_TILES = (Q_DIM + 6 * KV_DIM) // KV_DIM


def _proj_body(x_ref, g_ref, w_ref, wg_ref, cos_ref, sin_ref,
               qraw_ref, qrot_ref, kc_ref, vc_ref, ks_ref, vs_ref, kw_ref, vw_ref,
               kshm_ref, vshm_ref, kwhm_ref, vwhm_ref, gates_ref, xn_ref, *, tm):
    j = pl.program_id(1)

    @pl.when(j == 0)
    def _():
        xn = _rms(x_ref[...], g_ref[...]).astype(BF16)
        xn_ref[...] = xn
        for kvh in range(N_KV_HEADS):
            gates_ref[kvh] = jax.nn.sigmoid(_dot(xn, wg_ref[kvh]))

    z = _dot(xn_ref[...], w_ref[...])
    heads = lambda a: [a[:, h * HEAD_DIM:(h + 1) * HEAD_DIM] for h in range(N_KV_HEADS)]

    def rope(slabs):
        return [s * cos_ref[...] + pltpu.roll(s, HEAD_DIM // 2, 1) * sin_ref[...] for s in slabs]

    @pl.when(j < Q_DIM // KV_DIM)
    def _():
        zs = z * (SCALE * LOG2E)
        qraw_ref[...] = zs.astype(BF16)
        for h, r in enumerate(rope(heads(zs))):
            qrot_ref[:, h * HEAD_DIM:(h + 1) * HEAD_DIM] = r.astype(BF16)

    def store(ref, hm_ref, slabs):
        for h, s in enumerate(slabs):
            ref[pl.ds(h, tm, stride=N_KV_HEADS), :] = s
            if hm_ref is not None:
                hm_ref[h] = s.astype(BF16)

    first = Q_DIM // KV_DIM
    for off, (ref, hm_ref, rot) in enumerate([(kc_ref, None, False), (vc_ref, None, False), (ks_ref, kshm_ref, True),
                                              (vs_ref, vshm_ref, False), (kw_ref, kwhm_ref, True),
                                              (vw_ref, vwhm_ref, False)]):
        @pl.when(j == first + off)
        def _(ref=ref, hm_ref=hm_ref, rot=rot):
            slabs = heads(z)
            store(ref, hm_ref, rope(slabs) if rot else slabs)


def _nsa_project(x, g, w_main, w_gates, cos, sin, rope_period_tiles, tm):
    m = x.shape[0]
    nq = Q_DIM // KV_DIM
    row = lambda i, j: (i, 0)
    kv_sds = jax.ShapeDtypeStruct((m * N_KV_HEADS, HEAD_DIM), F32)
    hm_sds = jax.ShapeDtypeStruct((N_KV_HEADS, m, HEAD_DIM), BF16)
    q_sds = jax.ShapeDtypeStruct((m, Q_DIM), BF16)
    kv_spec = pl.BlockSpec((tm * N_KV_HEADS, HEAD_DIM), row)
    hm_spec = pl.BlockSpec((N_KV_HEADS, tm, HEAD_DIM), lambda i, j: (0, i, 0))
    q_spec = pl.BlockSpec((tm, KV_DIM), lambda i, j: (i, jnp.minimum(j, nq - 1)))
    return pl.pallas_call(
        functools.partial(_proj_body, tm=tm),
        grid=(m // tm, N_PROJ_TILES),
        in_specs=[
            pl.BlockSpec((tm, D_MODEL), row),
            pl.BlockSpec((1, D_MODEL), lambda i, j: (0, 0)),
            pl.BlockSpec((D_MODEL, KV_DIM), lambda i, j: (0, j)),
            pl.BlockSpec((N_KV_HEADS, D_MODEL, LANES), lambda i, j: (0, 0, 0)),
            pl.BlockSpec((tm, HEAD_DIM), lambda i, j: (i % rope_period_tiles, 0)),
            pl.BlockSpec((tm, HEAD_DIM), lambda i, j: (i % rope_period_tiles, 0)),
        ],
        out_specs=[q_spec, q_spec] + [kv_spec] * 6 + [hm_spec] * 4
        + [pl.BlockSpec((N_KV_HEADS, tm, LANES), lambda i, j: (0, i, 0))],
        out_shape=[q_sds, q_sds] + [kv_sds] * 6 + [hm_sds] * 4
        + [jax.ShapeDtypeStruct((N_KV_HEADS, m, LANES), F32)],
        scratch_shapes=[pltpu.VMEM((tm, D_MODEL), BF16)],
        compiler_params=_cparams("parallel", "arbitrary"),
        name="nsa_project",
    )(x, g, w_main, w_gates, cos, sin)


N_PAIR = CMP_STRIDE // 2


def _compress(streams, n_sub):
    n_rows = n_sub * N_KV_HEADS
    low = lax.broadcasted_iota(jnp.int32, (1, SUBLANES, 1), 1) < N_KV_HEADS

    def repack(v):
        pairs = lambda a: a.reshape(n_sub // 2, 2, SUBLANES, HEAD_DIM)
        v4, r4 = pairs(v), pairs(pltpu.roll(v, N_KV_HEADS, 0))
        ev = jnp.where(low, v4[:, 0], r4[:, 1]).reshape(n_rows, HEAD_DIM)
        od = jnp.where(low, r4[:, 1], v4[:, 1]).reshape(n_rows, HEAD_DIM)
        return jnp.concatenate([ev, od], axis=1).astype(BF16)

    accs = [jnp.zeros((n_rows, 2 * HEAD_DIM), F32) for _ in streams]
    biases = [jnp.zeros((SUBLANES, 2 * HEAD_DIM), F32) for _ in streams]
    for k in range(N_PAIR):
        for i, (get_rows, wpair_ref, pepair_ref, _) in enumerate(streams):
            accs[i] = accs[i] + _dot(repack(get_rows(k)), wpair_ref[k])
            biases[i] = biases[i] + _dot(pepair_ref[k].astype(BF16), wpair_ref[k])
    outs = []
    for acc, bias, (_, _, _, w2_ref) in zip(accs, biases, streams):
        first = acc[:, :HEAD_DIM] + bias[0:1, :HEAD_DIM]
        second = acc[:, HEAD_DIM:] + bias[1:2, HEAD_DIM:]
        pre = first + pltpu.roll(second, n_rows - N_KV_HEADS, 0)
        outs.append(_dot(jax.nn.gelu(pre, approximate=True).astype(BF16), w2_ref[...]))
    return outs


def _compress_body(xk_ref, xv_ref, wpk_ref, pek_ref, w2k_ref, wpv_ref, pev_ref, w2v_ref, ok_ref, ov_ref, res_ref, *,
                   n_sub):
    rows = lambda x_ref: lambda k: x_ref[:, k * SUBLANES:(k + 1) * SUBLANES, :].reshape(n_sub * SUBLANES, HEAD_DIM)
    outs = _compress([(rows(xk_ref), wpk_ref, pek_ref, w2k_ref), (rows(xv_ref), wpv_ref, pev_ref, w2v_ref)], n_sub)
    for res, o_ref in zip(outs, (ok_ref, ov_ref)):
        res_ref[...] = res
        for h in range(N_KV_HEADS):
            o_ref[0, h] = res_ref[pl.ds(h, n_sub, stride=N_KV_HEADS), :].astype(BF16)


def _compress_prompt(rows_k, rows_v, cmp_k_w, cmp_v_w, batch, seq):
    n_sub = seq // CMP_STRIDE
    chunk = CMP_STRIDE * N_KV_HEADS
    x3 = lambda rows: rows.reshape(batch * n_sub, chunk, HEAD_DIM)
    xspec = pl.BlockSpec((n_sub, chunk, HEAD_DIM), lambda b: (b, 0, 0))
    wspecs = [pl.BlockSpec((N_PAIR, 2 * HEAD_DIM, 2 * HEAD_DIM), lambda b: (0, 0, 0)),
              pl.BlockSpec((N_PAIR, SUBLANES, 2 * HEAD_DIM), lambda b: (0, 0, 0)),
              pl.BlockSpec((HEAD_DIM, HEAD_DIM), lambda b: (0, 0))]
    ospec = pl.BlockSpec((1, N_KV_HEADS, n_sub, HEAD_DIM), lambda b: (b, 0, 0, 0))
    osds = jax.ShapeDtypeStruct((batch, N_KV_HEADS, n_sub, HEAD_DIM), BF16)
    return pl.pallas_call(
        functools.partial(_compress_body, n_sub=n_sub),
        grid=(batch,),
        in_specs=[xspec, xspec] + wspecs + wspecs,
        out_specs=[ospec, ospec],
        out_shape=[osds, osds],
        scratch_shapes=[pltpu.VMEM((n_sub * N_KV_HEADS, HEAD_DIM), F32)],
        compiler_params=_cparams("parallel"),
        name="compress_prompt",
    )(x3(rows_k), x3(rows_v), *cmp_k_w, *cmp_v_w)


def _compress_weights(pe, w1, w2):
    w1r = w1.reshape(2, N_PAIR, 2, HEAD_DIM, HEAD_DIM)
    wpair = jnp.transpose(w1r, (1, 2, 3, 0, 4)).reshape(N_PAIR, 2 * HEAD_DIM, 2 * HEAD_DIM).astype(BF16)
    per = pe.reshape(2, N_PAIR, 2 * HEAD_DIM)
    pepair = jnp.zeros((N_PAIR, SUBLANES, 2 * HEAD_DIM), F32).at[:, 0:2, :].set(jnp.transpose(per, (1, 0, 2)))
    return wpair, pepair, w2.astype(BF16)


def _topk_select(score, valid, n_blk):
    lane = lax.broadcasted_iota(jnp.int32, score.shape, 1)
    rank = jnp.zeros(score.shape, F32)
    for i in range(n_blk):
        ci = score[:, i:i + 1]
        beats = (ci > score) | ((ci == score) & (lane > i))
        rank = rank + jnp.where(beats, 1.0, 0.0)
    return (rank < SEL_TOPK) & valid


def _split_dot(p, w):
    hi = p.astype(BF16)
    lo = (p - hi.astype(F32)).astype(BF16)
    return _dot(hi, w) + _dot(lo, w)


def _attn_prompt_tile(q, qraw_ref, qrot_ref, kcc_ref, vcc_ref, ks_ref, vs_ref, kw_ref, vw_ref, gates_ref, ovt_ref,
                      ex_ref, o_ref, *, tq, seq):
    q0 = q * tq
    kt = tq
    qpos = q0 + lax.broadcasted_iota(jnp.int32, (tq, 1), 0)
    qpos4 = jnp.concatenate([qpos] * GQA_GROUP, axis=0)
    stack = lambda ref: jnp.concatenate([ref[:, g * HEAD_DIM:(g + 1) * HEAD_DIM] for g in range(GQA_GROUP)], axis=0)
    q_raw = stack(qraw_ref)
    q_rot = stack(qrot_ref)

    n_cmp = seq // CMP_STRIDE
    sc = _dot_nt(q_raw, kcc_ref[0, 0])
    c_end = lax.broadcasted_iota(jnp.int32, (1, n_cmp), 1) * CMP_STRIDE + (CMP_BLOCK - 1)
    cmask = c_end <= qpos4
    scm = jnp.where(cmask, sc, NEG_INF)
    e = jnp.where(cmask, jnp.exp2(scm - jnp.max(scm, axis=-1, keepdims=True)), 0.0)
    pc = e / jnp.maximum(jnp.sum(e, axis=-1, keepdims=True), TINY)
    o_cmp = _dot(pc.astype(BF16), vcc_ref[0, 0])

    ranked = q0 + tq > SEL_TOPK * SEL_BLOCK
    if ranked:
        n_blk = seq // SEL_BLOCK
        psum = pc[0:tq] + pc[tq:2 * tq] + pc[2 * tq:3 * tq] + pc[3 * tq:4 * tq]
        hi = psum.astype(BF16)
        lo = (psum - hi.astype(F32)).astype(BF16)
        imp = _dot_nt(ovt_ref[...], hi) + _dot_nt(ovt_ref[...], lo)
        blk = lax.broadcasted_iota(jnp.int32, (n_blk, 1), 0)
        qlane = q0 + lax.broadcasted_iota(jnp.int32, (1, tq), 1)
        cur = lax.shift_right_logical(qlane, SEL_BLOCK.bit_length() - 1)
        valid = blk * SEL_BLOCK <= qlane
        forced = (blk == 0) | (blk == cur) | (blk == cur - 1)
        score = jnp.where(valid, jnp.where(forced, FORCE_SCORE, imp), NEG_INF)
        rank = jnp.zeros((n_blk, tq), F32)
        for i in range((q0 + tq) // SEL_BLOCK):
            ci = score[i:i + 1, :]
            beats = (ci > score) | ((ci == score) & (blk > i))
            rank = rank + jnp.where(beats, 1.0, 0.0)
        sel_t = jnp.where((rank < SEL_TOPK) & valid, 1.0, 0.0)
        sel = jnp.concatenate([sel_t, jnp.zeros((LANES - n_blk, tq), F32)], axis=0).T.astype(BF16)

    def attend(k_ref, v_ref, chunks):
        ss = []
        for c, bias in chunks:
            s = _dot_nt(q_rot, k_ref[0, c * kt:(c + 1) * kt, :])
            ss.append(s if bias is None else s + jnp.concatenate([bias] * GQA_GROUP, axis=0))
        top = ss[0]
        for s in ss[1:]:
            top = jnp.maximum(top, s)
        m = jnp.max(top, axis=-1, keepdims=True)
        tot = None
        acc = None
        for (c, _), s in zip(chunks, ss):
            p = jnp.exp2(s - m)
            tot = p if tot is None else tot + p
            pv = _dot(p.astype(BF16), v_ref[0, c * kt:(c + 1) * kt, :])
            acc = pv if acc is None else acc + pv
        return acc / jnp.sum(tot, axis=-1, keepdims=True)

    def mask_bias(c, ok):
        kpos = c * kt + lax.broadcasted_iota(jnp.int32, (1, kt), 1)
        return jnp.where(ok(kpos), 0.0, NEG_INF)

    sel_chunks = []
    for c in range(q + 1):
        if ranked:
            chosen = _dot(sel, ex_ref[:, c * kt:(c + 1) * kt]) > 0.5
            sel_chunks.append((c, mask_bias(c, lambda kpos: chosen & (kpos <= qpos))))
        else:
            sel_chunks.append((c, mask_bias(c, lambda kpos: kpos <= qpos) if c == q else None))
    o_sel = attend(ks_ref, vs_ref, sel_chunks)

    win_chunks = []
    for c in range(q + 1):
        if c * kt + kt - 1 <= q0 - WINDOW:
            continue
        whole = c * kt + kt - 1 <= q0 and c * kt > q0 + tq - 1 - WINDOW
        win_chunks.append((c, None if whole else mask_bias(c, lambda kpos: (kpos <= qpos) & (kpos > qpos - WINDOW))))
    o_win = attend(kw_ref, vw_ref, win_chunks)

    gates = gates_ref[0]
    for g in range(GQA_GROUP):
        r = slice(g * tq, (g + 1) * tq)
        gate = lambda br: gates[:, br * GQA_GROUP + g:br * GQA_GROUP + g + 1]
        o = gate(0) * o_cmp[r] + gate(1) * o_sel[r] + gate(2) * o_win[r]
        o_ref[:, g * HEAD_DIM:(g + 1) * HEAD_DIM] = o.astype(BF16)


def _attn_prompt_body(*refs, tq, seq):
    qi = pl.program_id(2)
    for q in range(seq // tq):
        @pl.when(qi == q)
        def _(q=q):
            _attn_prompt_tile(q, *refs, tq=tq, seq=seq)


def _attn_prompt(q_raw, q_rot, kcc, vcc, ks_hm, vs_hm, kw_hm, vw_hm, gates, ovt, ex, batch, seq, tq=256):
    nq = seq // tq
    n_cmp = seq // CMP_STRIDE
    qspec = pl.BlockSpec((tq, KV_DIM), lambda b, h, i: (b * nq + i, h))
    cspec = pl.BlockSpec((1, 1, n_cmp, HEAD_DIM), lambda b, h, i: (b, h, 0, 0))
    kvspec = pl.BlockSpec((1, seq, HEAD_DIM), lambda b, h, i: (h, b, 0))
    return pl.pallas_call(
        functools.partial(_attn_prompt_body, tq=tq, seq=seq),
        grid=(batch, N_KV_HEADS, nq),
        in_specs=[qspec, qspec, cspec, cspec, kvspec, kvspec, kvspec, kvspec,
                  pl.BlockSpec((1, tq, LANES), lambda b, h, i: (h, b * nq + i, 0)),
                  pl.BlockSpec(ovt.shape, lambda b, h, i: (0, 0)),
                  pl.BlockSpec((LANES, seq), lambda b, h, i: (0, 0))],
        out_specs=qspec,
        out_shape=jax.ShapeDtypeStruct((batch * seq, Q_DIM), BF16),
        compiler_params=_cparams("parallel", "parallel", "arbitrary"),
        name="attn_prompt",
    )(q_raw, q_rot, kcc, vcc, ks_hm, vs_hm, kw_hm, vw_hm, gates, ovt, ex)


def _matmul_res_body(a_ref, w_ref, r_ref, o_ref):
    o_ref[...] = r_ref[...] + _dot(a_ref[...], w_ref[...])


def _matmul_residual(a, w, res, tm):
    m, k = a.shape
    n = w.shape[1]
    return pl.pallas_call(
        _matmul_res_body,
        grid=(m // tm,),
        in_specs=[pl.BlockSpec((tm, k), lambda i: (i, 0)), pl.BlockSpec((k, n), lambda i: (0, 0)),
                  pl.BlockSpec((tm, n), lambda i: (i, 0))],
        out_specs=pl.BlockSpec((tm, n), lambda i: (i, 0)),
        out_shape=jax.ShapeDtypeStruct((m, n), F32),
        compiler_params=_cparams("parallel"),
        name="out_proj",
    )(a, w, res)


def _attn_sample_body(pt_ref, *refs, n_pages, past):
    del pt_ref
    pages = refs[:4 * n_pages]
    (wink_ref, winv_ref, qraw_ref, qrot_ref, kse_ref, vse_ref, kwe_ref, vwe_ref, gates_ref,
     wpk_ref, pek_ref, w2k_ref, wpv_ref, pev_ref, w2v_ref, ov_ref, gs_ref, o_ref, s_ref) = refs[4 * n_pages:]
    ck, cv, sk, sv = (pages[i * n_pages:(i + 1) * n_pages] for i in range(4))
    n_sub = past // CMP_STRIDE
    n_rows = n_sub * N_KV_HEADS
    sub_pp = PAGE_SIZE // CMP_STRIDE
    page_rows = PAGE_SIZE * N_KV_HEADS

    def rows_of(pg):
        return lambda k: jnp.concatenate(
            [r[0, :, k * SUBLANES:(k + 1) * SUBLANES, :].reshape(sub_pp * SUBLANES, HEAD_DIM) for r in pg], axis=0)

    kcc, vcc = _compress([(rows_of(ck), wpk_ref, pek_ref, w2k_ref), (rows_of(cv), wpv_ref, pev_ref, w2v_ref)], n_sub)
    kcc, vcc = kcc.astype(BF16), vcc.astype(BF16)

    q_raw = qraw_ref[0]
    q_rot = qrot_ref[0]
    q_rot32 = q_rot.astype(F32)
    hrow = lax.shift_right_logical(lax.broadcasted_iota(jnp.int32, (N_HEADS, 1), 0), 2)

    def softmax_parts(s_old, s_new):
        m = jnp.maximum(jnp.max(s_old, axis=-1, keepdims=True), s_new)
        p_old = jnp.exp2(s_old - m)
        p_new = jnp.exp2(s_new - m)
        return p_old, p_new, jnp.sum(p_old, axis=-1, keepdims=True) + p_new

    col = lax.broadcasted_iota(jnp.int32, (1, n_rows), 1)
    cmask = ((col & (N_KV_HEADS - 1)) == hrow) & (
        lax.shift_right_logical(col, 2) * CMP_STRIDE + (CMP_BLOCK - 1) <= past)
    scm = jnp.where(cmask, _dot_nt(q_raw, kcc), NEG_INF)
    e = jnp.where(cmask, jnp.exp2(scm - jnp.max(scm, axis=-1, keepdims=True)), 0.0)
    pc = e / jnp.maximum(jnp.sum(e, axis=-1, keepdims=True), TINY)
    o_cmp = _dot(pc.astype(BF16), vcc)

    imp16 = _split_dot(pc, ov_ref[...])
    hi = imp16.astype(BF16)
    lo = (imp16 - hi.astype(F32)).astype(BF16)
    imp = _dot(gs_ref[...], hi) + _dot(gs_ref[...], lo)
    n_blk = past // SEL_BLOCK + 1
    cur = past // SEL_BLOCK
    blk = lax.broadcasted_iota(jnp.int32, (N_HEADS, LANES), 1)
    valid = blk < n_blk
    forced = (blk == 0) | (blk == cur) | (blk == cur - 1)
    score = jnp.where(valid, jnp.where(forced, FORCE_SCORE, imp), NEG_INF)
    sel = jnp.where(_topk_select(score, valid, n_blk), 1.0, 0.0)

    pcol = lax.broadcasted_iota(jnp.int32, (1, page_rows), 1)
    head_ok = (pcol & (N_KV_HEADS - 1)) == hrow
    upper = pcol >= page_rows // 2
    for p in range(n_pages):
        s = _dot_nt(q_rot, sk[p][0].reshape(page_rows, HEAD_DIM).astype(BF16))
        chosen = jnp.where(upper, sel[:, 2 * p + 1:2 * p + 2], sel[:, 2 * p:2 * p + 1])
        s_ref[:, p * page_rows:(p + 1) * page_rows] = jnp.where(head_ok & (chosen > 0.5), s, NEG_INF)
    s_new = jnp.sum(q_rot32 * kse_ref[0], axis=-1, keepdims=True)
    p_old, p_new, denom = softmax_parts(s_ref[...], s_new)
    acc = p_new * vse_ref[0]
    for p in range(n_pages):
        acc = acc + _dot(p_old[:, p * page_rows:(p + 1) * page_rows].astype(BF16),
                         sv[p][0].reshape(page_rows, HEAD_DIM).astype(BF16))
    o_sel = acc / denom

    n_win = wink_ref.shape[1]
    wcol = lax.broadcasted_iota(jnp.int32, (1, n_win), 1)
    wmask = ((wcol & (N_KV_HEADS - 1)) == hrow) & (lax.shift_right_logical(wcol, 2) >= n_win // N_KV_HEADS + 1 - WINDOW)
    sw = jnp.where(wmask, _dot_nt(q_rot, wink_ref[0].astype(BF16)), NEG_INF)
    sw_new = jnp.sum(q_rot32 * kwe_ref[0], axis=-1, keepdims=True)
    p_old, p_new, denom = softmax_parts(sw, sw_new)
    o_win = (p_new * vwe_ref[0] + _dot(p_old.astype(BF16), winv_ref[0].astype(BF16))) / denom

    gates = gates_ref[0]
    o = gates[:, 0:1] * o_cmp + gates[:, 1:2] * o_sel + gates[:, 2:3] * o_win
    o_ref[0] = o.astype(BF16)


def _attn_sample(page_table, caches, win_k, win_v, q_raw, q_rot, new_rows, gates, cmp_k_w, cmp_v_w, ov, gs):
    batch, n_pages = page_table.shape
    past = n_pages * PAGE_SIZE
    chunk = CMP_STRIDE * N_KV_HEADS
    page_specs = [pl.BlockSpec((1, PAGE_SIZE // CMP_STRIDE, chunk, HEAD_DIM), lambda b, pt, p=p: (pt[b, p], 0, 0, 0))
                  for p in range(n_pages)]
    per_b = lambda a: pl.BlockSpec((1,) + a.shape[1:], lambda b, pt: (b,) + (0,) * (a.ndim - 1))
    whole = lambda a: pl.BlockSpec(a.shape, lambda b, pt: (0,) * a.ndim)
    ins, specs = [], []
    for c in caches:
        ins += [c] * n_pages
        specs += page_specs
    for a in (win_k, win_v, q_raw, q_rot, *new_rows, gates):
        ins.append(a)
        specs.append(per_b(a))
    for a in (*cmp_k_w, *cmp_v_w, ov, gs):
        ins.append(a)
        specs.append(whole(a))
    return pl.pallas_call(
        functools.partial(_attn_sample_body, n_pages=n_pages, past=past),
        grid_spec=pltpu.PrefetchScalarGridSpec(
            num_scalar_prefetch=1, grid=(batch,), in_specs=specs,
            out_specs=pl.BlockSpec((1, N_HEADS, HEAD_DIM), lambda b, pt: (b, 0, 0)),
            scratch_shapes=[pltpu.VMEM((N_HEADS, past * N_KV_HEADS), F32)]),
        out_shape=jax.ShapeDtypeStruct((batch, N_HEADS, HEAD_DIM), BF16),
        compiler_params=_cparams("arbitrary"),
        name="attn_sample",
    )(page_table, *ins)


def _win_update_body(sk_ref, sv_ref, nk_ref, nv_ref, ok_ref, ov_ref, sem, *, n_split):
    batch, width = sk_ref.shape[0], sk_ref.shape[1]
    step = batch // n_split
    copies = []
    for t, (s_ref, n_ref, o_ref) in enumerate(((sk_ref, nk_ref, ok_ref), (sv_ref, nv_ref, ov_ref))):
        for i in range(n_split):
            rows = pl.ds(i * step, step)
            copies.append(pltpu.make_async_copy(s_ref.at[rows, pl.ds(1, width - 1)],
                                                o_ref.at[rows, pl.ds(0, width - 1)], sem.at[t, i, 0]))
            copies.append(pltpu.make_async_copy(n_ref.at[rows], o_ref.at[rows, pl.ds(width - 1, 1)], sem.at[t, i, 1]))
    for c in copies:
        c.start()
    for c in copies:
        c.wait()


def _win_update(state_k, state_v, new_k, new_v, n_split=8):
    any_spec = pl.BlockSpec(memory_space=pl.ANY)
    sds = jax.ShapeDtypeStruct(state_k.shape, state_k.dtype)
    return pl.pallas_call(
        functools.partial(_win_update_body, n_split=n_split),
        in_specs=[any_spec] * 4,
        out_specs=[any_spec] * 2,
        out_shape=[sds, sds],
        scratch_shapes=[pltpu.SemaphoreType.DMA((2, n_split, 2))],
        name="win_update",
    )(state_k, state_v, new_k, new_v)


def _rope_tables(pos):
    half = HEAD_DIM // 2
    inv_freq = jnp.power(ROPE_THETA, -jnp.arange(half, dtype=F32) / half)
    ang = pos.astype(F32)[:, None] * inv_freq[None, :]
    cos, sin = jnp.cos(ang), jnp.sin(ang)
    return jnp.concatenate([cos, cos], axis=1), jnp.concatenate([-sin, sin], axis=1)


def _overlap(n_sub, n_cols):
    n = jnp.arange(n_sub)[:, None]
    j = jnp.arange(LANES)[None, :]
    c_start, s_start = n * CMP_STRIDE, j * SEL_BLOCK
    hit = (c_start <= s_start + SEL_BLOCK - 1) & (c_start + CMP_BLOCK - 1 >= s_start)
    return (hit & (n < n_sub - 1) & (j < n_cols)).astype(BF16)


def kernel(x_prompt, x_sample, state_pool, cache_cmp_k, cache_cmp_v, cache_sel_k, cache_sel_v, state_win_k, state_win_v, page_table, norm_ffn1, norm_mix, norm_ffn2, norm_final, w_ffn_gate, w_ffn_up, w_ffn_down, w_pool, pool_scale, w_nsa_in, w_nsa_out, cmp_pe_k, cmp_w1_k, cmp_w2_k, cmp_pe_v, cmp_w1_v, cmp_w2_v):
    batch, seq, _ = x_prompt.shape
    dec_batch, dec_seq, _ = x_sample.shape
    past = page_table.shape[1] * PAGE_SIZE
    win_buf = state_win_k.shape[2]
    assert dec_seq == 1 and past % SEL_BLOCK == 0 and win_buf == WINDOW and seq % 256 == 0
    assert norm_ffn1.shape[0] == 2 and w_pool.shape[0] == 1 and w_nsa_in.shape[0] == 1

    vec = lambda v: v.reshape(1, D_MODEL)
    wpool = w_pool[0].astype(BF16)
    n_main = Q_DIM + 6 * KV_DIM
    w_main = w_nsa_in[0][:, :n_main].astype(BF16)
    wgt = w_nsa_in[0][:, n_main:].reshape(D_MODEL, N_KV_HEADS, GQA_GROUP, N_BRANCHES)
    wgt = jnp.transpose(wgt, (1, 0, 3, 2)).reshape(N_KV_HEADS, D_MODEL, N_BRANCHES * GQA_GROUP)
    wgt = jnp.pad(wgt, ((0, 0), (0, 0), (0, LANES - N_BRANCHES * GQA_GROUP))).astype(BF16)
    w_out = w_nsa_out[0].astype(BF16)
    cmp_k_w = _compress_weights(cmp_pe_k[0], cmp_w1_k[0], cmp_w2_k[0])
    cmp_v_w = _compress_weights(cmp_pe_v[0], cmp_w1_v[0], cmp_w2_v[0])
    ffn = lambda x, g, layer, which, gf=None: _ffn(x, vec(g), w_ffn_gate, w_ffn_up, w_ffn_down, layer, which, g_final=gf)

    xp = x_prompt.reshape(batch * seq, D_MODEL)
    xp = ffn(xp, norm_ffn1[0], 0, 0)
    xp, pool_p = _pool_prompt(xp, vec(norm_mix[0]), wpool, vec(pool_scale[0]), batch, seq)
    xp = ffn(xp, norm_ffn2[0], 0, 1)
    xp = ffn(xp, norm_ffn1[1], 1, 0)
    tm = 512
    cos_p, sin_p = _rope_tables(jnp.arange(seq))
    (q_raw, q_rot, kc_p, vc_p, ks_p, vs_p, kw_p, vw_p, ks_hm, vs_hm, kw_hm, vw_hm, gates_p) = _nsa_project(
        xp, vec(norm_mix[1]), w_main, wgt, cos_p, sin_p, seq // tm, tm)
    kcc, vcc = _compress_prompt(kc_p, vc_p, cmp_k_w, cmp_v_w, batch, seq)
    n_sub = seq // CMP_STRIDE
    expand = (jnp.arange(LANES)[:, None] == jnp.arange(seq)[None, :] // SEL_BLOCK).astype(BF16)
    ovt = jnp.transpose(_overlap(n_sub, seq // SEL_BLOCK))[:seq // SEL_BLOCK]
    o_p = _attn_prompt(q_raw, q_rot, kcc, vcc, ks_hm, vs_hm, kw_hm, vw_hm, gates_p, ovt, expand, batch, seq)
    xp = _matmul_residual(o_p, w_out, xp, tm)
    y_prompt = ffn(xp, norm_ffn2[1], 1, 1, vec(norm_final)).reshape(batch, seq, D_MODEL)

    xs = x_sample.reshape(dec_batch, D_MODEL)
    xs = ffn(xs, norm_ffn1[0], 0, 0)
    xs, h_s = _pool_sample(xs, jnp.transpose(state_pool[0], (1, 0, 2)), vec(norm_mix[0]), wpool, vec(pool_scale[0]))
    pool_s = jnp.concatenate([state_pool[0][:, 1:], h_s[:, None]], axis=1)
    xs = ffn(xs, norm_ffn2[0], 0, 1)
    xs = ffn(xs, norm_ffn1[1], 1, 0)
    cos_s, sin_s = _rope_tables(jnp.full((dec_batch,), past))
    (q_raw_s, q_rot_s, kc_s, vc_s, ks_s, vs_s, kw_s, vw_s, _, _, _, _, gates_s) = _nsa_project(
        xs, vec(norm_mix[1]), w_main, wgt, cos_s, sin_s, 1, dec_batch)
    per_head = lambda a: jnp.repeat(a.reshape(dec_batch, N_KV_HEADS, HEAD_DIM), GQA_GROUP, axis=1)
    g3 = gates_s[:, :, :N_BRANCHES * GQA_GROUP].reshape(N_KV_HEADS, dec_batch, N_BRANCHES, GQA_GROUP)
    g3 = jnp.transpose(g3, (1, 0, 3, 2)).reshape(dec_batch, N_HEADS, N_BRANCHES)
    chunk = CMP_STRIDE * N_KV_HEADS
    paged = lambda c: c[0].reshape(c.shape[1], PAGE_SIZE // CMP_STRIDE, chunk, HEAD_DIM)
    win_rows = lambda s: s[0].reshape(dec_batch, win_buf * N_KV_HEADS, HEAD_DIM)
    gs = (jnp.arange(N_HEADS)[:, None] // GQA_GROUP == jnp.arange(N_HEADS)[None, :] // GQA_GROUP).astype(BF16)
    n_sub_s = past // CMP_STRIDE
    ov_s = jnp.repeat(_overlap(n_sub_s, past // SEL_BLOCK + 1), N_KV_HEADS, axis=0)
    o_s = _attn_sample(page_table, [paged(c) for c in (cache_cmp_k, cache_cmp_v, cache_sel_k, cache_sel_v)],
                       win_rows(state_win_k), win_rows(state_win_v),
                       q_raw_s.reshape(dec_batch, N_HEADS, HEAD_DIM), q_rot_s.reshape(dec_batch, N_HEADS, HEAD_DIM),
                       [per_head(a) for a in (ks_s, vs_s, kw_s, vw_s)], g3, cmp_k_w, cmp_v_w, ov_s, gs)
    xs = _matmul_residual(o_s.reshape(dec_batch, Q_DIM), w_out, xs, dec_batch)
    y_sample = ffn(xs, norm_ffn2[1], 1, 1, vec(norm_final)).reshape(dec_batch, 1, D_MODEL)

    kv5 = lambda a, b, s: a.reshape(1, b, s, N_KV_HEADS, HEAD_DIM)
    keep = min(WINDOW, seq)
    new_row = lambda a: a.reshape(dec_batch, 1, N_KV_HEADS, HEAD_DIM)
    win_k_s, win_v_s = _win_update(state_win_k[0], state_win_v[0], new_row(kw_s), new_row(vw_s))
    return (y_prompt, y_sample, pool_p[None], pool_s[None],
            kv5(kc_p, batch, seq), kv5(kc_s, dec_batch, 1), kv5(vc_p, batch, seq), kv5(vc_s, dec_batch, 1),
            kv5(ks_p, batch, seq), kv5(ks_s, dec_batch, 1), kv5(vs_p, batch, seq), kv5(vs_s, dec_batch, 1),
            kv5(kw_p, batch, seq)[:, :, seq - keep:], win_k_s[None],
            kv5(vw_p, batch, seq)[:, :, seq - keep:], win_v_s[None])
```

```python
import functools

import jax
import jax.numpy as jnp
from jax import lax
from jax.experimental import pallas as pl
from jax.experimental.pallas import tpu as pltpu

F32 = jnp.float32
BF16 = jnp.bfloat16

D_MODEL = 2048
D_FF = 5632
POOL_WINDOWS = (2, 4, 8, 16)
POOL_GROUP_DIM = D_MODEL // len(POOL_WINDOWS)
POOL_STATE_LEN = max(POOL_WINDOWS) - 1
N_HEADS = 16
HEAD_DIM = 128
N_KV_HEADS = 4
GQA_GROUP = N_HEADS // N_KV_HEADS
Q_DIM = N_HEADS * HEAD_DIM
KV_DIM = N_KV_HEADS * HEAD_DIM
N_BRANCHES = 3
CMP_BLOCK = 32
CMP_STRIDE = 16
SEL_BLOCK = 64
SEL_TOPK = 16
WINDOW = 512
PAGE_SIZE = 128
SCALE = HEAD_DIM ** -0.5
LOG2E = 1.4426950408889634
ROPE_THETA = 10000.0
RMS_EPS = 1e-6
NEG_INF = -1e30
FORCE_SCORE = 1e9
TINY = 1e-30

LANES = 128
SUBLANES = 8
VMEM_LIMIT = 56 * 1024 * 1024


def _cparams(*sem):
    return pltpu.CompilerParams(dimension_semantics=sem, vmem_limit_bytes=VMEM_LIMIT)


def _rms(x, g):
    y = x * lax.rsqrt(jnp.mean(x * x, axis=-1, keepdims=True) + RMS_EPS)
    return y * g


def _dot(a, b):
    return jnp.dot(a, b, preferred_element_type=F32)


def _dot_nt(a, b):
    return lax.dot_general(a, b, (((1,), (1,)), ((), ())), preferred_element_type=F32)


def _ffn_body(x_ref, g_ref, wg_ref, wu_ref, wd_ref, *rest, final_norm):
    if final_norm:
        gf_ref, o_ref, xn_ref = rest
    else:
        o_ref, xn_ref = rest
    f = pl.program_id(1)

    @pl.when(f == 0)
    def _():
        x = x_ref[...]
        xn_ref[...] = _rms(x, g_ref[...]).astype(BF16)
        o_ref[...] = x

    xn = xn_ref[...]
    gate = _dot(xn, wg_ref[...].astype(BF16))
    up = _dot(xn, wu_ref[...].astype(BF16))
    h = (gate * jax.nn.sigmoid(gate)) * up
    o_ref[...] += _dot((0.5 * h).astype(BF16), wd_ref[...].astype(BF16))

    if final_norm:
        @pl.when(f == pl.num_programs(1) - 1)
        def _():
            o_ref[...] = _rms(o_ref[...], gf_ref[...])


def _ffn(x, g, wg, wu, wd, layer, which, g_final=None, tf=256):
    m = x.shape[0]
    tm = min(m, 1024)
    final_norm = g_final is not None
    row = lambda i, f: (i, 0)
    in_specs = [
        pl.BlockSpec((tm, D_MODEL), row),
        pl.BlockSpec((1, D_MODEL), lambda i, f: (0, 0)),
        pl.BlockSpec((None, None, D_MODEL, tf), lambda i, f: (layer, which, 0, f)),
        pl.BlockSpec((None, None, D_MODEL, tf), lambda i, f: (layer, which, 0, f)),
        pl.BlockSpec((None, None, tf, D_MODEL), lambda i, f: (layer, which, f, 0)),
    ]
    args = [x, g, wg, wu, wd]
    if final_norm:
        in_specs.append(pl.BlockSpec((1, D_MODEL), lambda i, f: (0, 0)))
        args.append(g_final)
    return pl.pallas_call(
        functools.partial(_ffn_body, final_norm=final_norm),
        grid=(m // tm, D_FF // tf),
        in_specs=in_specs,
        out_specs=pl.BlockSpec((tm, D_MODEL), row),
        out_shape=jax.ShapeDtypeStruct((m, D_MODEL), F32),
        scratch_shapes=[pltpu.VMEM((tm, D_MODEL), BF16)],
        compiler_params=_cparams("parallel", "arbitrary"),
        name="ffn",
    )(*args)


def _pool_groups(h_cols, x, w_ref, sc_ref, o_ref, window_sum, inv_cnt):
    for gi, w in enumerate(POOL_WINDOWS):
        c0 = gi * POOL_GROUP_DIM
        cols = slice(c0, c0 + POOL_GROUP_DIM)
        hc = h_cols(cols)
        d = window_sum(w, cols, hc) * inv_cnt(w) - hc
        y = _dot(d.astype(BF16), w_ref[gi])
        o_ref[:, cols] = x[:, cols] + y * sc_ref[:, cols]


def _pool_prompt_body(x_ref, g_ref, w_ref, sc_ref, o_ref, st_ref, hbuf, *, ts):
    s = pl.program_id(1)
    halo = 2 * SUBLANES

    @pl.when(s == 0)
    def _():
        hbuf[0:halo, :] = jnp.zeros((halo, D_MODEL), F32)

    @pl.when(s > 0)
    def _():
        hbuf[0:halo, :] = hbuf[ts:ts + halo, :]

    x = x_ref[...]
    hbuf[halo:halo + ts, :] = _rms(x, g_ref[...])
    pos = s * ts + lax.broadcasted_iota(jnp.int32, (ts, 1), 0)

    def window_sum(w, cols, hc):
        acc = hc
        for k in range(1, w):
            acc = acc + hbuf[halo - k:halo - k + ts, cols]
        return acc

    _pool_groups(lambda cols: hbuf[halo:halo + ts, cols], x, w_ref, sc_ref, o_ref, window_sum,
                 lambda w: 1.0 / jnp.minimum(w, pos + 1).astype(F32))

    @pl.when(s == pl.num_programs(1) - 1)
    def _():
        st_ref[0] = hbuf[halo + ts - POOL_STATE_LEN:halo + ts, :]


def _pool_prompt(x, g, w_pool, scale, batch, seq, ts=256):
    nt = seq // ts
    return pl.pallas_call(
        functools.partial(_pool_prompt_body, ts=ts),
        grid=(batch, nt),
        in_specs=[
            pl.BlockSpec((ts, D_MODEL), lambda b, s: (b * nt + s, 0)),
            pl.BlockSpec((1, D_MODEL), lambda b, s: (0, 0)),
            pl.BlockSpec((len(POOL_WINDOWS), POOL_GROUP_DIM, POOL_GROUP_DIM), lambda b, s: (0, 0, 0)),
            pl.BlockSpec((1, D_MODEL), lambda b, s: (0, 0)),
        ],
        out_specs=[
            pl.BlockSpec((ts, D_MODEL), lambda b, s: (b * nt + s, 0)),
            pl.BlockSpec((1, POOL_STATE_LEN, D_MODEL), lambda b, s: (b, 0, 0)),
        ],
        out_shape=[
            jax.ShapeDtypeStruct((batch * seq, D_MODEL), F32),
            jax.ShapeDtypeStruct((batch, POOL_STATE_LEN, D_MODEL), F32),
        ],
        scratch_shapes=[pltpu.VMEM((ts + 2 * SUBLANES, D_MODEL), F32)],
        compiler_params=_cparams("arbitrary", "arbitrary"),
        name="pool_prompt",
    )(x, g, w_pool, scale)


def _pool_sample_body(x_ref, st_ref, g_ref, w_ref, sc_ref, o_ref, h_ref):
    x = x_ref[...]
    h_ref[...] = _rms(x, g_ref[...])

    def window_sum(w, cols, hc):
        acc = hc
        for k in range(1, w):
            acc = acc + st_ref[POOL_STATE_LEN - k, :, cols]
        return acc

    _pool_groups(lambda cols: h_ref[:, cols], x, w_ref, sc_ref, o_ref, window_sum, lambda w: 1.0 / w)


def _pool_sample(x, state_t, g, w_pool, scale, tb=32):
    b = x.shape[0]
    return pl.pallas_call(
        _pool_sample_body,
        grid=(b // tb,),
        in_specs=[
            pl.BlockSpec((tb, D_MODEL), lambda i: (i, 0)),
            pl.BlockSpec((POOL_STATE_LEN, tb, D_MODEL), lambda i: (0, i, 0)),
            pl.BlockSpec((1, D_MODEL), lambda i: (0, 0)),
            pl.BlockSpec((len(POOL_WINDOWS), POOL_GROUP_DIM, POOL_GROUP_DIM), lambda i: (0, 0, 0)),
            pl.BlockSpec((1, D_MODEL), lambda i: (0, 0)),
        ],
        out_specs=[pl.BlockSpec((tb, D_MODEL), lambda i: (i, 0)), pl.BlockSpec((tb, D_MODEL), lambda i: (i, 0))],
        out_shape=[jax.ShapeDtypeStruct((b, D_MODEL), F32), jax.ShapeDtypeStruct((b, D_MODEL), F32)],
        compiler_params=_cparams("parallel"),
        name="pool_sample",
    )(x, state_t, g, w_pool, scale)


def _proj_body(x_ref, g_ref, w_ref, wg_ref, cos_ref, sin_ref,
               qraw_ref, qrot_ref, kc_ref, vc_ref, ks_ref, vs_ref, kw_ref, vw_ref,
               kshm_ref, vshm_ref, kwhm_ref, vwhm_ref, gates_ref, *, tm):
    xn = _rms(x_ref[...], g_ref[...]).astype(BF16)
    gall = jax.nn.sigmoid(_dot(xn, wg_ref[...]))
    per_head = N_BRANCHES * GQA_GROUP
    for kvh in range(N_KV_HEADS):
        gates_ref[kvh] = gall if kvh == 0 else pltpu.roll(gall, LANES - kvh * per_head, 1)

    cos, sin = cos_ref[...], sin_ref[...]
    rope = lambda s: s * cos + pltpu.roll(s, HEAD_DIM // 2, 1) * sin
    tile = lambda j: _dot(xn, w_ref[:, j * KV_DIM:(j + 1) * KV_DIM])
    heads = lambda a: [a[:, h * HEAD_DIM:(h + 1) * HEAD_DIM] for h in range(N_KV_HEADS)]

    nq = Q_DIM // KV_DIM
    for j in range(nq):
        zs = tile(j) * (SCALE * LOG2E)
        qraw_ref[:, j * KV_DIM:(j + 1) * KV_DIM] = zs.astype(BF16)
        for h, s in enumerate(heads(zs)):
            c0 = j * KV_DIM + h * HEAD_DIM
            qrot_ref[:, c0:c0 + HEAD_DIM] = rope(s).astype(BF16)

    kinds = [(kc_ref, None, False), (vc_ref, None, False), (ks_ref, kshm_ref, True), (vs_ref, vshm_ref, False),
             (kw_ref, kwhm_ref, True), (vw_ref, vwhm_ref, False)]
    for off, (ref, hm_ref, rot) in enumerate(kinds):
        for h, s in enumerate(heads(tile(nq + off))):
            s = rope(s) if rot else s
            ref[pl.ds(h, tm, stride=N_KV_HEADS), :] = s
            if hm_ref is not None:
                hm_ref[h] = s.astype(BF16)


def _nsa_project(x, g, w_main, w_gates, cos, sin, rope_period_tiles, tm):
    m = x.shape[0]
    row = lambda i: (i, 0)
    const = lambda i: (0, 0)
    kv_sds = jax.ShapeDtypeStruct((m * N_KV_HEADS, HEAD_DIM), F32)
    hm_sds = jax.ShapeDtypeStruct((N_KV_HEADS, m, HEAD_DIM), BF16)
    q_sds = jax.ShapeDtypeStruct((m, Q_DIM), BF16)
    kv_spec = pl.BlockSpec((tm * N_KV_HEADS, HEAD_DIM), row)
    hm_spec = pl.BlockSpec((N_KV_HEADS, tm, HEAD_DIM), lambda i: (0, i, 0))
    q_spec = pl.BlockSpec((tm, Q_DIM), row)
    return pl.pallas_call(
        functools.partial(_proj_body, tm=tm),
        grid=(m // tm,),
        in_specs=[
            pl.BlockSpec((tm, D_MODEL), row),
            pl.BlockSpec((1, D_MODEL), const),
            pl.BlockSpec(w_main.shape, const, pipeline_mode=pl.Buffered(1)),
            pl.BlockSpec(w_gates.shape, const, pipeline_mode=pl.Buffered(1)),
            pl.BlockSpec((tm, HEAD_DIM), lambda i: (i % rope_period_tiles, 0)),
            pl.BlockSpec((tm, HEAD_DIM), lambda i: (i % rope_period_tiles, 0)),
        ],
        out_specs=[q_spec, q_spec] + [kv_spec] * 6 + [hm_spec] * 4
        + [pl.BlockSpec((N_KV_HEADS, tm, LANES), lambda i: (0, i, 0))],
        out_shape=[q_sds, q_sds] + [kv_sds] * 6 + [hm_sds] * 4
        + [jax.ShapeDtypeStruct((N_KV_HEADS, m, LANES), F32)],
        compiler_params=_cparams("parallel"),
        name="nsa_project",
    )(x, g, w_main, w_gates, cos, sin)


N_PAIR = CMP_STRIDE // 2


def _compress(streams, n_sub):
    n_rows = n_sub * N_KV_HEADS
    low = lax.broadcasted_iota(jnp.int32, (1, SUBLANES, 1), 1) < N_KV_HEADS

    def repack(v):
        pairs = lambda a: a.reshape(n_sub // 2, 2, SUBLANES, HEAD_DIM)
        v4, r4 = pairs(v), pairs(pltpu.roll(v, N_KV_HEADS, 0))
        ev = jnp.where(low, v4[:, 0], r4[:, 1]).reshape(n_rows, HEAD_DIM)
        od = jnp.where(low, r4[:, 1], v4[:, 1]).reshape(n_rows, HEAD_DIM)
        return jnp.concatenate([ev, od], axis=1).astype(BF16)

    accs = [jnp.zeros((n_rows, 2 * HEAD_DIM), F32) for _ in streams]
    biases = [jnp.zeros((SUBLANES, 2 * HEAD_DIM), F32) for _ in streams]
    for k in range(N_PAIR):
        for i, (get_rows, wpair_ref, pepair_ref, _) in enumerate(streams):
            accs[i] = accs[i] + _dot(repack(get_rows(k)), wpair_ref[k])
            biases[i] = biases[i] + _dot(pepair_ref[k].astype(BF16), wpair_ref[k])
    outs = []
    for acc, bias, (_, _, _, w2_ref) in zip(accs, biases, streams):
        first = acc[:, :HEAD_DIM] + bias[0:1, :HEAD_DIM]
        second = acc[:, HEAD_DIM:] + bias[1:2, HEAD_DIM:]
        pre = first + pltpu.roll(second, n_rows - N_KV_HEADS, 0)
        outs.append(_dot(jax.nn.gelu(pre, approximate=True).astype(BF16), w2_ref[...]))
    return outs


def _compress_body(xk_ref, xv_ref, wpk_ref, pek_ref, w2k_ref, wpv_ref, pev_ref, w2v_ref, ok_ref, ov_ref, res_ref, *,
                   n_sub):
    rows = lambda x_ref: lambda k: x_ref[:, k * SUBLANES:(k + 1) * SUBLANES, :].reshape(n_sub * SUBLANES, HEAD_DIM)
    outs = _compress([(rows(xk_ref), wpk_ref, pek_ref, w2k_ref), (rows(xv_ref), wpv_ref, pev_ref, w2v_ref)], n_sub)
    for res, o_ref in zip(outs, (ok_ref, ov_ref)):
        res_ref[...] = res
        for h in range(N_KV_HEADS):
            o_ref[0, h] = res_ref[pl.ds(h, n_sub, stride=N_KV_HEADS), :].astype(BF16)


def _compress_prompt(rows_k, rows_v, cmp_k_w, cmp_v_w, batch, seq):
    n_sub = seq // CMP_STRIDE
    chunk = CMP_STRIDE * N_KV_HEADS
    x3 = lambda rows: rows.reshape(batch * n_sub, chunk, HEAD_DIM)
    xspec = pl.BlockSpec((n_sub, chunk, HEAD_DIM), lambda b: (b, 0, 0))
    wspecs = [pl.BlockSpec((N_PAIR, 2 * HEAD_DIM, 2 * HEAD_DIM), lambda b: (0, 0, 0)),
              pl.BlockSpec((N_PAIR, SUBLANES, 2 * HEAD_DIM), lambda b: (0, 0, 0)),
              pl.BlockSpec((HEAD_DIM, HEAD_DIM), lambda b: (0, 0))]
    ospec = pl.BlockSpec((1, N_KV_HEADS, n_sub, HEAD_DIM), lambda b: (b, 0, 0, 0))
    osds = jax.ShapeDtypeStruct((batch, N_KV_HEADS, n_sub, HEAD_DIM), BF16)
    return pl.pallas_call(
        functools.partial(_compress_body, n_sub=n_sub),
        grid=(batch,),
        in_specs=[xspec, xspec] + wspecs + wspecs,
        out_specs=[ospec, ospec],
        out_shape=[osds, osds],
        scratch_shapes=[pltpu.VMEM((n_sub * N_KV_HEADS, HEAD_DIM), F32)],
        compiler_params=_cparams("parallel"),
        name="compress_prompt",
    )(x3(rows_k), x3(rows_v), *cmp_k_w, *cmp_v_w)


def _compress_weights(pe, w1, w2):
    w1r = w1.reshape(2, N_PAIR, 2, HEAD_DIM, HEAD_DIM)
    wpair = jnp.transpose(w1r, (1, 2, 3, 0, 4)).reshape(N_PAIR, 2 * HEAD_DIM, 2 * HEAD_DIM).astype(BF16)
    per = pe.reshape(2, N_PAIR, 2 * HEAD_DIM)
    pepair = jnp.zeros((N_PAIR, SUBLANES, 2 * HEAD_DIM), F32).at[:, 0:2, :].set(jnp.transpose(per, (1, 0, 2)))
    return wpair, pepair, w2.astype(BF16)


def _topk_select(score, valid, n_blk):
    lane = lax.broadcasted_iota(jnp.int32, score.shape, 1)
    rank = jnp.zeros(score.shape, F32)
    for i in range(n_blk):
        ci = score[:, i:i + 1]
        beats = (ci > score) | ((ci == score) & (lane > i))
        rank = rank + jnp.where(beats, 1.0, 0.0)
    return (rank < SEL_TOPK) & valid


def _split_dot(p, w):
    hi = p.astype(BF16)
    lo = (p - hi.astype(F32)).astype(BF16)
    return _dot(hi, w) + _dot(lo, w)


def _attn_prompt_tile(q, qraw_ref, qrot_ref, kcc_ref, vcc_ref, ks_ref, vs_ref, kw_ref, vw_ref, gates_ref, ovt_ref,
                      ex_ref, o_ref, *, tq, seq):
    q0 = q * tq
    kt = tq
    qpos = q0 + lax.broadcasted_iota(jnp.int32, (tq, 1), 0)
    qpos4 = jnp.concatenate([qpos] * GQA_GROUP, axis=0)
    stack = lambda ref: jnp.concatenate([ref[:, g * HEAD_DIM:(g + 1) * HEAD_DIM] for g in range(GQA_GROUP)], axis=0)
    q_raw = stack(qraw_ref)
    q_rot = stack(qrot_ref)

    n_cmp = seq // CMP_STRIDE
    sc = _dot_nt(q_raw, kcc_ref[0, 0])
    c_end = lax.broadcasted_iota(jnp.int32, (1, n_cmp), 1) * CMP_STRIDE + (CMP_BLOCK - 1)
    cmask = c_end <= qpos4
    scm = jnp.where(cmask, sc, NEG_INF)
    e = jnp.where(cmask, jnp.exp2(scm - jnp.max(scm, axis=-1, keepdims=True)), 0.0)
    pc = e / jnp.maximum(jnp.sum(e, axis=-1, keepdims=True), TINY)
    o_cmp = _dot(pc.astype(BF16), vcc_ref[0, 0])

    ranked = q0 + tq > SEL_TOPK * SEL_BLOCK
    if ranked:
        n_blk = seq // SEL_BLOCK
        psum = pc[0:tq] + pc[tq:2 * tq] + pc[2 * tq:3 * tq] + pc[3 * tq:4 * tq]
        hi = psum.astype(BF16)
        lo = (psum - hi.astype(F32)).astype(BF16)
        imp = _dot_nt(ovt_ref[...], hi) + _dot_nt(ovt_ref[...], lo)
        blk = lax.broadcasted_iota(jnp.int32, (n_blk, 1), 0)
        qlane = q0 + lax.broadcasted_iota(jnp.int32, (1, tq), 1)
        cur = lax.shift_right_logical(qlane, SEL_BLOCK.bit_length() - 1)
        valid = blk * SEL_BLOCK <= qlane
        forced = (blk == 0) | (blk == cur) | (blk == cur - 1)
        score = jnp.where(valid, jnp.where(forced, FORCE_SCORE, imp), NEG_INF)
        rank = jnp.zeros((n_blk, tq), F32)
        for i in range((q0 + tq) // SEL_BLOCK):
            ci = score[i:i + 1, :]
            beats = (ci > score) | ((ci == score) & (blk > i))
            rank = rank + jnp.where(beats, 1.0, 0.0)
        sel_t = jnp.where((rank < SEL_TOPK) & valid, 1.0, 0.0)
        sel = jnp.concatenate([sel_t, jnp.zeros((LANES - n_blk, tq), F32)], axis=0).T.astype(BF16)

    def attend(k_ref, v_ref, chunks):
        ss = []
        for c, bias in chunks:
            s = _dot_nt(q_rot, k_ref[0, c * kt:(c + 1) * kt, :])
            ss.append(s if bias is None else s + jnp.concatenate([bias] * GQA_GROUP, axis=0))
        top = ss[0]
        for s in ss[1:]:
            top = jnp.maximum(top, s)
        m = jnp.max(top, axis=-1, keepdims=True)
        tot = None
        acc = None
        for (c, _), s in zip(chunks, ss):
            p = jnp.exp2(s - m)
            tot = p if tot is None else tot + p
            pv = _dot(p.astype(BF16), v_ref[0, c * kt:(c + 1) * kt, :])
            acc = pv if acc is None else acc + pv
        return acc / jnp.sum(tot, axis=-1, keepdims=True)

    def mask_bias(c, ok):
        kpos = c * kt + lax.broadcasted_iota(jnp.int32, (1, kt), 1)
        return jnp.where(ok(kpos), 0.0, NEG_INF)

    sel_chunks = []
    for c in range(q + 1):
        if ranked:
            chosen = _dot(sel, ex_ref[:, c * kt:(c + 1) * kt]) > 0.5
            sel_chunks.append((c, mask_bias(c, lambda kpos: chosen & (kpos <= qpos))))
        else:
            sel_chunks.append((c, mask_bias(c, lambda kpos: kpos <= qpos) if c == q else None))
    o_sel = attend(ks_ref, vs_ref, sel_chunks)

    win_chunks = []
    for c in range(q + 1):
        if c * kt + kt - 1 <= q0 - WINDOW:
            continue
        whole = c * kt + kt - 1 <= q0 and c * kt > q0 + tq - 1 - WINDOW
        win_chunks.append((c, None if whole else mask_bias(c, lambda kpos: (kpos <= qpos) & (kpos > qpos - WINDOW))))
    o_win = attend(kw_ref, vw_ref, win_chunks)

    gates = gates_ref[0]
    for g in range(GQA_GROUP):
        r = slice(g * tq, (g + 1) * tq)
        gate = lambda br: gates[:, br * GQA_GROUP + g:br * GQA_GROUP + g + 1]
        o = gate(0) * o_cmp[r] + gate(1) * o_sel[r] + gate(2) * o_win[r]
        o_ref[:, g * HEAD_DIM:(g + 1) * HEAD_DIM] = o.astype(BF16)


def _attn_prompt_body(*refs, tq, seq):
    qi = pl.program_id(2)
    for q in range(seq // tq):
        @pl.when(qi == q)
        def _(q=q):
            _attn_prompt_tile(q, *refs, tq=tq, seq=seq)


def _attn_prompt(q_raw, q_rot, kcc, vcc, ks_hm, vs_hm, kw_hm, vw_hm, gates, ovt, ex, batch, seq, tq=256):
    nq = seq // tq
    n_cmp = seq // CMP_STRIDE
    qspec = pl.BlockSpec((tq, KV_DIM), lambda b, h, i: (b * nq + i, h))
    cspec = pl.BlockSpec((1, 1, n_cmp, HEAD_DIM), lambda b, h, i: (b, h, 0, 0))
    kvspec = pl.BlockSpec((1, seq, HEAD_DIM), lambda b, h, i: (h, b, 0))
    return pl.pallas_call(
        functools.partial(_attn_prompt_body, tq=tq, seq=seq),
        grid=(batch, N_KV_HEADS, nq),
        in_specs=[qspec, qspec, cspec, cspec, kvspec, kvspec, kvspec, kvspec,
                  pl.BlockSpec((1, tq, LANES), lambda b, h, i: (h, b * nq + i, 0)),
                  pl.BlockSpec(ovt.shape, lambda b, h, i: (0, 0)),
                  pl.BlockSpec((LANES, seq), lambda b, h, i: (0, 0))],
        out_specs=qspec,
        out_shape=jax.ShapeDtypeStruct((batch * seq, Q_DIM), BF16),
        compiler_params=_cparams("parallel", "parallel", "arbitrary"),
        name="attn_prompt",
    )(q_raw, q_rot, kcc, vcc, ks_hm, vs_hm, kw_hm, vw_hm, gates, ovt, ex)


def _matmul_res_body(a_ref, w_ref, r_ref, o_ref):
    o_ref[...] = r_ref[...] + _dot(a_ref[...], w_ref[...])


def _matmul_residual(a, w, res, tm):
    m, k = a.shape
    n = w.shape[1]
    return pl.pallas_call(
        _matmul_res_body,
        grid=(m // tm,),
        in_specs=[pl.BlockSpec((tm, k), lambda i: (i, 0)), pl.BlockSpec((k, n), lambda i: (0, 0)),
                  pl.BlockSpec((tm, n), lambda i: (i, 0))],
        out_specs=pl.BlockSpec((tm, n), lambda i: (i, 0)),
        out_shape=jax.ShapeDtypeStruct((m, n), F32),
        compiler_params=_cparams("parallel"),
        name="out_proj",
    )(a, w, res)


def _attn_sample_body(pt_ref, *refs, n_pages, past):
    del pt_ref
    pages = refs[:4 * n_pages]
    (wink_ref, winv_ref, qraw_ref, qrot_ref, kse_ref, vse_ref, kwe_ref, vwe_ref, gates_ref,
     wpk_ref, pek_ref, w2k_ref, wpv_ref, pev_ref, w2v_ref, ov_ref, gs_ref, o_ref, s_ref) = refs[4 * n_pages:]
    ck, cv, sk, sv = (pages[i * n_pages:(i + 1) * n_pages] for i in range(4))
    n_sub = past // CMP_STRIDE
    n_rows = n_sub * N_KV_HEADS
    sub_pp = PAGE_SIZE // CMP_STRIDE
    page_rows = PAGE_SIZE * N_KV_HEADS

    def rows_of(pg):
        return lambda k: jnp.concatenate(
            [r[0, :, k * SUBLANES:(k + 1) * SUBLANES, :].reshape(sub_pp * SUBLANES, HEAD_DIM) for r in pg], axis=0)

    kcc, vcc = _compress([(rows_of(ck), wpk_ref, pek_ref, w2k_ref), (rows_of(cv), wpv_ref, pev_ref, w2v_ref)], n_sub)
    kcc, vcc = kcc.astype(BF16), vcc.astype(BF16)

    q_raw = qraw_ref[0]
    q_rot = qrot_ref[0]
    q_rot32 = q_rot.astype(F32)
    hrow = lax.shift_right_logical(lax.broadcasted_iota(jnp.int32, (N_HEADS, 1), 0), 2)

    def softmax_parts(s_old, s_new):
        m = jnp.maximum(jnp.max(s_old, axis=-1, keepdims=True), s_new)
        p_old = jnp.exp2(s_old - m)
        p_new = jnp.exp2(s_new - m)
        return p_old, p_new, jnp.sum(p_old, axis=-1, keepdims=True) + p_new

    col = lax.broadcasted_iota(jnp.int32, (1, n_rows), 1)
    cmask = ((col & (N_KV_HEADS - 1)) == hrow) & (
        lax.shift_right_logical(col, 2) * CMP_STRIDE + (CMP_BLOCK - 1) <= past)
    scm = jnp.where(cmask, _dot_nt(q_raw, kcc), NEG_INF)
    e = jnp.where(cmask, jnp.exp2(scm - jnp.max(scm, axis=-1, keepdims=True)), 0.0)
    pc = e / jnp.maximum(jnp.sum(e, axis=-1, keepdims=True), TINY)
    o_cmp = _dot(pc.astype(BF16), vcc)

    imp16 = _split_dot(pc, ov_ref[...])
    hi = imp16.astype(BF16)
    lo = (imp16 - hi.astype(F32)).astype(BF16)
    imp = _dot(gs_ref[...], hi) + _dot(gs_ref[...], lo)
    n_blk = past // SEL_BLOCK + 1
    cur = past // SEL_BLOCK
    blk = lax.broadcasted_iota(jnp.int32, (N_HEADS, LANES), 1)
    valid = blk < n_blk
    forced = (blk == 0) | (blk == cur) | (blk == cur - 1)
    score = jnp.where(valid, jnp.where(forced, FORCE_SCORE, imp), NEG_INF)
    sel = jnp.where(_topk_select(score, valid, n_blk), 1.0, 0.0)

    pcol = lax.broadcasted_iota(jnp.int32, (1, page_rows), 1)
    head_ok = (pcol & (N_KV_HEADS - 1)) == hrow
    upper = pcol >= page_rows // 2
    for p in range(n_pages):
        s = _dot_nt(q_rot, sk[p][0].reshape(page_rows, HEAD_DIM).astype(BF16))
        chosen = jnp.where(upper, sel[:, 2 * p + 1:2 * p + 2], sel[:, 2 * p:2 * p + 1])
        s_ref[:, p * page_rows:(p + 1) * page_rows] = jnp.where(head_ok & (chosen > 0.5), s, NEG_INF)
    s_new = jnp.sum(q_rot32 * kse_ref[0], axis=-1, keepdims=True)
    p_old, p_new, denom = softmax_parts(s_ref[...], s_new)
    acc = p_new * vse_ref[0]
    for p in range(n_pages):
        acc = acc + _dot(p_old[:, p * page_rows:(p + 1) * page_rows].astype(BF16),
                         sv[p][0].reshape(page_rows, HEAD_DIM).astype(BF16))
    o_sel = acc / denom

    n_win = wink_ref.shape[1]
    wcol = lax.broadcasted_iota(jnp.int32, (1, n_win), 1)
    wmask = ((wcol & (N_KV_HEADS - 1)) == hrow) & (lax.shift_right_logical(wcol, 2) >= n_win // N_KV_HEADS + 1 - WINDOW)
    sw = jnp.where(wmask, _dot_nt(q_rot, wink_ref[0].astype(BF16)), NEG_INF)
    sw_new = jnp.sum(q_rot32 * kwe_ref[0], axis=-1, keepdims=True)
    p_old, p_new, denom = softmax_parts(sw, sw_new)
    o_win = (p_new * vwe_ref[0] + _dot(p_old.astype(BF16), winv_ref[0].astype(BF16))) / denom

    gates = gates_ref[0]
    o = gates[:, 0:1] * o_cmp + gates[:, 1:2] * o_sel + gates[:, 2:3] * o_win
    o_ref[0] = o.astype(BF16)


def _attn_sample(page_table, caches, win_k, win_v, q_raw, q_rot, new_rows, gates, cmp_k_w, cmp_v_w, ov, gs):
    batch, n_pages = page_table.shape
    past = n_pages * PAGE_SIZE
    chunk = CMP_STRIDE * N_KV_HEADS
    page_specs = [pl.BlockSpec((1, PAGE_SIZE // CMP_STRIDE, chunk, HEAD_DIM), lambda b, pt, p=p: (pt[b, p], 0, 0, 0))
                  for p in range(n_pages)]
    per_b = lambda a: pl.BlockSpec((1,) + a.shape[1:], lambda b, pt: (b,) + (0,) * (a.ndim - 1))
    whole = lambda a: pl.BlockSpec(a.shape, lambda b, pt: (0,) * a.ndim)
    ins, specs = [], []
    for c in caches:
        ins += [c] * n_pages
        specs += page_specs
    for a in (win_k, win_v, q_raw, q_rot, *new_rows, gates):
        ins.append(a)
        specs.append(per_b(a))
    for a in (*cmp_k_w, *cmp_v_w, ov, gs):
        ins.append(a)
        specs.append(whole(a))
    return pl.pallas_call(
        functools.partial(_attn_sample_body, n_pages=n_pages, past=past),
        grid_spec=pltpu.PrefetchScalarGridSpec(
            num_scalar_prefetch=1, grid=(batch,), in_specs=specs,
            out_specs=pl.BlockSpec((1, N_HEADS, HEAD_DIM), lambda b, pt: (b, 0, 0)),
            scratch_shapes=[pltpu.VMEM((N_HEADS, past * N_KV_HEADS), F32)]),
        out_shape=jax.ShapeDtypeStruct((batch, N_HEADS, HEAD_DIM), BF16),
        compiler_params=_cparams("arbitrary"),
        name="attn_sample",
    )(page_table, *ins)


def _win_update_body(sk_ref, sv_ref, nk_ref, nv_ref, ok_ref, ov_ref):
    width = sk_ref.shape[1]
    for s_ref, n_ref, o_ref in ((sk_ref, nk_ref, ok_ref), (sv_ref, nv_ref, ov_ref)):
        o_ref[:, 0:width - 1] = s_ref[:, 1:width]
        o_ref[:, width - 1:width] = n_ref[...]


def _win_update(state_k, state_v, new_k, new_v, tb=4):
    batch, width = state_k.shape[:2]
    big = pl.BlockSpec((tb, width, N_KV_HEADS, HEAD_DIM), lambda i: (i, 0, 0, 0))
    one = pl.BlockSpec((tb, 1, N_KV_HEADS, HEAD_DIM), lambda i: (i, 0, 0, 0))
    sds = jax.ShapeDtypeStruct(state_k.shape, state_k.dtype)
    return pl.pallas_call(
        _win_update_body,
        grid=(batch // tb,),
        in_specs=[big, big, one, one],
        out_specs=[big, big],
        out_shape=[sds, sds],
        compiler_params=_cparams("parallel"),
        name="win_update",
    )(state_k, state_v, new_k, new_v)


def _rope_tables(pos):
    half = HEAD_DIM // 2
    inv_freq = jnp.power(ROPE_THETA, -jnp.arange(half, dtype=F32) / half)
    ang = pos.astype(F32)[:, None] * inv_freq[None, :]
    cos, sin = jnp.cos(ang), jnp.sin(ang)
    return jnp.concatenate([cos, cos], axis=1), jnp.concatenate([-sin, sin], axis=1)


def _overlap(n_sub, n_cols):
    n = jnp.arange(n_sub)[:, None]
    j = jnp.arange(LANES)[None, :]
    c_start, s_start = n * CMP_STRIDE, j * SEL_BLOCK
    hit = (c_start <= s_start + SEL_BLOCK - 1) & (c_start + CMP_BLOCK - 1 >= s_start)
    return (hit & (n < n_sub - 1) & (j < n_cols)).astype(BF16)


def kernel(x_prompt, x_sample, state_pool, cache_cmp_k, cache_cmp_v, cache_sel_k, cache_sel_v, state_win_k, state_win_v, page_table, norm_ffn1, norm_mix, norm_ffn2, norm_final, w_ffn_gate, w_ffn_up, w_ffn_down, w_pool, pool_scale, w_nsa_in, w_nsa_out, cmp_pe_k, cmp_w1_k, cmp_w2_k, cmp_pe_v, cmp_w1_v, cmp_w2_v):
    batch, seq, _ = x_prompt.shape
    dec_batch, dec_seq, _ = x_sample.shape
    past = page_table.shape[1] * PAGE_SIZE
    win_buf = state_win_k.shape[2]
    assert dec_seq == 1 and past % SEL_BLOCK == 0 and win_buf == WINDOW and seq % 256 == 0
    assert norm_ffn1.shape[0] == 2 and w_pool.shape[0] == 1 and w_nsa_in.shape[0] == 1

    vec = lambda v: v.reshape(1, D_MODEL)
    wpool = w_pool[0].astype(BF16)
    n_main = Q_DIM + 6 * KV_DIM
    w_main = w_nsa_in[0][:, :n_main].astype(BF16)
    wgt = w_nsa_in[0][:, n_main:].reshape(D_MODEL, N_KV_HEADS, GQA_GROUP, N_BRANCHES)
    wgt = jnp.transpose(wgt, (0, 1, 3, 2)).reshape(D_MODEL, N_HEADS * N_BRANCHES)
    wgt = jnp.pad(wgt, ((0, 0), (0, LANES - N_HEADS * N_BRANCHES))).astype(BF16)
    w_out = w_nsa_out[0].astype(BF16)
    cmp_k_w = _compress_weights(cmp_pe_k[0], cmp_w1_k[0], cmp_w2_k[0])
    cmp_v_w = _compress_weights(cmp_pe_v[0], cmp_w1_v[0], cmp_w2_v[0])
    ffn = lambda x, g, layer, which, gf=None: _ffn(x, vec(g), w_ffn_gate, w_ffn_up, w_ffn_down, layer, which, g_final=gf)

    xp = x_prompt.reshape(batch * seq, D_MODEL)
    xp = ffn(xp, norm_ffn1[0], 0, 0)
    xp, pool_p = _pool_prompt(xp, vec(norm_mix[0]), wpool, vec(pool_scale[0]), batch, seq)
    xp = ffn(xp, norm_ffn2[0], 0, 1)
    xp = ffn(xp, norm_ffn1[1], 1, 0)
    tm = 256
    cos_p, sin_p = _rope_tables(jnp.arange(seq))
    (q_raw, q_rot, kc_p, vc_p, ks_p, vs_p, kw_p, vw_p, ks_hm, vs_hm, kw_hm, vw_hm, gates_p) = _nsa_project(
        xp, vec(norm_mix[1]), w_main, wgt, cos_p, sin_p, seq // tm, tm)
    kcc, vcc = _compress_prompt(kc_p, vc_p, cmp_k_w, cmp_v_w, batch, seq)
    n_sub = seq // CMP_STRIDE
    expand = (jnp.arange(LANES)[:, None] == jnp.arange(seq)[None, :] // SEL_BLOCK).astype(BF16)
    ovt = jnp.transpose(_overlap(n_sub, seq // SEL_BLOCK))[:seq // SEL_BLOCK]
    o_p = _attn_prompt(q_raw, q_rot, kcc, vcc, ks_hm, vs_hm, kw_hm, vw_hm, gates_p, ovt, expand, batch, seq)
    xp = _matmul_residual(o_p, w_out, xp, 2 * tm)
    y_prompt = ffn(xp, norm_ffn2[1], 1, 1, vec(norm_final)).reshape(batch, seq, D_MODEL)

    xs = x_sample.reshape(dec_batch, D_MODEL)
    xs = ffn(xs, norm_ffn1[0], 0, 0)
    xs, h_s = _pool_sample(xs, jnp.transpose(state_pool[0], (1, 0, 2)), vec(norm_mix[0]), wpool, vec(pool_scale[0]))
    pool_s = jnp.concatenate([state_pool[0][:, 1:], h_s[:, None]], axis=1)
    xs = ffn(xs, norm_ffn2[0], 0, 1)
    xs = ffn(xs, norm_ffn1[1], 1, 0)
    cos_s, sin_s = _rope_tables(jnp.full((dec_batch,), past))
    (q_raw_s, q_rot_s, kc_s, vc_s, ks_s, vs_s, kw_s, vw_s, _, _, _, _, gates_s) = _nsa_project(
        xs, vec(norm_mix[1]), w_main, wgt, cos_s, sin_s, 1, dec_batch)
    per_head = lambda a: jnp.repeat(a.reshape(dec_batch, N_KV_HEADS, HEAD_DIM), GQA_GROUP, axis=1)
    g3 = gates_s[:, :, :N_BRANCHES * GQA_GROUP].reshape(N_KV_HEADS, dec_batch, N_BRANCHES, GQA_GROUP)
    g3 = jnp.transpose(g3, (1, 0, 3, 2)).reshape(dec_batch, N_HEADS, N_BRANCHES)
    chunk = CMP_STRIDE * N_KV_HEADS
    paged = lambda c: c[0].reshape(c.shape[1], PAGE_SIZE // CMP_STRIDE, chunk, HEAD_DIM)
    win_rows = lambda s: s[0].reshape(dec_batch, win_buf * N_KV_HEADS, HEAD_DIM)
    gs = (jnp.arange(N_HEADS)[:, None] // GQA_GROUP == jnp.arange(N_HEADS)[None, :] // GQA_GROUP).astype(BF16)
    n_sub_s = past // CMP_STRIDE
    ov_s = jnp.repeat(_overlap(n_sub_s, past // SEL_BLOCK + 1), N_KV_HEADS, axis=0)
    o_s = _attn_sample(page_table, [paged(c) for c in (cache_cmp_k, cache_cmp_v, cache_sel_k, cache_sel_v)],
                       win_rows(state_win_k), win_rows(state_win_v),
                       q_raw_s.reshape(dec_batch, N_HEADS, HEAD_DIM), q_rot_s.reshape(dec_batch, N_HEADS, HEAD_DIM),
                       [per_head(a) for a in (ks_s, vs_s, kw_s, vw_s)], g3, cmp_k_w, cmp_v_w, ov_s, gs)
    xs = _matmul_residual(o_s.reshape(dec_batch, Q_DIM), w_out, xs, dec_batch)
    y_sample = ffn(xs, norm_ffn2[1], 1, 1, vec(norm_final)).reshape(dec_batch, 1, D_MODEL)

    kv5 = lambda a, b, s: a.reshape(1, b, s, N_KV_HEADS, HEAD_DIM)
    keep = min(WINDOW, seq)
    new_row = lambda a: a.reshape(dec_batch, 1, N_KV_HEADS, HEAD_DIM)
    win_k_s, win_v_s = _win_update(state_win_k[0], state_win_v[0], new_row(kw_s), new_row(vw_s))
    return (y_prompt, y_sample, pool_p[None], pool_s[None],
            kv5(kc_p, batch, seq), kv5(kc_s, dec_batch, 1), kv5(vc_p, batch, seq), kv5(vc_s, dec_batch, 1),
            kv5(ks_p, batch, seq), kv5(ks_s, dec_batch, 1), kv5(vs_p, batch, seq), kv5(vs_s, dec_batch, 1),
            kv5(kw_p, batch, seq)[:, :, seq - keep:], win_k_s[None],
            kv5(vw_p, batch, seq)[:, :, seq - keep:], win_v_s[None])
```

```python
import functools

import jax
import jax.numpy as jnp
from jax import lax
from jax.experimental import pallas as pl
from jax.experimental.pallas import tpu as pltpu

F32 = jnp.float32
BF16 = jnp.bfloat16

D_MODEL = 2048
D_FF = 5632
POOL_WINDOWS = (2, 4, 8, 16)
POOL_GROUP_DIM = D_MODEL // len(POOL_WINDOWS)
POOL_STATE_LEN = max(POOL_WINDOWS) - 1
N_HEADS = 16
HEAD_DIM = 128
N_KV_HEADS = 4
GQA_GROUP = N_HEADS // N_KV_HEADS
Q_DIM = N_HEADS * HEAD_DIM
KV_DIM = N_KV_HEADS * HEAD_DIM
N_BRANCHES = 3
CMP_BLOCK = 32
CMP_STRIDE = 16
SEL_BLOCK = 64
SEL_TOPK = 16
ATTN_TQ = 256
WINDOW = 512
PAGE_SIZE = 128
SCALE = HEAD_DIM ** -0.5
LOG2E = 1.4426950408889634
ROPE_THETA = 10000.0
RMS_EPS = 1e-6
NEG_INF = -1e30
FORCE_SCORE = 1e9
TINY = 1e-30

LANES = 128
SUBLANES = 8
VMEM_LIMIT = 56 * 1024 * 1024


def _cparams(*sem):
    return pltpu.CompilerParams(dimension_semantics=sem, vmem_limit_bytes=VMEM_LIMIT)


def _rms(x, g):
    y = x * lax.rsqrt(jnp.mean(x * x, axis=-1, keepdims=True) + RMS_EPS)
    return y * g


def _dot(a, b):
    return jnp.dot(a, b, preferred_element_type=F32)


def _dot_nt(a, b):
    return lax.dot_general(a, b, (((1,), (1,)), ((), ())), preferred_element_type=F32)


def _ffn_body(xp_ref, xs_ref, g_ref, wg_ref, wu_ref, wd_ref, *rest, final_norm, tm, ts):
    if final_norm:
        gf_ref, op_ref, os_ref, xn_ref = rest
    else:
        op_ref, os_ref, xn_ref = rest
    i, f = pl.program_id(0), pl.program_id(1)
    last = pl.num_programs(1) - 1

    @pl.when(f == 0)
    def _():
        x = xp_ref[...]
        xn_ref[0:tm] = _rms(x, g_ref[...]).astype(BF16)
        op_ref[...] = x

    @pl.when((f == 0) & (i == 0))
    def _():
        x = xs_ref[...]
        xn_ref[tm:tm + ts] = _rms(x, g_ref[...]).astype(BF16)
        os_ref[...] = x

    def half_step(rows):
        xn = xn_ref[0:rows]
        gate = _dot(xn, wg_ref[...].astype(BF16))
        up = _dot(xn, wu_ref[...].astype(BF16))
        h = (gate * jax.nn.sigmoid(gate)) * up
        return _dot((0.5 * h).astype(BF16), wd_ref[...].astype(BF16))

    @pl.when(i == 0)
    def _():
        y = half_step(tm + ts)
        op_ref[...] += y[0:tm]
        os_ref[...] += y[tm:tm + ts]

    @pl.when(i > 0)
    def _():
        op_ref[...] += half_step(tm)

    if final_norm:
        @pl.when(f == last)
        def _():
            op_ref[...] = _rms(op_ref[...], gf_ref[...])

        @pl.when((f == last) & (i == 0))
        def _():
            os_ref[...] = _rms(os_ref[...], gf_ref[...])


def _ffn(xp, xs, g, wg, wu, wd, layer, which, g_final=None, tm=1024, tf=256):
    m, ts = xp.shape[0], xs.shape[0]
    final_norm = g_final is not None
    row = lambda i, f: (i, 0)
    const = lambda i, f: (0, 0)
    in_specs = [
        pl.BlockSpec((tm, D_MODEL), row),
        pl.BlockSpec((ts, D_MODEL), const),
        pl.BlockSpec((1, D_MODEL), const),
        pl.BlockSpec((None, None, D_MODEL, tf), lambda i, f: (layer, which, 0, f)),
        pl.BlockSpec((None, None, D_MODEL, tf), lambda i, f: (layer, which, 0, f)),
        pl.BlockSpec((None, None, tf, D_MODEL), lambda i, f: (layer, which, f, 0)),
    ]
    args = [xp, xs, g, wg, wu, wd]
    if final_norm:
        in_specs.append(pl.BlockSpec((1, D_MODEL), const))
        args.append(g_final)
    return pl.pallas_call(
        functools.partial(_ffn_body, final_norm=final_norm, tm=tm, ts=ts),
        grid=(m // tm, D_FF // tf),
        in_specs=in_specs,
        out_specs=[pl.BlockSpec((tm, D_MODEL), row), pl.BlockSpec((ts, D_MODEL), const)],
        out_shape=[jax.ShapeDtypeStruct((m, D_MODEL), F32), jax.ShapeDtypeStruct((ts, D_MODEL), F32)],
        scratch_shapes=[pltpu.VMEM((tm + ts, D_MODEL), BF16)],
        compiler_params=_cparams("arbitrary", "arbitrary"),
        name="ffn",
    )(*args)


def _pool_groups(h_cols, x, w_ref, sc_ref, o_ref, window_sum, inv_cnt):
    for gi, w in enumerate(POOL_WINDOWS):
        c0 = gi * POOL_GROUP_DIM
        cols = slice(c0, c0 + POOL_GROUP_DIM)
        hc = h_cols(cols)
        d = window_sum(w, cols, hc) * inv_cnt(w) - hc
        y = _dot(d.astype(BF16), w_ref[gi])
        o_ref[:, cols] = x[:, cols] + y * sc_ref[:, cols]


def _pool_prompt_body(x_ref, g_ref, w_ref, sc_ref, o_ref, st_ref, hbuf, *, ts):
    s = pl.program_id(1)
    halo = 2 * SUBLANES

    @pl.when(s == 0)
    def _():
        hbuf[0:halo, :] = jnp.zeros((halo, D_MODEL), F32)

    @pl.when(s > 0)
    def _():
        hbuf[0:halo, :] = hbuf[ts:ts + halo, :]

    x = x_ref[...]
    hbuf[halo:halo + ts, :] = _rms(x, g_ref[...])
    pos = s * ts + lax.broadcasted_iota(jnp.int32, (ts, 1), 0)

    def window_sum(w, cols, hc):
        acc = hc
        for k in range(1, w):
            acc = acc + hbuf[halo - k:halo - k + ts, cols]
        return acc

    _pool_groups(lambda cols: hbuf[halo:halo + ts, cols], x, w_ref, sc_ref, o_ref, window_sum,
                 lambda w: 1.0 / jnp.minimum(w, pos + 1).astype(F32))

    @pl.when(s == pl.num_programs(1) - 1)
    def _():
        st_ref[0] = hbuf[halo + ts - POOL_STATE_LEN:halo + ts, :]


def _pool_prompt(x, g, w_pool, scale, batch, seq, ts=256):
    nt = seq // ts
    return pl.pallas_call(
        functools.partial(_pool_prompt_body, ts=ts),
        grid=(batch, nt),
        in_specs=[
            pl.BlockSpec((ts, D_MODEL), lambda b, s: (b * nt + s, 0)),
            pl.BlockSpec((1, D_MODEL), lambda b, s: (0, 0)),
            pl.BlockSpec((len(POOL_WINDOWS), POOL_GROUP_DIM, POOL_GROUP_DIM), lambda b, s: (0, 0, 0)),
            pl.BlockSpec((1, D_MODEL), lambda b, s: (0, 0)),
        ],
        out_specs=[
            pl.BlockSpec((ts, D_MODEL), lambda b, s: (b * nt + s, 0)),
            pl.BlockSpec((1, POOL_STATE_LEN, D_MODEL), lambda b, s: (b, 0, 0)),
        ],
        out_shape=[
            jax.ShapeDtypeStruct((batch * seq, D_MODEL), F32),
            jax.ShapeDtypeStruct((batch, POOL_STATE_LEN, D_MODEL), F32),
        ],
        scratch_shapes=[pltpu.VMEM((ts + 2 * SUBLANES, D_MODEL), F32)],
        compiler_params=_cparams("arbitrary", "arbitrary"),
        name="pool_prompt",
    )(x, g, w_pool, scale)


def _pool_sample_body(x_ref, st_ref, g_ref, w_ref, sc_ref, o_ref, h_ref):
    x = x_ref[...]
    h_ref[...] = _rms(x, g_ref[...])

    def window_sum(w, cols, hc):
        acc = hc
        for k in range(1, w):
            acc = acc + st_ref[POOL_STATE_LEN - k, :, cols]
        return acc

    _pool_groups(lambda cols: h_ref[:, cols], x, w_ref, sc_ref, o_ref, window_sum, lambda w: 1.0 / w)


def _pool_sample(x, state_t, g, w_pool, scale, tb=32):
    b = x.shape[0]
    return pl.pallas_call(
        _pool_sample_body,
        grid=(b // tb,),
        in_specs=[
            pl.BlockSpec((tb, D_MODEL), lambda i: (i, 0)),
            pl.BlockSpec((POOL_STATE_LEN, tb, D_MODEL), lambda i: (0, i, 0)),
            pl.BlockSpec((1, D_MODEL), lambda i: (0, 0)),
            pl.BlockSpec((len(POOL_WINDOWS), POOL_GROUP_DIM, POOL_GROUP_DIM), lambda i: (0, 0, 0)),
            pl.BlockSpec((1, D_MODEL), lambda i: (0, 0)),
        ],
        out_specs=[pl.BlockSpec((tb, D_MODEL), lambda i: (i, 0)), pl.BlockSpec((tb, D_MODEL), lambda i: (i, 0))],
        out_shape=[jax.ShapeDtypeStruct((b, D_MODEL), F32), jax.ShapeDtypeStruct((b, D_MODEL), F32)],
        compiler_params=_cparams("parallel"),
        name="pool_sample",
    )(x, state_t, g, w_pool, scale)


def _proj_body(x_ref, g_ref, w_ref, wg_ref, cos_ref, sin_ref,
               qraw_ref, qrot_ref, kc_ref, vc_ref, ks_ref, vs_ref, kw_ref, vw_ref,
               kshm_ref, vshm_ref, kwhm_ref, vwhm_ref, gates_ref, *, tm):
    xn = _rms(x_ref[...], g_ref[...]).astype(BF16)
    gall = jax.nn.sigmoid(_dot(xn, wg_ref[...]))
    per_head = N_BRANCHES * GQA_GROUP
    for kvh in range(N_KV_HEADS):
        gates_ref[kvh] = gall if kvh == 0 else pltpu.roll(gall, LANES - kvh * per_head, 1)

    cos, sin = cos_ref[...], sin_ref[...]
    rope = lambda s: s * cos + pltpu.roll(s, HEAD_DIM // 2, 1) * sin
    tile = lambda j: _dot(xn, w_ref[:, j * KV_DIM:(j + 1) * KV_DIM])
    heads = lambda a: [a[:, h * HEAD_DIM:(h + 1) * HEAD_DIM] for h in range(N_KV_HEADS)]

    nq = Q_DIM // KV_DIM
    for j in range(nq):
        zs = tile(j) * (SCALE * LOG2E)
        qraw_ref[:, j * KV_DIM:(j + 1) * KV_DIM] = zs.astype(BF16)
        for h, s in enumerate(heads(zs)):
            c0 = j * KV_DIM + h * HEAD_DIM
            qrot_ref[:, c0:c0 + HEAD_DIM] = rope(s).astype(BF16)

    kinds = [(kc_ref, None, False), (vc_ref, None, False), (ks_ref, kshm_ref, True), (vs_ref, vshm_ref, False),
             (kw_ref, kwhm_ref, True), (vw_ref, vwhm_ref, False)]
    for off, (ref, hm_ref, rot) in enumerate(kinds):
        for h, s in enumerate(heads(tile(nq + off))):
            s = rope(s) if rot else s
            ref[pl.ds(h, tm, stride=N_KV_HEADS), :] = s
            if hm_ref is not None:
                hm_ref[h] = s.astype(BF16)


def _nsa_project(x, g, w_main, w_gates, cos, sin, rope_period_tiles, tm):
    m = x.shape[0]
    row = lambda i: (i, 0)
    const = lambda i: (0, 0)
    kv_sds = jax.ShapeDtypeStruct((m * N_KV_HEADS, HEAD_DIM), F32)
    hm_sds = jax.ShapeDtypeStruct((N_KV_HEADS, m, HEAD_DIM), BF16)
    q_sds = jax.ShapeDtypeStruct((m, Q_DIM), BF16)
    kv_spec = pl.BlockSpec((tm * N_KV_HEADS, HEAD_DIM), row)
    hm_spec = pl.BlockSpec((N_KV_HEADS, tm, HEAD_DIM), lambda i: (0, i, 0))
    q_spec = pl.BlockSpec((tm, Q_DIM), row)
    return pl.pallas_call(
        functools.partial(_proj_body, tm=tm),
        grid=(m // tm,),
        in_specs=[
            pl.BlockSpec((tm, D_MODEL), row),
            pl.BlockSpec((1, D_MODEL), const),
            pl.BlockSpec(w_main.shape, const, pipeline_mode=pl.Buffered(1)),
            pl.BlockSpec(w_gates.shape, const, pipeline_mode=pl.Buffered(1)),
            pl.BlockSpec((tm, HEAD_DIM), lambda i: (i % rope_period_tiles, 0)),
            pl.BlockSpec((tm, HEAD_DIM), lambda i: (i % rope_period_tiles, 0)),
        ],
        out_specs=[q_spec, q_spec] + [kv_spec] * 6 + [hm_spec] * 4
        + [pl.BlockSpec((N_KV_HEADS, tm, LANES), lambda i: (0, i, 0))],
        out_shape=[q_sds, q_sds] + [kv_sds] * 6 + [hm_sds] * 4
        + [jax.ShapeDtypeStruct((N_KV_HEADS, m, LANES), F32)],
        compiler_params=_cparams("parallel"),
        name="nsa_project",
    )(x, g, w_main, w_gates, cos, sin)


N_PAIR = CMP_STRIDE // 2


def _compress(streams, n_sub):
    n_rows = n_sub * N_KV_HEADS
    low = lax.broadcasted_iota(jnp.int32, (1, SUBLANES, 1), 1) < N_KV_HEADS

    def repack(v):
        pairs = lambda a: a.reshape(n_sub // 2, 2, SUBLANES, HEAD_DIM)
        v4, r4 = pairs(v), pairs(pltpu.roll(v, N_KV_HEADS, 0))
        ev = jnp.where(low, v4[:, 0], r4[:, 1]).reshape(n_rows, HEAD_DIM)
        od = jnp.where(low, r4[:, 1], v4[:, 1]).reshape(n_rows, HEAD_DIM)
        return jnp.concatenate([ev, od], axis=1).astype(BF16)

    accs = [jnp.zeros((n_rows, 2 * HEAD_DIM), F32) for _ in streams]
    biases = [jnp.zeros((SUBLANES, 2 * HEAD_DIM), F32) for _ in streams]
    for k in range(N_PAIR):
        for i, (get_rows, wpair_ref, pepair_ref, _) in enumerate(streams):
            accs[i] = accs[i] + _dot(repack(get_rows(k)), wpair_ref[k])
            biases[i] = biases[i] + _dot(pepair_ref[k].astype(BF16), wpair_ref[k])
    outs = []
    for acc, bias, (_, _, _, w2_ref) in zip(accs, biases, streams):
        first = acc[:, :HEAD_DIM] + bias[0:1, :HEAD_DIM]
        second = acc[:, HEAD_DIM:] + bias[1:2, HEAD_DIM:]
        pre = first + pltpu.roll(second, n_rows - N_KV_HEADS, 0)
        outs.append(_dot(jax.nn.gelu(pre, approximate=True).astype(BF16), w2_ref[...]))
    return outs


def _compress_body(xk_ref, xv_ref, wpk_ref, pek_ref, w2k_ref, wpv_ref, pev_ref, w2v_ref, ok_ref, ov_ref, res_ref, *,
                   n_sub):
    rows = lambda x_ref: lambda k: x_ref[:, k * SUBLANES:(k + 1) * SUBLANES, :].reshape(n_sub * SUBLANES, HEAD_DIM)
    outs = _compress([(rows(xk_ref), wpk_ref, pek_ref, w2k_ref), (rows(xv_ref), wpv_ref, pev_ref, w2v_ref)], n_sub)
    for res, o_ref in zip(outs, (ok_ref, ov_ref)):
        res_ref[...] = res
        for h in range(N_KV_HEADS):
            o_ref[0, h] = res_ref[pl.ds(h, n_sub, stride=N_KV_HEADS), :].astype(BF16)


def _compress_prompt(rows_k, rows_v, cmp_k_w, cmp_v_w, batch, seq):
    n_sub = seq // CMP_STRIDE
    chunk = CMP_STRIDE * N_KV_HEADS
    x3 = lambda rows: rows.reshape(batch * n_sub, chunk, HEAD_DIM)
    xspec = pl.BlockSpec((n_sub, chunk, HEAD_DIM), lambda b: (b, 0, 0))
    wspecs = [pl.BlockSpec((N_PAIR, 2 * HEAD_DIM, 2 * HEAD_DIM), lambda b: (0, 0, 0)),
              pl.BlockSpec((N_PAIR, SUBLANES, 2 * HEAD_DIM), lambda b: (0, 0, 0)),
              pl.BlockSpec((HEAD_DIM, HEAD_DIM), lambda b: (0, 0))]
    ospec = pl.BlockSpec((1, N_KV_HEADS, n_sub, HEAD_DIM), lambda b: (b, 0, 0, 0))
    osds = jax.ShapeDtypeStruct((batch, N_KV_HEADS, n_sub, HEAD_DIM), BF16)
    return pl.pallas_call(
        functools.partial(_compress_body, n_sub=n_sub),
        grid=(batch,),
        in_specs=[xspec, xspec] + wspecs + wspecs,
        out_specs=[ospec, ospec],
        out_shape=[osds, osds],
        scratch_shapes=[pltpu.VMEM((n_sub * N_KV_HEADS, HEAD_DIM), F32)],
        compiler_params=_cparams("parallel"),
        name="compress_prompt",
    )(x3(rows_k), x3(rows_v), *cmp_k_w, *cmp_v_w)


def _compress_weights(pe, w1, w2):
    w1r = w1.reshape(2, N_PAIR, 2, HEAD_DIM, HEAD_DIM)
    wpair = jnp.transpose(w1r, (1, 2, 3, 0, 4)).reshape(N_PAIR, 2 * HEAD_DIM, 2 * HEAD_DIM).astype(BF16)
    per = pe.reshape(2, N_PAIR, 2 * HEAD_DIM)
    pepair = jnp.zeros((N_PAIR, SUBLANES, 2 * HEAD_DIM), F32).at[:, 0:2, :].set(jnp.transpose(per, (1, 0, 2)))
    return wpair, pepair, w2.astype(BF16)


def _topk_select(score, valid, n_blk):
    lane = lax.broadcasted_iota(jnp.int32, score.shape, 1)
    rank = jnp.zeros(score.shape, F32)
    for i in range(n_blk):
        ci = score[:, i:i + 1]
        beats = (ci > score) | ((ci == score) & (lane > i))
        rank = rank + jnp.where(beats, 1.0, 0.0)
    return (rank < SEL_TOPK) & valid


def _split_dot(p, w):
    hi = p.astype(BF16)
    lo = (p - hi.astype(F32)).astype(BF16)
    return _dot(hi, w) + _dot(lo, w)


def _attn_prompt_tile(q, qraw_ref, qrot_ref, kcc_ref, vcc_ref, ks_ref, vs_ref, kw_ref, vw_ref, gates_ref, ovt_ref,
                      ex_ref, o_ref, *, tq, seq):
    q0 = q * tq
    kt = tq
    qpos = q0 + lax.broadcasted_iota(jnp.int32, (tq, 1), 0)
    qpos4 = jnp.concatenate([qpos] * GQA_GROUP, axis=0)
    stack = lambda ref: jnp.concatenate([ref[:, g * HEAD_DIM:(g + 1) * HEAD_DIM] for g in range(GQA_GROUP)], axis=0)
    q_raw = stack(qraw_ref)
    q_rot = stack(qrot_ref)

    n_cmp = seq // CMP_STRIDE
    sc = _dot_nt(q_raw, kcc_ref[0, 0])
    c_end = lax.broadcasted_iota(jnp.int32, (1, n_cmp), 1) * CMP_STRIDE + (CMP_BLOCK - 1)
    cmask = c_end <= qpos4
    scm = jnp.where(cmask, sc, NEG_INF)
    e = jnp.where(cmask, jnp.exp2(scm - jnp.max(scm, axis=-1, keepdims=True)), 0.0)
    pc = e / jnp.maximum(jnp.sum(e, axis=-1, keepdims=True), TINY)
    o_cmp = _dot(pc.astype(BF16), vcc_ref[0, 0])

    ranked = q0 + tq > SEL_TOPK * SEL_BLOCK
    if ranked:
        n_blk = seq // SEL_BLOCK
        psum = pc[0:tq] + pc[tq:2 * tq] + pc[2 * tq:3 * tq] + pc[3 * tq:4 * tq]
        hi = psum.astype(BF16)
        lo = (psum - hi.astype(F32)).astype(BF16)
        imp = _dot_nt(ovt_ref[...], hi) + _dot_nt(ovt_ref[...], lo)
        blk = lax.broadcasted_iota(jnp.int32, (n_blk, 1), 0)
        qlane = q0 + lax.broadcasted_iota(jnp.int32, (1, tq), 1)
        cur = lax.shift_right_logical(qlane, SEL_BLOCK.bit_length() - 1)
        valid = blk * SEL_BLOCK <= qlane
        forced = (blk == 0) | (blk == cur) | (blk == cur - 1)
        score = jnp.where(valid, jnp.where(forced, FORCE_SCORE, imp), NEG_INF)
        rank = jnp.zeros((n_blk, tq), F32)
        for i in range((q0 + tq) // SEL_BLOCK):
            ci = score[i:i + 1, :]
            beats = (ci > score) | ((ci == score) & (blk > i))
            rank = rank + jnp.where(beats, 1.0, 0.0)
        sel_t = jnp.where((rank < SEL_TOPK) & valid, 1.0, 0.0)
        sel = jnp.concatenate([sel_t, jnp.zeros((LANES - n_blk, tq), F32)], axis=0).T.astype(BF16)

    def attend(k_ref, v_ref, chunks):
        ss = []
        for c, bias in chunks:
            s = _dot_nt(q_rot, k_ref[0, c * kt:(c + 1) * kt, :])
            ss.append(s if bias is None else s + jnp.concatenate([bias] * GQA_GROUP, axis=0))
        top = ss[0]
        for s in ss[1:]:
            top = jnp.maximum(top, s)
        m = jnp.max(top, axis=-1, keepdims=True)
        tot = None
        acc = None
        for (c, _), s in zip(chunks, ss):
            p = jnp.exp2(s - m)
            tot = p if tot is None else tot + p
            pv = _dot(p.astype(BF16), v_ref[0, c * kt:(c + 1) * kt, :])
            acc = pv if acc is None else acc + pv
        return acc / jnp.sum(tot, axis=-1, keepdims=True)

    def mask_bias(c, ok):
        kpos = c * kt + lax.broadcasted_iota(jnp.int32, (1, kt), 1)
        return jnp.where(ok(kpos), 0.0, NEG_INF)

    sel_chunks = []
    for c in range(q + 1):
        if ranked:
            chosen = _dot(sel, ex_ref[:, c * kt:(c + 1) * kt]) > 0.5
            sel_chunks.append((c, mask_bias(c, lambda kpos: chosen & (kpos <= qpos))))
        else:
            sel_chunks.append((c, mask_bias(c, lambda kpos: kpos <= qpos) if c == q else None))
    o_sel = attend(ks_ref, vs_ref, sel_chunks)

    win_chunks = []
    for c in range(q + 1):
        if c * kt + kt - 1 <= q0 - WINDOW:
            continue
        whole = c * kt + kt - 1 <= q0 and c * kt > q0 + tq - 1 - WINDOW
        win_chunks.append((c, None if whole else mask_bias(c, lambda kpos: (kpos <= qpos) & (kpos > qpos - WINDOW))))
    o_win = attend(kw_ref, vw_ref, win_chunks)

    gates = gates_ref[0]
    for g in range(GQA_GROUP):
        r = slice(g * tq, (g + 1) * tq)
        gate = lambda br: gates[:, br * GQA_GROUP + g:br * GQA_GROUP + g + 1]
        o = gate(0) * o_cmp[r] + gate(1) * o_sel[r] + gate(2) * o_win[r]
        o_ref[:, g * HEAD_DIM:(g + 1) * HEAD_DIM] = o.astype(BF16)


def _attn_prompt_body(*refs, tq, seq, with_win):
    if with_win:
        (*refs, sk_ref, sv_ref, nk_ref, nv_ref, o_ref, ok_ref, ov_ref) = refs
        _win_update_body(sk_ref, sv_ref, nk_ref, nv_ref, ok_ref, ov_ref)
        refs = (*refs, o_ref)
    qi = pl.program_id(2)
    for q in range(seq // tq):
        @pl.when(qi == q)
        def _(q=q):
            _attn_prompt_tile(q, *refs, tq=tq, seq=seq)


def _attn_prompt(q_raw, q_rot, kcc, vcc, ks_hm, vs_hm, kw_hm, vw_hm, gates, ovt, ex, batch, seq, win=None, tq=ATTN_TQ):
    nq = seq // tq
    n_cmp = seq // CMP_STRIDE
    qspec = pl.BlockSpec((tq, KV_DIM), lambda b, h, i: (b * nq + i, h))
    cspec = pl.BlockSpec((1, 1, n_cmp, HEAD_DIM), lambda b, h, i: (b, h, 0, 0))
    kvspec = pl.BlockSpec((1, seq, HEAD_DIM), lambda b, h, i: (h, b, 0))
    in_specs = [qspec, qspec, cspec, cspec, kvspec, kvspec, kvspec, kvspec,
                pl.BlockSpec((1, tq, LANES), lambda b, h, i: (h, b * nq + i, 0)),
                pl.BlockSpec(ovt.shape, lambda b, h, i: (0, 0)),
                pl.BlockSpec((LANES, seq), lambda b, h, i: (0, 0))]
    args = [q_raw, q_rot, kcc, vcc, ks_hm, vs_hm, kw_hm, vw_hm, gates, ovt, ex]
    out_specs = [qspec]
    out_shape = [jax.ShapeDtypeStruct((batch * seq, Q_DIM), BF16)]
    if win is not None:
        state_k = win[0]
        tb = state_k.shape[0] // (batch * N_KV_HEADS * nq)
        step = lambda b, h, i: ((b * N_KV_HEADS + h) * nq + i, 0, 0, 0)
        big = pl.BlockSpec((tb,) + state_k.shape[1:], step)
        one = pl.BlockSpec((tb, 1) + state_k.shape[2:], step)
        in_specs += [big, big, one, one]
        args += list(win)
        out_specs += [big, big]
        out_shape += [jax.ShapeDtypeStruct(state_k.shape, state_k.dtype)] * 2
    return pl.pallas_call(
        functools.partial(_attn_prompt_body, tq=tq, seq=seq, with_win=win is not None),
        grid=(batch, N_KV_HEADS, nq),
        in_specs=in_specs,
        out_specs=out_specs,
        out_shape=out_shape,
        compiler_params=_cparams("parallel", "parallel", "arbitrary"),
        name="attn_prompt",
    )(*args)


def _matmul_res_body(a_ref, w_ref, r_ref, o_ref):
    o_ref[...] = r_ref[...] + _dot(a_ref[...], w_ref[...])


def _matmul_residual(a, w, res, tm):
    m, k = a.shape
    n = w.shape[1]
    return pl.pallas_call(
        _matmul_res_body,
        grid=(m // tm,),
        in_specs=[pl.BlockSpec((tm, k), lambda i: (i, 0)), pl.BlockSpec((k, n), lambda i: (0, 0)),
                  pl.BlockSpec((tm, n), lambda i: (i, 0))],
        out_specs=pl.BlockSpec((tm, n), lambda i: (i, 0)),
        out_shape=jax.ShapeDtypeStruct((m, n), F32),
        compiler_params=_cparams("parallel"),
        name="out_proj",
    )(a, w, res)


def _attn_sample_body(pt_ref, *refs, n_pages, past):
    del pt_ref
    pages = refs[:4 * n_pages]
    (wink_ref, winv_ref, qraw_ref, qrot_ref, kse_ref, vse_ref, kwe_ref, vwe_ref, gates_ref,
     wpk_ref, pek_ref, w2k_ref, wpv_ref, pev_ref, w2v_ref, ov_ref, gs_ref, o_ref, s_ref) = refs[4 * n_pages:]
    ck, cv, sk, sv = (pages[i * n_pages:(i + 1) * n_pages] for i in range(4))
    n_sub = past // CMP_STRIDE
    n_rows = n_sub * N_KV_HEADS
    sub_pp = PAGE_SIZE // CMP_STRIDE
    page_rows = PAGE_SIZE * N_KV_HEADS

    def rows_of(pg):
        return lambda k: jnp.concatenate(
            [r[0, :, k * SUBLANES:(k + 1) * SUBLANES, :].reshape(sub_pp * SUBLANES, HEAD_DIM) for r in pg], axis=0)

    kcc, vcc = _compress([(rows_of(ck), wpk_ref, pek_ref, w2k_ref), (rows_of(cv), wpv_ref, pev_ref, w2v_ref)], n_sub)
    kcc, vcc = kcc.astype(BF16), vcc.astype(BF16)

    q_raw = qraw_ref[0]
    q_rot = qrot_ref[0]
    q_rot32 = q_rot.astype(F32)
    hrow = lax.shift_right_logical(lax.broadcasted_iota(jnp.int32, (N_HEADS, 1), 0), 2)

    def softmax_parts(s_old, s_new):
        m = jnp.maximum(jnp.max(s_old, axis=-1, keepdims=True), s_new)
        p_old = jnp.exp2(s_old - m)
        p_new = jnp.exp2(s_new - m)
        return p_old, p_new, jnp.sum(p_old, axis=-1, keepdims=True) + p_new

    col = lax.broadcasted_iota(jnp.int32, (1, n_rows), 1)
    cmask = ((col & (N_KV_HEADS - 1)) == hrow) & (
        lax.shift_right_logical(col, 2) * CMP_STRIDE + (CMP_BLOCK - 1) <= past)
    scm = jnp.where(cmask, _dot_nt(q_raw, kcc), NEG_INF)
    e = jnp.where(cmask, jnp.exp2(scm - jnp.max(scm, axis=-1, keepdims=True)), 0.0)
    pc = e / jnp.maximum(jnp.sum(e, axis=-1, keepdims=True), TINY)
    o_cmp = _dot(pc.astype(BF16), vcc)

    imp16 = _split_dot(pc, ov_ref[...])
    hi = imp16.astype(BF16)
    lo = (imp16 - hi.astype(F32)).astype(BF16)
    imp = _dot(gs_ref[...], hi) + _dot(gs_ref[...], lo)
    n_blk = past // SEL_BLOCK + 1
    cur = past // SEL_BLOCK
    blk = lax.broadcasted_iota(jnp.int32, (N_HEADS, LANES), 1)
    valid = blk < n_blk
    forced = (blk == 0) | (blk == cur) | (blk == cur - 1)
    score = jnp.where(valid, jnp.where(forced, FORCE_SCORE, imp), NEG_INF)
    sel = jnp.where(_topk_select(score, valid, n_blk), 1.0, 0.0)

    pcol = lax.broadcasted_iota(jnp.int32, (1, page_rows), 1)
    head_ok = (pcol & (N_KV_HEADS - 1)) == hrow
    upper = pcol >= page_rows // 2
    for p in range(n_pages):
        s = _dot_nt(q_rot, sk[p][0].reshape(page_rows, HEAD_DIM).astype(BF16))
        chosen = jnp.where(upper, sel[:, 2 * p + 1:2 * p + 2], sel[:, 2 * p:2 * p + 1])
        s_ref[:, p * page_rows:(p + 1) * page_rows] = jnp.where(head_ok & (chosen > 0.5), s, NEG_INF)
    s_new = jnp.sum(q_rot32 * kse_ref[0], axis=-1, keepdims=True)
    p_old, p_new, denom = softmax_parts(s_ref[...], s_new)
    acc = p_new * vse_ref[0]
    for p in range(n_pages):
        acc = acc + _dot(p_old[:, p * page_rows:(p + 1) * page_rows].astype(BF16),
                         sv[p][0].reshape(page_rows, HEAD_DIM).astype(BF16))
    o_sel = acc / denom

    n_win = wink_ref.shape[1]
    wcol = lax.broadcasted_iota(jnp.int32, (1, n_win), 1)
    wmask = ((wcol & (N_KV_HEADS - 1)) == hrow) & (lax.shift_right_logical(wcol, 2) >= n_win // N_KV_HEADS + 1 - WINDOW)
    sw = jnp.where(wmask, _dot_nt(q_rot, wink_ref[0].astype(BF16)), NEG_INF)
    sw_new = jnp.sum(q_rot32 * kwe_ref[0], axis=-1, keepdims=True)
    p_old, p_new, denom = softmax_parts(sw, sw_new)
    o_win = (p_new * vwe_ref[0] + _dot(p_old.astype(BF16), winv_ref[0].astype(BF16))) / denom

    gates = gates_ref[0]
    o = gates[:, 0:1] * o_cmp + gates[:, 1:2] * o_sel + gates[:, 2:3] * o_win
    o_ref[0] = o.astype(BF16)


def _attn_sample(page_table, caches, win_k, win_v, q_raw, q_rot, new_rows, gates, cmp_k_w, cmp_v_w, ov, gs):
    batch, n_pages = page_table.shape
    past = n_pages * PAGE_SIZE
    chunk = CMP_STRIDE * N_KV_HEADS
    page_specs = [pl.BlockSpec((1, PAGE_SIZE // CMP_STRIDE, chunk, HEAD_DIM), lambda b, pt, p=p: (pt[b, p], 0, 0, 0))
                  for p in range(n_pages)]
    per_b = lambda a: pl.BlockSpec((1,) + a.shape[1:], lambda b, pt: (b,) + (0,) * (a.ndim - 1))
    whole = lambda a: pl.BlockSpec(a.shape, lambda b, pt: (0,) * a.ndim)
    ins, specs = [], []
    for c in caches:
        ins += [c] * n_pages
        specs += page_specs
    for a in (win_k, win_v, q_raw, q_rot, *new_rows, gates):
        ins.append(a)
        specs.append(per_b(a))
    for a in (*cmp_k_w, *cmp_v_w, ov, gs):
        ins.append(a)
        specs.append(whole(a))
    return pl.pallas_call(
        functools.partial(_attn_sample_body, n_pages=n_pages, past=past),
        grid_spec=pltpu.PrefetchScalarGridSpec(
            num_scalar_prefetch=1, grid=(batch,), in_specs=specs,
            out_specs=pl.BlockSpec((1, N_HEADS, HEAD_DIM), lambda b, pt: (b, 0, 0)),
            scratch_shapes=[pltpu.VMEM((N_HEADS, past * N_KV_HEADS), F32)]),
        out_shape=jax.ShapeDtypeStruct((batch, N_HEADS, HEAD_DIM), BF16),
        compiler_params=_cparams("arbitrary"),
        name="attn_sample",
    )(page_table, *ins)


def _win_update_body(sk_ref, sv_ref, nk_ref, nv_ref, ok_ref, ov_ref):
    width = sk_ref.shape[1]
    for s_ref, n_ref, o_ref in ((sk_ref, nk_ref, ok_ref), (sv_ref, nv_ref, ov_ref)):
        o_ref[:, 0:width - 1] = s_ref[:, 1:width]
        o_ref[:, width - 1:width] = n_ref[...]


def _win_update(state_k, state_v, new_k, new_v, tb=4):
    batch, width = state_k.shape[:2]
    big = pl.BlockSpec((tb, width, N_KV_HEADS, HEAD_DIM), lambda i: (i, 0, 0, 0))
    one = pl.BlockSpec((tb, 1, N_KV_HEADS, HEAD_DIM), lambda i: (i, 0, 0, 0))
    sds = jax.ShapeDtypeStruct(state_k.shape, state_k.dtype)
    return pl.pallas_call(
        _win_update_body,
        grid=(batch // tb,),
        in_specs=[big, big, one, one],
        out_specs=[big, big],
        out_shape=[sds, sds],
        compiler_params=_cparams("parallel"),
        name="win_update",
    )(state_k, state_v, new_k, new_v)


def _rope_tables(pos):
    half = HEAD_DIM // 2
    inv_freq = jnp.power(ROPE_THETA, -jnp.arange(half, dtype=F32) / half)
    ang = pos.astype(F32)[:, None] * inv_freq[None, :]
    cos, sin = jnp.cos(ang), jnp.sin(ang)
    return jnp.concatenate([cos, cos], axis=1), jnp.concatenate([-sin, sin], axis=1)


def _overlap(n_sub, n_cols):
    n = jnp.arange(n_sub)[:, None]
    j = jnp.arange(LANES)[None, :]
    c_start, s_start = n * CMP_STRIDE, j * SEL_BLOCK
    hit = (c_start <= s_start + SEL_BLOCK - 1) & (c_start + CMP_BLOCK - 1 >= s_start)
    return (hit & (n < n_sub - 1) & (j < n_cols)).astype(BF16)


def kernel(x_prompt, x_sample, state_pool, cache_cmp_k, cache_cmp_v, cache_sel_k, cache_sel_v, state_win_k, state_win_v, page_table, norm_ffn1, norm_mix, norm_ffn2, norm_final, w_ffn_gate, w_ffn_up, w_ffn_down, w_pool, pool_scale, w_nsa_in, w_nsa_out, cmp_pe_k, cmp_w1_k, cmp_w2_k, cmp_pe_v, cmp_w1_v, cmp_w2_v):
    batch, seq, _ = x_prompt.shape
    dec_batch, dec_seq, _ = x_sample.shape
    past = page_table.shape[1] * PAGE_SIZE
    win_buf = state_win_k.shape[2]
    assert dec_seq == 1 and past % SEL_BLOCK == 0 and win_buf == WINDOW and seq % ATTN_TQ == 0
    assert norm_ffn1.shape[0] == 2 and w_pool.shape[0] == 1 and w_nsa_in.shape[0] == 1

    vec = lambda v: v.reshape(1, D_MODEL)
    wpool = w_pool[0].astype(BF16)
    n_main = Q_DIM + 6 * KV_DIM
    w_main = w_nsa_in[0][:, :n_main].astype(BF16)
    wgt = w_nsa_in[0][:, n_main:].reshape(D_MODEL, N_KV_HEADS, GQA_GROUP, N_BRANCHES)
    wgt = jnp.transpose(wgt, (0, 1, 3, 2)).reshape(D_MODEL, N_HEADS * N_BRANCHES)
    wgt = jnp.pad(wgt, ((0, 0), (0, LANES - N_HEADS * N_BRANCHES))).astype(BF16)
    w_out = w_nsa_out[0].astype(BF16)
    cmp_k_w = _compress_weights(cmp_pe_k[0], cmp_w1_k[0], cmp_w2_k[0])
    cmp_v_w = _compress_weights(cmp_pe_v[0], cmp_w1_v[0], cmp_w2_v[0])
    ffn = lambda xp, xs, g, layer, which, gf=None: _ffn(xp, xs, vec(g), w_ffn_gate, w_ffn_up, w_ffn_down, layer, which,
                                                        g_final=gf)

    xp = x_prompt.reshape(batch * seq, D_MODEL)
    xs = x_sample.reshape(dec_batch, D_MODEL)
    xp, xs = ffn(xp, xs, norm_ffn1[0], 0, 0)

    xp, pool_p = _pool_prompt(xp, vec(norm_mix[0]), wpool, vec(pool_scale[0]), batch, seq)
    xs, h_s = _pool_sample(xs, jnp.transpose(state_pool[0], (1, 0, 2)), vec(norm_mix[0]), wpool, vec(pool_scale[0]))
    pool_s = jnp.concatenate([state_pool[0][:, 1:], h_s[:, None]], axis=1)
    xp, xs = ffn(xp, xs, norm_ffn2[0], 0, 1)
    xp, xs = ffn(xp, xs, norm_ffn1[1], 1, 0)

    tm = 256
    cos_p, sin_p = _rope_tables(jnp.arange(seq))
    (q_raw, q_rot, kc_p, vc_p, ks_p, vs_p, kw_p, vw_p, ks_hm, vs_hm, kw_hm, vw_hm, gates_p) = _nsa_project(
        xp, vec(norm_mix[1]), w_main, wgt, cos_p, sin_p, seq // tm, tm)
    cos_s, sin_s = _rope_tables(jnp.full((dec_batch,), past))
    (q_raw_s, q_rot_s, kc_s, vc_s, ks_s, vs_s, kw_s, vw_s, _, _, _, _, gates_s) = _nsa_project(
        xs, vec(norm_mix[1]), w_main, wgt, cos_s, sin_s, 1, dec_batch)

    kcc, vcc = _compress_prompt(kc_p, vc_p, cmp_k_w, cmp_v_w, batch, seq)
    n_sub = seq // CMP_STRIDE
    expand = (jnp.arange(LANES)[:, None] == jnp.arange(seq)[None, :] // SEL_BLOCK).astype(BF16)
    ovt = jnp.transpose(_overlap(n_sub, seq // SEL_BLOCK))[:seq // SEL_BLOCK]
    new_row = lambda a: a.reshape(dec_batch, 1, N_KV_HEADS, HEAD_DIM)
    win = (state_win_k[0], state_win_v[0], new_row(kw_s), new_row(vw_s))
    attn_steps = batch * N_KV_HEADS * (seq // ATTN_TQ)
    if dec_batch % attn_steps == 0:
        o_p, win_k_s, win_v_s = _attn_prompt(q_raw, q_rot, kcc, vcc, ks_hm, vs_hm, kw_hm, vw_hm, gates_p, ovt, expand,
                                             batch, seq, win=win)
    else:
        (o_p,) = _attn_prompt(q_raw, q_rot, kcc, vcc, ks_hm, vs_hm, kw_hm, vw_hm, gates_p, ovt, expand, batch, seq)
        win_k_s, win_v_s = _win_update(*win)
    xp = _matmul_residual(o_p, w_out, xp, 2 * tm)

    per_head = lambda a: jnp.repeat(a.reshape(dec_batch, N_KV_HEADS, HEAD_DIM), GQA_GROUP, axis=1)
    g3 = gates_s[:, :, :N_BRANCHES * GQA_GROUP].reshape(N_KV_HEADS, dec_batch, N_BRANCHES, GQA_GROUP)
    g3 = jnp.transpose(g3, (1, 0, 3, 2)).reshape(dec_batch, N_HEADS, N_BRANCHES)
    chunk = CMP_STRIDE * N_KV_HEADS
    paged = lambda c: c[0].reshape(c.shape[1], PAGE_SIZE // CMP_STRIDE, chunk, HEAD_DIM)
    win_rows = lambda s: s[0].reshape(dec_batch, win_buf * N_KV_HEADS, HEAD_DIM)
    gs = (jnp.arange(N_HEADS)[:, None] // GQA_GROUP == jnp.arange(N_HEADS)[None, :] // GQA_GROUP).astype(BF16)
    n_sub_s = past // CMP_STRIDE
    ov_s = jnp.repeat(_overlap(n_sub_s, past // SEL_BLOCK + 1), N_KV_HEADS, axis=0)
    o_s = _attn_sample(page_table, [paged(c) for c in (cache_cmp_k, cache_cmp_v, cache_sel_k, cache_sel_v)],
                       win_rows(state_win_k), win_rows(state_win_v),
                       q_raw_s.reshape(dec_batch, N_HEADS, HEAD_DIM), q_rot_s.reshape(dec_batch, N_HEADS, HEAD_DIM),
                       [per_head(a) for a in (ks_s, vs_s, kw_s, vw_s)], g3, cmp_k_w, cmp_v_w, ov_s, gs)
    xs = _matmul_residual(o_s.reshape(dec_batch, Q_DIM), w_out, xs, dec_batch)

    yp, ys = ffn(xp, xs, norm_ffn2[1], 1, 1, vec(norm_final))
    y_prompt = yp.reshape(batch, seq, D_MODEL)
    y_sample = ys.reshape(dec_batch, 1, D_MODEL)

    kv5 = lambda a, b, s: a.reshape(1, b, s, N_KV_HEADS, HEAD_DIM)
    keep = min(WINDOW, seq)
    return (y_prompt, y_sample, pool_p[None], pool_s[None],
            kv5(kc_p, batch, seq), kv5(kc_s, dec_batch, 1), kv5(vc_p, batch, seq), kv5(vc_s, dec_batch, 1),
            kv5(ks_p, batch, seq), kv5(ks_s, dec_batch, 1), kv5(vs_p, batch, seq), kv5(vs_s, dec_batch, 1),
            kv5(kw_p, batch, seq)[:, :, seq - keep:], win_k_s[None],
            kv5(vw_p, batch, seq)[:, :, seq - keep:], win_v_s[None])
```

```python
import functools

import jax
import jax.numpy as jnp
from jax import lax
from jax.experimental import pallas as pl
from jax.experimental.pallas import tpu as pltpu

F32 = jnp.float32
BF16 = jnp.bfloat16

D_MODEL = 2048
D_FF = 5632
POOL_WINDOWS = (2, 4, 8, 16)
POOL_GROUP_DIM = D_MODEL // len(POOL_WINDOWS)
POOL_STATE_LEN = max(POOL_WINDOWS) - 1
N_HEADS = 16
HEAD_DIM = 128
N_KV_HEADS = 4
GQA_GROUP = N_HEADS // N_KV_HEADS
Q_DIM = N_HEADS * HEAD_DIM
KV_DIM = N_KV_HEADS * HEAD_DIM
N_BRANCHES = 3
CMP_BLOCK = 32
CMP_STRIDE = 16
SEL_BLOCK = 64
SEL_TOPK = 16
ATTN_TQ = 256
WINDOW = 512
PAGE_SIZE = 128
SCALE = HEAD_DIM ** -0.5
LOG2E = 1.4426950408889634
ROPE_THETA = 10000.0
RMS_EPS = 1e-6
NEG_INF = -1e30
FORCE_SCORE = 1e9
TINY = 1e-30

LANES = 128
SUBLANES = 8
VMEM_LIMIT = 56 * 1024 * 1024


def _cparams(*sem):
    return pltpu.CompilerParams(dimension_semantics=sem, vmem_limit_bytes=VMEM_LIMIT)


def _rms(x, g):
    y = x * lax.rsqrt(jnp.mean(x * x, axis=-1, keepdims=True) + RMS_EPS)
    return y * g


def _dot(a, b):
    return jnp.dot(a, b, preferred_element_type=F32)


def _dot_nt(a, b):
    return lax.dot_general(a, b, (((1,), (1,)), ((), ())), preferred_element_type=F32)


def _ffn_body(xp_ref, xs_ref, g_ref, wg_ref, wu_ref, wd_ref, *rest, final_norm, tm, ts):
    if final_norm:
        gf_ref, op_ref, os_ref, xn_ref = rest
    else:
        op_ref, os_ref, xn_ref = rest
    i, f = pl.program_id(0), pl.program_id(1)
    last = pl.num_programs(1) - 1

    @pl.when(f == 0)
    def _():
        x = xp_ref[...]
        xn_ref[0:tm] = _rms(x, g_ref[...]).astype(BF16)
        op_ref[...] = x

    @pl.when((f == 0) & (i == 0))
    def _():
        x = xs_ref[...]
        xn_ref[tm:tm + ts] = _rms(x, g_ref[...]).astype(BF16)
        os_ref[...] = x

    def half_step(rows):
        xn = xn_ref[0:rows]
        gate = _dot(xn, wg_ref[...].astype(BF16))
        up = _dot(xn, wu_ref[...].astype(BF16))
        h = (gate * jax.nn.sigmoid(gate)) * up
        return _dot((0.5 * h).astype(BF16), wd_ref[...].astype(BF16))

    @pl.when(i == 0)
    def _():
        y = half_step(tm + ts)
        op_ref[...] += y[0:tm]
        os_ref[...] += y[tm:tm + ts]

    @pl.when(i > 0)
    def _():
        op_ref[...] += half_step(tm)

    if final_norm:
        @pl.when(f == last)
        def _():
            op_ref[...] = _rms(op_ref[...], gf_ref[...])

        @pl.when((f == last) & (i == 0))
        def _():
            os_ref[...] = _rms(os_ref[...], gf_ref[...])


def _ffn(xp, xs, g, wg, wu, wd, layer, which, g_final=None, tm=1024, tf=256):
    m, ts = xp.shape[0], xs.shape[0]
    final_norm = g_final is not None
    row = lambda i, f: (i, 0)
    const = lambda i, f: (0, 0)
    in_specs = [
        pl.BlockSpec((tm, D_MODEL), row),
        pl.BlockSpec((ts, D_MODEL), const),
        pl.BlockSpec((1, D_MODEL), const),
        pl.BlockSpec((None, None, D_MODEL, tf), lambda i, f: (layer, which, 0, f)),
        pl.BlockSpec((None, None, D_MODEL, tf), lambda i, f: (layer, which, 0, f)),
        pl.BlockSpec((None, None, tf, D_MODEL), lambda i, f: (layer, which, f, 0)),
    ]
    args = [xp, xs, g, wg, wu, wd]
    if final_norm:
        in_specs.append(pl.BlockSpec((1, D_MODEL), const))
        args.append(g_final)
    return pl.pallas_call(
        functools.partial(_ffn_body, final_norm=final_norm, tm=tm, ts=ts),
        grid=(m // tm, D_FF // tf),
        in_specs=in_specs,
        out_specs=[pl.BlockSpec((tm, D_MODEL), row), pl.BlockSpec((ts, D_MODEL), const)],
        out_shape=[jax.ShapeDtypeStruct((m, D_MODEL), F32), jax.ShapeDtypeStruct((ts, D_MODEL), F32)],
        scratch_shapes=[pltpu.VMEM((tm + ts, D_MODEL), BF16)],
        compiler_params=_cparams("arbitrary", "arbitrary"),
        name="ffn",
    )(*args)


def _pool_groups(h_cols, x, w_ref, sc_ref, o_ref, window_sum, inv_cnt):
    for gi, w in enumerate(POOL_WINDOWS):
        c0 = gi * POOL_GROUP_DIM
        cols = slice(c0, c0 + POOL_GROUP_DIM)
        hc = h_cols(cols)
        d = window_sum(w, cols, hc) * inv_cnt(w) - hc
        y = _dot(d.astype(BF16), w_ref[gi])
        o_ref[:, cols] = x[:, cols] + y * sc_ref[:, cols]


def _pool_prompt_body(x_ref, g_ref, w_ref, sc_ref, o_ref, st_ref, hbuf, *, ts):
    s = pl.program_id(1)
    halo = 2 * SUBLANES

    @pl.when(s == 0)
    def _():
        hbuf[0:halo, :] = jnp.zeros((halo, D_MODEL), F32)

    @pl.when(s > 0)
    def _():
        hbuf[0:halo, :] = hbuf[ts:ts + halo, :]

    x = x_ref[...]
    hbuf[halo:halo + ts, :] = _rms(x, g_ref[...])
    pos = s * ts + lax.broadcasted_iota(jnp.int32, (ts, 1), 0)

    def window_sum(w, cols, hc):
        acc = hc
        for k in range(1, w):
            acc = acc + hbuf[halo - k:halo - k + ts, cols]
        return acc

    _pool_groups(lambda cols: hbuf[halo:halo + ts, cols], x, w_ref, sc_ref, o_ref, window_sum,
                 lambda w: 1.0 / jnp.minimum(w, pos + 1).astype(F32))

    @pl.when(s == pl.num_programs(1) - 1)
    def _():
        st_ref[0] = hbuf[halo + ts - POOL_STATE_LEN:halo + ts, :]


def _pool_prompt(x, g, w_pool, scale, batch, seq, ts=512):
    nt = seq // ts
    return pl.pallas_call(
        functools.partial(_pool_prompt_body, ts=ts),
        grid=(batch, nt),
        in_specs=[
            pl.BlockSpec((ts, D_MODEL), lambda b, s: (b * nt + s, 0)),
            pl.BlockSpec((1, D_MODEL), lambda b, s: (0, 0)),
            pl.BlockSpec((len(POOL_WINDOWS), POOL_GROUP_DIM, POOL_GROUP_DIM), lambda b, s: (0, 0, 0)),
            pl.BlockSpec((1, D_MODEL), lambda b, s: (0, 0)),
        ],
        out_specs=[
            pl.BlockSpec((ts, D_MODEL), lambda b, s: (b * nt + s, 0)),
            pl.BlockSpec((1, POOL_STATE_LEN, D_MODEL), lambda b, s: (b, 0, 0)),
        ],
        out_shape=[
            jax.ShapeDtypeStruct((batch * seq, D_MODEL), F32),
            jax.ShapeDtypeStruct((batch, POOL_STATE_LEN, D_MODEL), F32),
        ],
        scratch_shapes=[pltpu.VMEM((ts + 2 * SUBLANES, D_MODEL), F32)],
        compiler_params=_cparams("arbitrary", "arbitrary"),
        name="pool_prompt",
    )(x, g, w_pool, scale)


def _pool_sample_body(x_ref, st_ref, g_ref, w_ref, sc_ref, o_ref, h_ref):
    x = x_ref[...]
    h_ref[...] = _rms(x, g_ref[...])

    def window_sum(w, cols, hc):
        acc = hc
        for k in range(1, w):
            acc = acc + st_ref[POOL_STATE_LEN - k, :, cols]
        return acc

    _pool_groups(lambda cols: h_ref[:, cols], x, w_ref, sc_ref, o_ref, window_sum, lambda w: 1.0 / w)


def _pool_sample(x, state_t, g, w_pool, scale, tb=32):
    b = x.shape[0]
    return pl.pallas_call(
        _pool_sample_body,
        grid=(b // tb,),
        in_specs=[
            pl.BlockSpec((tb, D_MODEL), lambda i: (i, 0)),
            pl.BlockSpec((POOL_STATE_LEN, tb, D_MODEL), lambda i: (0, i, 0)),
            pl.BlockSpec((1, D_MODEL), lambda i: (0, 0)),
            pl.BlockSpec((len(POOL_WINDOWS), POOL_GROUP_DIM, POOL_GROUP_DIM), lambda i: (0, 0, 0)),
            pl.BlockSpec((1, D_MODEL), lambda i: (0, 0)),
        ],
        out_specs=[pl.BlockSpec((tb, D_MODEL), lambda i: (i, 0)), pl.BlockSpec((tb, D_MODEL), lambda i: (i, 0))],
        out_shape=[jax.ShapeDtypeStruct((b, D_MODEL), F32), jax.ShapeDtypeStruct((b, D_MODEL), F32)],
        compiler_params=_cparams("parallel"),
        name="pool_sample",
    )(x, state_t, g, w_pool, scale)


def _proj_body(x_ref, g_ref, w_ref, wg_ref, cos_ref, sin_ref,
               qraw_ref, qrot_ref, kc_ref, vc_ref, ks_ref, vs_ref, kw_ref, vw_ref,
               kshm_ref, vshm_ref, kwhm_ref, vwhm_ref, gates_ref, *, tm):
    xn = _rms(x_ref[...], g_ref[...]).astype(BF16)
    gall = jax.nn.sigmoid(_dot(xn, wg_ref[...]))
    per_head = N_BRANCHES * GQA_GROUP
    for kvh in range(N_KV_HEADS):
        gates_ref[kvh] = gall if kvh == 0 else pltpu.roll(gall, LANES - kvh * per_head, 1)

    cos, sin = cos_ref[...], sin_ref[...]
    rope = lambda s: s * cos + pltpu.roll(s, HEAD_DIM // 2, 1) * sin
    tile = lambda j: _dot(xn, w_ref[:, j * KV_DIM:(j + 1) * KV_DIM])
    heads = lambda a: [a[:, h * HEAD_DIM:(h + 1) * HEAD_DIM] for h in range(N_KV_HEADS)]

    nq = Q_DIM // KV_DIM
    for j in range(nq):
        zs = tile(j) * (SCALE * LOG2E)
        qraw_ref[:, j * KV_DIM:(j + 1) * KV_DIM] = zs.astype(BF16)
        for h, s in enumerate(heads(zs)):
            c0 = j * KV_DIM + h * HEAD_DIM
            qrot_ref[:, c0:c0 + HEAD_DIM] = rope(s).astype(BF16)

    kinds = [(kc_ref, None, False), (vc_ref, None, False), (ks_ref, kshm_ref, True), (vs_ref, vshm_ref, False),
             (kw_ref, kwhm_ref, True), (vw_ref, vwhm_ref, False)]
    for off, (ref, hm_ref, rot) in enumerate(kinds):
        for h, s in enumerate(heads(tile(nq + off))):
            s = rope(s) if rot else s
            ref[pl.ds(h, tm, stride=N_KV_HEADS), :] = s
            if hm_ref is not None:
                hm_ref[h] = s.astype(BF16)


def _nsa_project(x, g, w_main, w_gates, cos, sin, rope_period_tiles, tm):
    m = x.shape[0]
    row = lambda i: (i, 0)
    const = lambda i: (0, 0)
    kv_sds = jax.ShapeDtypeStruct((m * N_KV_HEADS, HEAD_DIM), F32)
    hm_sds = jax.ShapeDtypeStruct((N_KV_HEADS, m, HEAD_DIM), BF16)
    q_sds = jax.ShapeDtypeStruct((m, Q_DIM), BF16)
    kv_spec = pl.BlockSpec((tm * N_KV_HEADS, HEAD_DIM), row)
    hm_spec = pl.BlockSpec((N_KV_HEADS, tm, HEAD_DIM), lambda i: (0, i, 0))
    q_spec = pl.BlockSpec((tm, Q_DIM), row)
    return pl.pallas_call(
        functools.partial(_proj_body, tm=tm),
        grid=(m // tm,),
        in_specs=[
            pl.BlockSpec((tm, D_MODEL), row),
            pl.BlockSpec((1, D_MODEL), const),
            pl.BlockSpec(w_main.shape, const, pipeline_mode=pl.Buffered(1)),
            pl.BlockSpec(w_gates.shape, const, pipeline_mode=pl.Buffered(1)),
            pl.BlockSpec((tm, HEAD_DIM), lambda i: (i % rope_period_tiles, 0)),
            pl.BlockSpec((tm, HEAD_DIM), lambda i: (i % rope_period_tiles, 0)),
        ],
        out_specs=[q_spec, q_spec] + [kv_spec] * 6 + [hm_spec] * 4
        + [pl.BlockSpec((N_KV_HEADS, tm, LANES), lambda i: (0, i, 0))],
        out_shape=[q_sds, q_sds] + [kv_sds] * 6 + [hm_sds] * 4
        + [jax.ShapeDtypeStruct((N_KV_HEADS, m, LANES), F32)],
        compiler_params=_cparams("parallel"),
        name="nsa_project",
    )(x, g, w_main, w_gates, cos, sin)


N_PAIR = CMP_STRIDE // 2


def _compress(streams, n_sub):
    n_rows = n_sub * N_KV_HEADS
    low = lax.broadcasted_iota(jnp.int32, (1, SUBLANES, 1), 1) < N_KV_HEADS

    def repack(v):
        pairs = lambda a: a.reshape(n_sub // 2, 2, SUBLANES, HEAD_DIM)
        v4, r4 = pairs(v), pairs(pltpu.roll(v, N_KV_HEADS, 0))
        ev = jnp.where(low, v4[:, 0], r4[:, 1]).reshape(n_rows, HEAD_DIM)
        od = jnp.where(low, r4[:, 1], v4[:, 1]).reshape(n_rows, HEAD_DIM)
        return jnp.concatenate([ev, od], axis=1).astype(BF16)

    accs = [jnp.zeros((n_rows, 2 * HEAD_DIM), F32) for _ in streams]
    biases = [jnp.zeros((SUBLANES, 2 * HEAD_DIM), F32) for _ in streams]
    for k in range(N_PAIR):
        for i, (get_rows, wpair_ref, pepair_ref, _) in enumerate(streams):
            accs[i] = accs[i] + _dot(repack(get_rows(k)), wpair_ref[k])
            biases[i] = biases[i] + _dot(pepair_ref[k].astype(BF16), wpair_ref[k])
    outs = []
    for acc, bias, (_, _, _, w2_ref) in zip(accs, biases, streams):
        first = acc[:, :HEAD_DIM] + bias[0:1, :HEAD_DIM]
        second = acc[:, HEAD_DIM:] + bias[1:2, HEAD_DIM:]
        pre = first + pltpu.roll(second, n_rows - N_KV_HEADS, 0)
        outs.append(_dot(jax.nn.gelu(pre, approximate=True).astype(BF16), w2_ref[...]))
    return outs


def _compress_body(xk_ref, xv_ref, wpk_ref, pek_ref, w2k_ref, wpv_ref, pev_ref, w2v_ref, ok_ref, ov_ref, res_ref, *,
                   n_sub):
    rows = lambda x_ref: lambda k: x_ref[:, k * SUBLANES:(k + 1) * SUBLANES, :].reshape(n_sub * SUBLANES, HEAD_DIM)
    outs = _compress([(rows(xk_ref), wpk_ref, pek_ref, w2k_ref), (rows(xv_ref), wpv_ref, pev_ref, w2v_ref)], n_sub)
    for res, o_ref in zip(outs, (ok_ref, ov_ref)):
        res_ref[...] = res
        for h in range(N_KV_HEADS):
            o_ref[0, h] = res_ref[pl.ds(h, n_sub, stride=N_KV_HEADS), :].astype(BF16)


def _compress_prompt(rows_k, rows_v, cmp_k_w, cmp_v_w, batch, seq):
    n_sub = seq // CMP_STRIDE
    chunk = CMP_STRIDE * N_KV_HEADS
    x3 = lambda rows: rows.reshape(batch * n_sub, chunk, HEAD_DIM)
    xspec = pl.BlockSpec((n_sub, chunk, HEAD_DIM), lambda b: (b, 0, 0))
    wspecs = [pl.BlockSpec((N_PAIR, 2 * HEAD_DIM, 2 * HEAD_DIM), lambda b: (0, 0, 0)),
              pl.BlockSpec((N_PAIR, SUBLANES, 2 * HEAD_DIM), lambda b: (0, 0, 0)),
              pl.BlockSpec((HEAD_DIM, HEAD_DIM), lambda b: (0, 0))]
    ospec = pl.BlockSpec((1, N_KV_HEADS, n_sub, HEAD_DIM), lambda b: (b, 0, 0, 0))
    osds = jax.ShapeDtypeStruct((batch, N_KV_HEADS, n_sub, HEAD_DIM), BF16)
    return pl.pallas_call(
        functools.partial(_compress_body, n_sub=n_sub),
        grid=(batch,),
        in_specs=[xspec, xspec] + wspecs + wspecs,
        out_specs=[ospec, ospec],
        out_shape=[osds, osds],
        scratch_shapes=[pltpu.VMEM((n_sub * N_KV_HEADS, HEAD_DIM), F32)],
        compiler_params=_cparams("parallel"),
        name="compress_prompt",
    )(x3(rows_k), x3(rows_v), *cmp_k_w, *cmp_v_w)


def _compress_weights(pe, w1, w2):
    w1r = w1.reshape(2, N_PAIR, 2, HEAD_DIM, HEAD_DIM)
    wpair = jnp.transpose(w1r, (1, 2, 3, 0, 4)).reshape(N_PAIR, 2 * HEAD_DIM, 2 * HEAD_DIM).astype(BF16)
    per = pe.reshape(2, N_PAIR, 2 * HEAD_DIM)
    pepair = jnp.zeros((N_PAIR, SUBLANES, 2 * HEAD_DIM), F32).at[:, 0:2, :].set(jnp.transpose(per, (1, 0, 2)))
    return wpair, pepair, w2.astype(BF16)


def _topk_select(score, valid, n_blk):
    lane = lax.broadcasted_iota(jnp.int32, score.shape, 1)
    rank = jnp.zeros(score.shape, F32)
    for i in range(n_blk):
        ci = score[:, i:i + 1]
        beats = (ci > score) | ((ci == score) & (lane > i))
        rank = rank + jnp.where(beats, 1.0, 0.0)
    return (rank < SEL_TOPK) & valid


def _split_dot(p, w):
    hi = p.astype(BF16)
    lo = (p - hi.astype(F32)).astype(BF16)
    return _dot(hi, w) + _dot(lo, w)


def _attn_prompt_tile(q, qraw_ref, qrot_ref, kcc_ref, vcc_ref, ks_ref, vs_ref, kw_ref, vw_ref, gates_ref, ovt_ref,
                      ex_ref, o_ref, *, tq, seq):
    q0 = q * tq
    kt = tq
    qpos = q0 + lax.broadcasted_iota(jnp.int32, (tq, 1), 0)
    qpos4 = jnp.concatenate([qpos] * GQA_GROUP, axis=0)
    stack = lambda ref: jnp.concatenate([ref[:, g * HEAD_DIM:(g + 1) * HEAD_DIM] for g in range(GQA_GROUP)], axis=0)
    q_raw = stack(qraw_ref)
    q_rot = stack(qrot_ref)

    n_cmp = seq // CMP_STRIDE
    sc = _dot_nt(q_raw, kcc_ref[0, 0])
    c_end = lax.broadcasted_iota(jnp.int32, (1, n_cmp), 1) * CMP_STRIDE + (CMP_BLOCK - 1)
    cmask = c_end <= qpos4
    scm = jnp.where(cmask, sc, NEG_INF)
    e = jnp.where(cmask, jnp.exp2(scm - jnp.max(scm, axis=-1, keepdims=True)), 0.0)
    pc = e / jnp.maximum(jnp.sum(e, axis=-1, keepdims=True), TINY)
    o_cmp = _dot(pc.astype(BF16), vcc_ref[0, 0])

    ranked = q0 + tq > SEL_TOPK * SEL_BLOCK
    if ranked:
        n_blk = seq // SEL_BLOCK
        psum = pc[0:tq] + pc[tq:2 * tq] + pc[2 * tq:3 * tq] + pc[3 * tq:4 * tq]
        hi = psum.astype(BF16)
        lo = (psum - hi.astype(F32)).astype(BF16)
        imp = _dot_nt(ovt_ref[...], hi) + _dot_nt(ovt_ref[...], lo)
        blk = lax.broadcasted_iota(jnp.int32, (n_blk, 1), 0)
        qlane = q0 + lax.broadcasted_iota(jnp.int32, (1, tq), 1)
        cur = lax.shift_right_logical(qlane, SEL_BLOCK.bit_length() - 1)
        valid = blk * SEL_BLOCK <= qlane
        forced = (blk == 0) | (blk == cur) | (blk == cur - 1)
        score = jnp.where(valid, jnp.where(forced, FORCE_SCORE, imp), NEG_INF)
        rank = jnp.zeros((n_blk, tq), F32)
        for i in range((q0 + tq) // SEL_BLOCK):
            ci = score[i:i + 1, :]
            beats = (ci > score) | ((ci == score) & (blk > i))
            rank = rank + jnp.where(beats, 1.0, 0.0)
        sel_t = jnp.where((rank < SEL_TOPK) & valid, 1.0, 0.0)
        sel = jnp.concatenate([sel_t, jnp.zeros((LANES - n_blk, tq), F32)], axis=0).T.astype(BF16)

    def attend_pair(first, second):
        def scores(k_ref, chunks):
            for c, bias in chunks:
                s = _dot_nt(q_rot, k_ref[0, c * kt:(c + 1) * kt, :])
                yield s if bias is None else s + jnp.concatenate([bias] * GQA_GROUP, axis=0)

        def row_max(ss):
            top = ss[0]
            for s in ss[1:]:
                top = jnp.maximum(top, s)
            return jnp.max(top, axis=-1, keepdims=True)

        def probs(ss, m):
            for s in ss:
                yield jnp.exp2(s - m)

        def weighted(v_ref, chunks, ps):
            for (c, _), p in zip(chunks, ps):
                yield _dot(p.astype(BF16), v_ref[0, c * kt:(c + 1) * kt, :])

        def alternate(lead, follow):
            a, b = [], []
            for item in lead:
                a.append(item)
                nxt = next(follow, None)
                if nxt is not None:
                    b.append(nxt)
            b.extend(follow)
            return a, b

        (k1, v1, ch1), (k2, v2, ch2) = first, second
        s1 = list(scores(k1, ch1))
        m1 = row_max(s1)
        p1, s2 = alternate(probs(s1, m1), scores(k2, ch2))
        m2 = row_max(s2)
        pv1, p2 = alternate(weighted(v1, ch1, p1), probs(s2, m2))
        pv2 = list(weighted(v2, ch2, p2))
        out = lambda pv, p: sum(pv[1:], pv[0]) / jnp.sum(sum(p[1:], p[0]), axis=-1, keepdims=True)
        return out(pv1, p1), out(pv2, p2)

    def mask_bias(c, ok):
        kpos = c * kt + lax.broadcasted_iota(jnp.int32, (1, kt), 1)
        return jnp.where(ok(kpos), 0.0, NEG_INF)

    sel_chunks = []
    for c in range(q + 1):
        if ranked:
            chosen = _dot(sel, ex_ref[:, c * kt:(c + 1) * kt]) > 0.5
            sel_chunks.append((c, mask_bias(c, lambda kpos: chosen & (kpos <= qpos))))
        else:
            sel_chunks.append((c, mask_bias(c, lambda kpos: kpos <= qpos) if c == q else None))

    win_chunks = []
    for c in range(q + 1):
        if c * kt + kt - 1 <= q0 - WINDOW:
            continue
        whole = c * kt + kt - 1 <= q0 and c * kt > q0 + tq - 1 - WINDOW
        win_chunks.append((c, None if whole else mask_bias(c, lambda kpos: (kpos <= qpos) & (kpos > qpos - WINDOW))))
    o_sel, o_win = attend_pair((ks_ref, vs_ref, sel_chunks), (kw_ref, vw_ref, win_chunks))

    gates = gates_ref[0]
    for g in range(GQA_GROUP):
        r = slice(g * tq, (g + 1) * tq)
        gate = lambda br: gates[:, br * GQA_GROUP + g:br * GQA_GROUP + g + 1]
        o = gate(0) * o_cmp[r] + gate(1) * o_sel[r] + gate(2) * o_win[r]
        o_ref[:, g * HEAD_DIM:(g + 1) * HEAD_DIM] = o.astype(BF16)


def _attn_prompt_body(*refs, tq, seq, with_win):
    if with_win:
        (*refs, sk_ref, sv_ref, nk_ref, nv_ref, o_ref, ok_ref, ov_ref) = refs
        _win_update_body(sk_ref, sv_ref, nk_ref, nv_ref, ok_ref, ov_ref)
        refs = (*refs, o_ref)
    qi = pl.program_id(2)
    for q in range(seq // tq):
        @pl.when(qi == q)
        def _(q=q):
            _attn_prompt_tile(q, *refs, tq=tq, seq=seq)


def _attn_prompt(q_raw, q_rot, kcc, vcc, ks_hm, vs_hm, kw_hm, vw_hm, gates, ovt, ex, batch, seq, win=None, tq=ATTN_TQ):
    nq = seq // tq
    n_cmp = seq // CMP_STRIDE
    qspec = pl.BlockSpec((tq, KV_DIM), lambda b, h, i: (b * nq + i, h))
    cspec = pl.BlockSpec((1, 1, n_cmp, HEAD_DIM), lambda b, h, i: (b, h, 0, 0))
    kvspec = pl.BlockSpec((1, seq, HEAD_DIM), lambda b, h, i: (h, b, 0))
    in_specs = [qspec, qspec, cspec, cspec, kvspec, kvspec, kvspec, kvspec,
                pl.BlockSpec((1, tq, LANES), lambda b, h, i: (h, b * nq + i, 0)),
                pl.BlockSpec(ovt.shape, lambda b, h, i: (0, 0)),
                pl.BlockSpec((LANES, seq), lambda b, h, i: (0, 0))]
    args = [q_raw, q_rot, kcc, vcc, ks_hm, vs_hm, kw_hm, vw_hm, gates, ovt, ex]
    out_specs = [qspec]
    out_shape = [jax.ShapeDtypeStruct((batch * seq, Q_DIM), BF16)]
    if win is not None:
        state_k = win[0]
        tb = state_k.shape[0] // (batch * N_KV_HEADS * nq)
        step = lambda b, h, i: ((b * N_KV_HEADS + h) * nq + i, 0, 0, 0)
        big = pl.BlockSpec((tb,) + state_k.shape[1:], step)
        one = pl.BlockSpec((tb, 1) + state_k.shape[2:], step)
        in_specs += [big, big, one, one]
        args += list(win)
        out_specs += [big, big]
        out_shape += [jax.ShapeDtypeStruct(state_k.shape, state_k.dtype)] * 2
    return pl.pallas_call(
        functools.partial(_attn_prompt_body, tq=tq, seq=seq, with_win=win is not None),
        grid=(batch, N_KV_HEADS, nq),
        in_specs=in_specs,
        out_specs=out_specs,
        out_shape=out_shape,
        compiler_params=_cparams("parallel", "parallel", "arbitrary"),
        name="attn_prompt",
    )(*args)


def _matmul_res_body(a_ref, w_ref, r_ref, o_ref):
    o_ref[...] = r_ref[...] + _dot(a_ref[...], w_ref[...])


def _matmul_residual(a, w, res, tm):
    m, k = a.shape
    n = w.shape[1]
    return pl.pallas_call(
        _matmul_res_body,
        grid=(m // tm,),
        in_specs=[pl.BlockSpec((tm, k), lambda i: (i, 0)), pl.BlockSpec((k, n), lambda i: (0, 0)),
                  pl.BlockSpec((tm, n), lambda i: (i, 0))],
        out_specs=pl.BlockSpec((tm, n), lambda i: (i, 0)),
        out_shape=jax.ShapeDtypeStruct((m, n), F32),
        compiler_params=_cparams("parallel"),
        name="out_proj",
    )(a, w, res)


def _attn_sample_body(pt_ref, *refs, n_pages, past):
    del pt_ref
    pages = refs[:4 * n_pages]
    (wink_ref, winv_ref, qraw_ref, qrot_ref, kse_ref, vse_ref, kwe_ref, vwe_ref, gates_ref,
     wpk_ref, pek_ref, w2k_ref, wpv_ref, pev_ref, w2v_ref, ov_ref, gs_ref, o_ref, s_ref) = refs[4 * n_pages:]
    ck, cv, sk, sv = (pages[i * n_pages:(i + 1) * n_pages] for i in range(4))
    n_sub = past // CMP_STRIDE
    n_rows = n_sub * N_KV_HEADS
    sub_pp = PAGE_SIZE // CMP_STRIDE
    page_rows = PAGE_SIZE * N_KV_HEADS

    def rows_of(pg):
        return lambda k: jnp.concatenate(
            [r[0, :, k * SUBLANES:(k + 1) * SUBLANES, :].reshape(sub_pp * SUBLANES, HEAD_DIM) for r in pg], axis=0)

    kcc, vcc = _compress([(rows_of(ck), wpk_ref, pek_ref, w2k_ref), (rows_of(cv), wpv_ref, pev_ref, w2v_ref)], n_sub)
    kcc, vcc = kcc.astype(BF16), vcc.astype(BF16)

    q_raw = qraw_ref[0]
    q_rot = qrot_ref[0]
    q_rot32 = q_rot.astype(F32)
    hrow = lax.shift_right_logical(lax.broadcasted_iota(jnp.int32, (N_HEADS, 1), 0), 2)

    def softmax_parts(s_old, s_new):
        m = jnp.maximum(jnp.max(s_old, axis=-1, keepdims=True), s_new)
        p_old = jnp.exp2(s_old - m)
        p_new = jnp.exp2(s_new - m)
        return p_old, p_new, jnp.sum(p_old, axis=-1, keepdims=True) + p_new

    col = lax.broadcasted_iota(jnp.int32, (1, n_rows), 1)
    cmask = ((col & (N_KV_HEADS - 1)) == hrow) & (
        lax.shift_right_logical(col, 2) * CMP_STRIDE + (CMP_BLOCK - 1) <= past)
    scm = jnp.where(cmask, _dot_nt(q_raw, kcc), NEG_INF)
    e = jnp.where(cmask, jnp.exp2(scm - jnp.max(scm, axis=-1, keepdims=True)), 0.0)
    pc = e / jnp.maximum(jnp.sum(e, axis=-1, keepdims=True), TINY)
    o_cmp = _dot(pc.astype(BF16), vcc)

    imp16 = _split_dot(pc, ov_ref[...])
    hi = imp16.astype(BF16)
    lo = (imp16 - hi.astype(F32)).astype(BF16)
    imp = _dot(gs_ref[...], hi) + _dot(gs_ref[...], lo)
    n_blk = past // SEL_BLOCK + 1
    cur = past // SEL_BLOCK
    blk = lax.broadcasted_iota(jnp.int32, (N_HEADS, LANES), 1)
    valid = blk < n_blk
    forced = (blk == 0) | (blk == cur) | (blk == cur - 1)
    score = jnp.where(valid, jnp.where(forced, FORCE_SCORE, imp), NEG_INF)
    sel = jnp.where(_topk_select(score, valid, n_blk), 1.0, 0.0)

    pcol = lax.broadcasted_iota(jnp.int32, (1, page_rows), 1)
    head_ok = (pcol & (N_KV_HEADS - 1)) == hrow
    upper = pcol >= page_rows // 2
    for p in range(n_pages):
        s = _dot_nt(q_rot, sk[p][0].reshape(page_rows, HEAD_DIM).astype(BF16))
        chosen = jnp.where(upper, sel[:, 2 * p + 1:2 * p + 2], sel[:, 2 * p:2 * p + 1])
        s_ref[:, p * page_rows:(p + 1) * page_rows] = jnp.where(head_ok & (chosen > 0.5), s, NEG_INF)
    s_new = jnp.sum(q_rot32 * kse_ref[0], axis=-1, keepdims=True)
    p_old, p_new, denom = softmax_parts(s_ref[...], s_new)
    acc = p_new * vse_ref[0]
    for p in range(n_pages):
        acc = acc + _dot(p_old[:, p * page_rows:(p + 1) * page_rows].astype(BF16),
                         sv[p][0].reshape(page_rows, HEAD_DIM).astype(BF16))
    o_sel = acc / denom

    n_win = wink_ref.shape[1]
    wcol = lax.broadcasted_iota(jnp.int32, (1, n_win), 1)
    wmask = ((wcol & (N_KV_HEADS - 1)) == hrow) & (lax.shift_right_logical(wcol, 2) >= n_win // N_KV_HEADS + 1 - WINDOW)
    sw = jnp.where(wmask, _dot_nt(q_rot, wink_ref[0].astype(BF16)), NEG_INF)
    sw_new = jnp.sum(q_rot32 * kwe_ref[0], axis=-1, keepdims=True)
    p_old, p_new, denom = softmax_parts(sw, sw_new)
    o_win = (p_new * vwe_ref[0] + _dot(p_old.astype(BF16), winv_ref[0].astype(BF16))) / denom

    gates = gates_ref[0]
    o = gates[:, 0:1] * o_cmp + gates[:, 1:2] * o_sel + gates[:, 2:3] * o_win
    o_ref[0] = o.astype(BF16)


def _attn_sample(page_table, caches, win_k, win_v, q_raw, q_rot, new_rows, gates, cmp_k_w, cmp_v_w, ov, gs):
    batch, n_pages = page_table.shape
    past = n_pages * PAGE_SIZE
    chunk = CMP_STRIDE * N_KV_HEADS
    page_specs = [pl.BlockSpec((1, PAGE_SIZE // CMP_STRIDE, chunk, HEAD_DIM), lambda b, pt, p=p: (pt[b, p], 0, 0, 0))
                  for p in range(n_pages)]
    per_b = lambda a: pl.BlockSpec((1,) + a.shape[1:], lambda b, pt: (b,) + (0,) * (a.ndim - 1))
    whole = lambda a: pl.BlockSpec(a.shape, lambda b, pt: (0,) * a.ndim)
    ins, specs = [], []
    for c in caches:
        ins += [c] * n_pages
        specs += page_specs
    for a in (win_k, win_v, q_raw, q_rot, *new_rows, gates):
        ins.append(a)
        specs.append(per_b(a))
    for a in (*cmp_k_w, *cmp_v_w, ov, gs):
        ins.append(a)
        specs.append(whole(a))
    return pl.pallas_call(
        functools.partial(_attn_sample_body, n_pages=n_pages, past=past),
        grid_spec=pltpu.PrefetchScalarGridSpec(
            num_scalar_prefetch=1, grid=(batch,), in_specs=specs,
            out_specs=pl.BlockSpec((1, N_HEADS, HEAD_DIM), lambda b, pt: (b, 0, 0)),
            scratch_shapes=[pltpu.VMEM((N_HEADS, past * N_KV_HEADS), F32)]),
        out_shape=jax.ShapeDtypeStruct((batch, N_HEADS, HEAD_DIM), BF16),
        compiler_params=_cparams("arbitrary"),
        name="attn_sample",
    )(page_table, *ins)


def _win_update_body(sk_ref, sv_ref, nk_ref, nv_ref, ok_ref, ov_ref):
    width = sk_ref.shape[1]
    for s_ref, n_ref, o_ref in ((sk_ref, nk_ref, ok_ref), (sv_ref, nv_ref, ov_ref)):
        o_ref[:, 0:width - 1] = s_ref[:, 1:width]
        o_ref[:, width - 1:width] = n_ref[...]


def _win_update(state_k, state_v, new_k, new_v, tb=4):
    batch, width = state_k.shape[:2]
    big = pl.BlockSpec((tb, width, N_KV_HEADS, HEAD_DIM), lambda i: (i, 0, 0, 0))
    one = pl.BlockSpec((tb, 1, N_KV_HEADS, HEAD_DIM), lambda i: (i, 0, 0, 0))
    sds = jax.ShapeDtypeStruct(state_k.shape, state_k.dtype)
    return pl.pallas_call(
        _win_update_body,
        grid=(batch // tb,),
        in_specs=[big, big, one, one],
        out_specs=[big, big],
        out_shape=[sds, sds],
        compiler_params=_cparams("parallel"),
        name="win_update",
    )(state_k, state_v, new_k, new_v)


def _rope_tables(pos):
    half = HEAD_DIM // 2
    inv_freq = jnp.power(ROPE_THETA, -jnp.arange(half, dtype=F32) / half)
    ang = pos.astype(F32)[:, None] * inv_freq[None, :]
    cos, sin = jnp.cos(ang), jnp.sin(ang)
    return jnp.concatenate([cos, cos], axis=1), jnp.concatenate([-sin, sin], axis=1)


def _overlap(n_sub, n_cols):
    n = jnp.arange(n_sub)[:, None]
    j = jnp.arange(LANES)[None, :]
    c_start, s_start = n * CMP_STRIDE, j * SEL_BLOCK
    hit = (c_start <= s_start + SEL_BLOCK - 1) & (c_start + CMP_BLOCK - 1 >= s_start)
    return (hit & (n < n_sub - 1) & (j < n_cols)).astype(BF16)


def kernel(x_prompt, x_sample, state_pool, cache_cmp_k, cache_cmp_v, cache_sel_k, cache_sel_v, state_win_k, state_win_v, page_table, norm_ffn1, norm_mix, norm_ffn2, norm_final, w_ffn_gate, w_ffn_up, w_ffn_down, w_pool, pool_scale, w_nsa_in, w_nsa_out, cmp_pe_k, cmp_w1_k, cmp_w2_k, cmp_pe_v, cmp_w1_v, cmp_w2_v):
    batch, seq, _ = x_prompt.shape
    dec_batch, dec_seq, _ = x_sample.shape
    past = page_table.shape[1] * PAGE_SIZE
    win_buf = state_win_k.shape[2]
    assert dec_seq == 1 and past % SEL_BLOCK == 0 and win_buf == WINDOW and seq % ATTN_TQ == 0
    assert norm_ffn1.shape[0] == 2 and w_pool.shape[0] == 1 and w_nsa_in.shape[0] == 1

    vec = lambda v: v.reshape(1, D_MODEL)
    wpool = w_pool[0].astype(BF16)
    n_main = Q_DIM + 6 * KV_DIM
    w_main = w_nsa_in[0].astype(BF16)
    wgt = w_nsa_in[0][:, n_main:].reshape(D_MODEL, N_KV_HEADS, GQA_GROUP, N_BRANCHES)
    wgt = jnp.transpose(wgt, (0, 1, 3, 2)).reshape(D_MODEL, N_HEADS * N_BRANCHES)
    wgt = jnp.pad(wgt, ((0, 0), (0, LANES - N_HEADS * N_BRANCHES))).astype(BF16)
    w_out = w_nsa_out[0].astype(BF16)
    cmp_k_w = _compress_weights(cmp_pe_k[0], cmp_w1_k[0], cmp_w2_k[0])
    cmp_v_w = _compress_weights(cmp_pe_v[0], cmp_w1_v[0], cmp_w2_v[0])
    ffn = lambda xp, xs, g, layer, which, gf=None: _ffn(xp, xs, vec(g), w_ffn_gate, w_ffn_up, w_ffn_down, layer, which,
                                                        g_final=gf)

    xp = x_prompt.reshape(batch * seq, D_MODEL)
    xs = x_sample.reshape(dec_batch, D_MODEL)
    xp, xs = ffn(xp, xs, norm_ffn1[0], 0, 0)

    xp, pool_p = _pool_prompt(xp, vec(norm_mix[0]), wpool, vec(pool_scale[0]), batch, seq)
    xs, h_s = _pool_sample(xs, jnp.transpose(state_pool[0], (1, 0, 2)), vec(norm_mix[0]), wpool, vec(pool_scale[0]))
    pool_s = jnp.concatenate([state_pool[0][:, 1:], h_s[:, None]], axis=1)
    xp, xs = ffn(xp, xs, norm_ffn2[0], 0, 1)
    xp, xs = ffn(xp, xs, norm_ffn1[1], 1, 0)

    tm = 256
    cos_p, sin_p = _rope_tables(jnp.arange(seq))
    (q_raw, q_rot, kc_p, vc_p, ks_p, vs_p, kw_p, vw_p, ks_hm, vs_hm, kw_hm, vw_hm, gates_p) = _nsa_project(
        xp, vec(norm_mix[1]), w_main, wgt, cos_p, sin_p, seq // tm, tm)
    cos_s, sin_s = _rope_tables(jnp.full((dec_batch,), past))
    (q_raw_s, q_rot_s, kc_s, vc_s, ks_s, vs_s, kw_s, vw_s, _, _, _, _, gates_s) = _nsa_project(
        xs, vec(norm_mix[1]), w_main, wgt, cos_s, sin_s, 1, dec_batch)

    kcc, vcc = _compress_prompt(kc_p, vc_p, cmp_k_w, cmp_v_w, batch, seq)
    n_sub = seq // CMP_STRIDE
    expand = (jnp.arange(LANES)[:, None] == jnp.arange(seq)[None, :] // SEL_BLOCK).astype(BF16)
    ovt = jnp.transpose(_overlap(n_sub, seq // SEL_BLOCK))[:seq // SEL_BLOCK]
    new_row = lambda a: a.reshape(dec_batch, 1, N_KV_HEADS, HEAD_DIM)
    win = (state_win_k[0], state_win_v[0], new_row(kw_s), new_row(vw_s))
    attn_steps = batch * N_KV_HEADS * (seq // ATTN_TQ)
    if dec_batch % attn_steps == 0:
        o_p, win_k_s, win_v_s = _attn_prompt(q_raw, q_rot, kcc, vcc, ks_hm, vs_hm, kw_hm, vw_hm, gates_p, ovt, expand,
                                             batch, seq, win=win)
    else:
        (o_p,) = _attn_prompt(q_raw, q_rot, kcc, vcc, ks_hm, vs_hm, kw_hm, vw_hm, gates_p, ovt, expand, batch, seq)
        win_k_s, win_v_s = _win_update(*win)
    xp = _matmul_residual(o_p, w_out, xp, 2 * tm)

    per_head = lambda a: jnp.repeat(a.reshape(dec_batch, N_KV_HEADS, HEAD_DIM), GQA_GROUP, axis=1)
    g3 = gates_s[:, :, :N_BRANCHES * GQA_GROUP].reshape(N_KV_HEADS, dec_batch, N_BRANCHES, GQA_GROUP)
    g3 = jnp.transpose(g3, (1, 0, 3, 2)).reshape(dec_batch, N_HEADS, N_BRANCHES)
    chunk = CMP_STRIDE * N_KV_HEADS
    paged = lambda c: c[0].reshape(c.shape[1], PAGE_SIZE // CMP_STRIDE, chunk, HEAD_DIM)
    win_rows = lambda s: s[0].reshape(dec_batch, win_buf * N_KV_HEADS, HEAD_DIM)
    gs = (jnp.arange(N_HEADS)[:, None] // GQA_GROUP == jnp.arange(N_HEADS)[None, :] // GQA_GROUP).astype(BF16)
    n_sub_s = past // CMP_STRIDE
    ov_s = jnp.repeat(_overlap(n_sub_s, past // SEL_BLOCK + 1), N_KV_HEADS, axis=0)
    o_s = _attn_sample(page_table, [paged(c) for c in (cache_cmp_k, cache_cmp_v, cache_sel_k, cache_sel_v)],
                       win_rows(state_win_k), win_rows(state_win_v),
                       q_raw_s.reshape(dec_batch, N_HEADS, HEAD_DIM), q_rot_s.reshape(dec_batch, N_HEADS, HEAD_DIM),
                       [per_head(a) for a in (ks_s, vs_s, kw_s, vw_s)], g3, cmp_k_w, cmp_v_w, ov_s, gs)
    xs = _matmul_residual(o_s.reshape(dec_batch, Q_DIM), w_out, xs, dec_batch)

    yp, ys = ffn(xp, xs, norm_ffn2[1], 1, 1, vec(norm_final))
    y_prompt = yp.reshape(batch, seq, D_MODEL)
    y_sample = ys.reshape(dec_batch, 1, D_MODEL)

    kv5 = lambda a, b, s: a.reshape(1, b, s, N_KV_HEADS, HEAD_DIM)
    keep = min(WINDOW, seq)
    return (y_prompt, y_sample, pool_p[None], pool_s[None],
            kv5(kc_p, batch, seq), kv5(kc_s, dec_batch, 1), kv5(vc_p, batch, seq), kv5(vc_s, dec_batch, 1),
            kv5(ks_p, batch, seq), kv5(ks_s, dec_batch, 1), kv5(vs_p, batch, seq), kv5(vs_s, dec_batch, 1),
            kv5(kw_p, batch, seq)[:, :, seq - keep:], win_k_s[None],
            kv5(vw_p, batch, seq)[:, :, seq - keep:], win_v_s[None])
```

```python
import functools

import jax
import jax.numpy as jnp
from jax import lax
from jax.experimental import pallas as pl
from jax.experimental.pallas import tpu as pltpu

F32 = jnp.float32
BF16 = jnp.bfloat16

D_MODEL = 2048
D_FF = 5632
POOL_WINDOWS = (2, 4, 8, 16)
POOL_GROUP_DIM = D_MODEL // len(POOL_WINDOWS)
POOL_STATE_LEN = max(POOL_WINDOWS) - 1
N_HEADS = 16
HEAD_DIM = 128
N_KV_HEADS = 4
GQA_GROUP = N_HEADS // N_KV_HEADS
Q_DIM = N_HEADS * HEAD_DIM
KV_DIM = N_KV_HEADS * HEAD_DIM
N_BRANCHES = 3
CMP_BLOCK = 32
CMP_STRIDE = 16
SEL_BLOCK = 64
SEL_TOPK = 16
ATTN_TQ = 256
WINDOW = 512
PAGE_SIZE = 128
SCALE = HEAD_DIM ** -0.5
LOG2E = 1.4426950408889634
ROPE_THETA = 10000.0
RMS_EPS = 1e-6
NEG_INF = -1e30
FORCE_SCORE = 1e9
TINY = 1e-30

LANES = 128
SUBLANES = 8
VMEM_LIMIT = 56 * 1024 * 1024


def _cparams(*sem):
    return pltpu.CompilerParams(dimension_semantics=sem, vmem_limit_bytes=VMEM_LIMIT)


def _rms(x, g):
    y = x * lax.rsqrt(jnp.mean(x * x, axis=-1, keepdims=True) + RMS_EPS)
    return y * g


def _dot(a, b):
    return jnp.dot(a, b, preferred_element_type=F32)


def _dot_nt(a, b):
    return lax.dot_general(a, b, (((1,), (1,)), ((), ())), preferred_element_type=F32)


def _ffn_body(xp_ref, xs_ref, g_ref, wg_ref, wu_ref, wd_ref, *rest, final_norm, tm, ts):
    if final_norm:
        gf_ref, op_ref, os_ref, xn_ref = rest
    else:
        op_ref, os_ref, xn_ref = rest
    i, f = pl.program_id(0), pl.program_id(1)
    last = pl.num_programs(1) - 1

    @pl.when(f == 0)
    def _():
        x = xp_ref[...]
        xn_ref[0:tm] = _rms(x, g_ref[...]).astype(BF16)
        op_ref[...] = x

    @pl.when((f == 0) & (i == 0))
    def _():
        x = xs_ref[...]
        xn_ref[tm:tm + ts] = _rms(x, g_ref[...]).astype(BF16)
        os_ref[...] = x

    def half_step(rows):
        xn = xn_ref[0:rows]
        gate = _dot(xn, wg_ref[...].astype(BF16))
        up = _dot(xn, wu_ref[...].astype(BF16))
        h = (gate * jax.nn.sigmoid(gate)) * up
        return _dot((0.5 * h).astype(BF16), wd_ref[...].astype(BF16))

    @pl.when(i == 0)
    def _():
        y = half_step(tm + ts)
        op_ref[...] += y[0:tm]
        os_ref[...] += y[tm:tm + ts]

    @pl.when(i > 0)
    def _():
        op_ref[...] += half_step(tm)

    if final_norm:
        @pl.when(f == last)
        def _():
            op_ref[...] = _rms(op_ref[...], gf_ref[...])

        @pl.when((f == last) & (i == 0))
        def _():
            os_ref[...] = _rms(os_ref[...], gf_ref[...])


def _ffn(xp, xs, g, wg, wu, wd, layer, which, g_final=None, tm=1024, tf=256):
    m, ts = xp.shape[0], xs.shape[0]
    final_norm = g_final is not None
    row = lambda i, f: (i, 0)
    const = lambda i, f: (0, 0)
    in_specs = [
        pl.BlockSpec((tm, D_MODEL), row),
        pl.BlockSpec((ts, D_MODEL), const),
        pl.BlockSpec((1, D_MODEL), const),
        pl.BlockSpec((None, None, D_MODEL, tf), lambda i, f: (layer, which, 0, f)),
        pl.BlockSpec((None, None, D_MODEL, tf), lambda i, f: (layer, which, 0, f)),
        pl.BlockSpec((None, None, tf, D_MODEL), lambda i, f: (layer, which, f, 0)),
    ]
    args = [xp, xs, g, wg, wu, wd]
    if final_norm:
        in_specs.append(pl.BlockSpec((1, D_MODEL), const))
        args.append(g_final)
    return pl.pallas_call(
        functools.partial(_ffn_body, final_norm=final_norm, tm=tm, ts=ts),
        grid=(m // tm, D_FF // tf),
        in_specs=in_specs,
        out_specs=[pl.BlockSpec((tm, D_MODEL), row), pl.BlockSpec((ts, D_MODEL), const)],
        out_shape=[jax.ShapeDtypeStruct((m, D_MODEL), F32), jax.ShapeDtypeStruct((ts, D_MODEL), F32)],
        scratch_shapes=[pltpu.VMEM((tm + ts, D_MODEL), BF16)],
        compiler_params=_cparams("arbitrary", "arbitrary"),
        name="ffn",
    )(*args)


def _pool_groups(h_cols, x, w_ref, sc_ref, o_ref, window_sum, inv_cnt):
    for gi, w in enumerate(POOL_WINDOWS):
        c0 = gi * POOL_GROUP_DIM
        cols = slice(c0, c0 + POOL_GROUP_DIM)
        hc = h_cols(cols)
        d = window_sum(w, cols, hc) * inv_cnt(w) - hc
        y = _dot(d.astype(BF16), w_ref[gi])
        o_ref[:, cols] = x[:, cols] + y * sc_ref[:, cols]


def _pool_prompt_body(x_ref, g_ref, w_ref, sc_ref, o_ref, st_ref, hbuf, *, ts):
    s = pl.program_id(1)
    halo = 2 * SUBLANES

    @pl.when(s == 0)
    def _():
        hbuf[0:halo, :] = jnp.zeros((halo, D_MODEL), F32)

    @pl.when(s > 0)
    def _():
        hbuf[0:halo, :] = hbuf[ts:ts + halo, :]

    x = x_ref[...]
    hbuf[halo:halo + ts, :] = _rms(x, g_ref[...])
    pos = s * ts + lax.broadcasted_iota(jnp.int32, (ts, 1), 0)

    def window_sum(w, cols, hc):
        acc = hc
        for k in range(1, w):
            acc = acc + hbuf[halo - k:halo - k + ts, cols]
        return acc

    _pool_groups(lambda cols: hbuf[halo:halo + ts, cols], x, w_ref, sc_ref, o_ref, window_sum,
                 lambda w: 1.0 / jnp.minimum(w, pos + 1).astype(F32))

    @pl.when(s == pl.num_programs(1) - 1)
    def _():
        st_ref[0] = hbuf[halo + ts - POOL_STATE_LEN:halo + ts, :]


def _pool_prompt(x, g, w_pool, scale, batch, seq, ts=512):
    nt = seq // ts
    return pl.pallas_call(
        functools.partial(_pool_prompt_body, ts=ts),
        grid=(batch, nt),
        in_specs=[
            pl.BlockSpec((ts, D_MODEL), lambda b, s: (b * nt + s, 0)),
            pl.BlockSpec((1, D_MODEL), lambda b, s: (0, 0)),
            pl.BlockSpec((len(POOL_WINDOWS), POOL_GROUP_DIM, POOL_GROUP_DIM), lambda b, s: (0, 0, 0)),
            pl.BlockSpec((1, D_MODEL), lambda b, s: (0, 0)),
        ],
        out_specs=[
            pl.BlockSpec((ts, D_MODEL), lambda b, s: (b * nt + s, 0)),
            pl.BlockSpec((1, POOL_STATE_LEN, D_MODEL), lambda b, s: (b, 0, 0)),
        ],
        out_shape=[
            jax.ShapeDtypeStruct((batch * seq, D_MODEL), F32),
            jax.ShapeDtypeStruct((batch, POOL_STATE_LEN, D_MODEL), F32),
        ],
        scratch_shapes=[pltpu.VMEM((ts + 2 * SUBLANES, D_MODEL), F32)],
        compiler_params=_cparams("arbitrary", "arbitrary"),
        name="pool_prompt",
    )(x, g, w_pool, scale)


def _pool_sample_body(x_ref, st_ref, g_ref, w_ref, sc_ref, o_ref, h_ref):
    x = x_ref[...]
    h_ref[...] = _rms(x, g_ref[...])

    def window_sum(w, cols, hc):
        acc = hc
        for k in range(1, w):
            acc = acc + st_ref[POOL_STATE_LEN - k, :, cols]
        return acc

    _pool_groups(lambda cols: h_ref[:, cols], x, w_ref, sc_ref, o_ref, window_sum, lambda w: 1.0 / w)


def _pool_sample(x, state_t, g, w_pool, scale, tb=32):
    b = x.shape[0]
    return pl.pallas_call(
        _pool_sample_body,
        grid=(b // tb,),
        in_specs=[
            pl.BlockSpec((tb, D_MODEL), lambda i: (i, 0)),
            pl.BlockSpec((POOL_STATE_LEN, tb, D_MODEL), lambda i: (0, i, 0)),
            pl.BlockSpec((1, D_MODEL), lambda i: (0, 0)),
            pl.BlockSpec((len(POOL_WINDOWS), POOL_GROUP_DIM, POOL_GROUP_DIM), lambda i: (0, 0, 0)),
            pl.BlockSpec((1, D_MODEL), lambda i: (0, 0)),
        ],
        out_specs=[pl.BlockSpec((tb, D_MODEL), lambda i: (i, 0)), pl.BlockSpec((tb, D_MODEL), lambda i: (i, 0))],
        out_shape=[jax.ShapeDtypeStruct((b, D_MODEL), F32), jax.ShapeDtypeStruct((b, D_MODEL), F32)],
        compiler_params=_cparams("parallel"),
        name="pool_sample",
    )(x, state_t, g, w_pool, scale)


def _proj_body(x_ref, g_ref, w_ref, wg_ref, cos_ref, sin_ref,
               qraw_ref, qrot_ref, kc_ref, vc_ref, ks_ref, vs_ref, kw_ref, vw_ref,
               kshm_ref, vshm_ref, kwhm_ref, vwhm_ref, gates_ref, *, tm):
    xn = _rms(x_ref[...], g_ref[...]).astype(BF16)
    gall = jax.nn.sigmoid(_dot(xn, wg_ref[...]))
    per_head = N_BRANCHES * GQA_GROUP
    for kvh in range(N_KV_HEADS):
        gates_ref[kvh] = gall if kvh == 0 else pltpu.roll(gall, LANES - kvh * per_head, 1)

    cos, sin = cos_ref[...], sin_ref[...]
    rope = lambda s: s * cos + pltpu.roll(s, HEAD_DIM // 2, 1) * sin
    tile = lambda j: _dot(xn, w_ref[:, j * KV_DIM:(j + 1) * KV_DIM])
    heads = lambda a: [a[:, h * HEAD_DIM:(h + 1) * HEAD_DIM] for h in range(N_KV_HEADS)]

    nq = Q_DIM // KV_DIM
    for j in range(nq):
        zs = tile(j) * (SCALE * LOG2E)
        qraw_ref[:, j * KV_DIM:(j + 1) * KV_DIM] = zs.astype(BF16)
        for h, s in enumerate(heads(zs)):
            c0 = j * KV_DIM + h * HEAD_DIM
            qrot_ref[:, c0:c0 + HEAD_DIM] = rope(s).astype(BF16)

    kinds = [(kc_ref, None, False), (vc_ref, None, False), (ks_ref, kshm_ref, True), (vs_ref, vshm_ref, False),
             (kw_ref, kwhm_ref, True), (vw_ref, vwhm_ref, False)]
    for off, (ref, hm_ref, rot) in enumerate(kinds):
        for h, s in enumerate(heads(tile(nq + off))):
            s = rope(s) if rot else s
            ref[pl.ds(h, tm, stride=N_KV_HEADS), :] = s
            if hm_ref is not None:
                hm_ref[h] = s.astype(BF16)


def _nsa_project(x, g, w_main, w_gates, cos, sin, rope_period_tiles, tm):
    m = x.shape[0]
    row = lambda i: (i, 0)
    const = lambda i: (0, 0)
    kv_sds = jax.ShapeDtypeStruct((m * N_KV_HEADS, HEAD_DIM), F32)
    hm_sds = jax.ShapeDtypeStruct((N_KV_HEADS, m, HEAD_DIM), BF16)
    q_sds = jax.ShapeDtypeStruct((m, Q_DIM), BF16)
    kv_spec = pl.BlockSpec((tm * N_KV_HEADS, HEAD_DIM), row)
    hm_spec = pl.BlockSpec((N_KV_HEADS, tm, HEAD_DIM), lambda i: (0, i, 0))
    q_spec = pl.BlockSpec((tm, Q_DIM), row)
    return pl.pallas_call(
        functools.partial(_proj_body, tm=tm),
        grid=(m // tm,),
        in_specs=[
            pl.BlockSpec((tm, D_MODEL), row),
            pl.BlockSpec((1, D_MODEL), const),
            pl.BlockSpec(w_main.shape, const, pipeline_mode=pl.Buffered(1)),
            pl.BlockSpec(w_gates.shape, const, pipeline_mode=pl.Buffered(1)),
            pl.BlockSpec((tm, HEAD_DIM), lambda i: (i % rope_period_tiles, 0)),
            pl.BlockSpec((tm, HEAD_DIM), lambda i: (i % rope_period_tiles, 0)),
        ],
        out_specs=[q_spec, q_spec] + [kv_spec] * 6 + [hm_spec] * 4
        + [pl.BlockSpec((N_KV_HEADS, tm, LANES), lambda i: (0, i, 0))],
        out_shape=[q_sds, q_sds] + [kv_sds] * 6 + [hm_sds] * 4
        + [jax.ShapeDtypeStruct((N_KV_HEADS, m, LANES), F32)],
        compiler_params=_cparams("parallel"),
        name="nsa_project",
    )(x, g, w_main, w_gates, cos, sin)


N_PAIR = CMP_STRIDE // 2


def _compress(streams, n_sub):
    n_rows = n_sub * N_KV_HEADS
    low = lax.broadcasted_iota(jnp.int32, (1, SUBLANES, 1), 1) < N_KV_HEADS

    def repack(v):
        pairs = lambda a: a.reshape(n_sub // 2, 2, SUBLANES, HEAD_DIM)
        v4, r4 = pairs(v), pairs(pltpu.roll(v, N_KV_HEADS, 0))
        ev = jnp.where(low, v4[:, 0], r4[:, 1]).reshape(n_rows, HEAD_DIM)
        od = jnp.where(low, r4[:, 1], v4[:, 1]).reshape(n_rows, HEAD_DIM)
        return jnp.concatenate([ev, od], axis=1).astype(BF16)

    accs = [jnp.zeros((n_rows, 2 * HEAD_DIM), F32) for _ in streams]
    biases = [jnp.zeros((SUBLANES, 2 * HEAD_DIM), F32) for _ in streams]
    for k in range(N_PAIR):
        for i, (get_rows, wpair_ref, pepair_ref, _) in enumerate(streams):
            accs[i] = accs[i] + _dot(repack(get_rows(k)), wpair_ref[k])
            biases[i] = biases[i] + _dot(pepair_ref[k].astype(BF16), wpair_ref[k])
    outs = []
    for acc, bias, (_, _, _, w2_ref) in zip(accs, biases, streams):
        first = acc[:, :HEAD_DIM] + bias[0:1, :HEAD_DIM]
        second = acc[:, HEAD_DIM:] + bias[1:2, HEAD_DIM:]
        pre = first + pltpu.roll(second, n_rows - N_KV_HEADS, 0)
        outs.append(_dot(jax.nn.gelu(pre, approximate=True).astype(BF16), w2_ref[...]))
    return outs


def _compress_body(xk_ref, xv_ref, wpk_ref, pek_ref, w2k_ref, wpv_ref, pev_ref, w2v_ref, ok_ref, ov_ref, res_ref, *,
                   n_sub):
    rows = lambda x_ref: lambda k: x_ref[:, k * SUBLANES:(k + 1) * SUBLANES, :].reshape(n_sub * SUBLANES, HEAD_DIM)
    outs = _compress([(rows(xk_ref), wpk_ref, pek_ref, w2k_ref), (rows(xv_ref), wpv_ref, pev_ref, w2v_ref)], n_sub)
    for res, o_ref in zip(outs, (ok_ref, ov_ref)):
        res_ref[...] = res
        for h in range(N_KV_HEADS):
            o_ref[0, h] = res_ref[pl.ds(h, n_sub, stride=N_KV_HEADS), :].astype(BF16)


def _compress_prompt(rows_k, rows_v, cmp_k_w, cmp_v_w, batch, seq):
    n_sub = seq // CMP_STRIDE
    chunk = CMP_STRIDE * N_KV_HEADS
    x3 = lambda rows: rows.reshape(batch * n_sub, chunk, HEAD_DIM)
    xspec = pl.BlockSpec((n_sub, chunk, HEAD_DIM), lambda b: (b, 0, 0))
    wspecs = [pl.BlockSpec((N_PAIR, 2 * HEAD_DIM, 2 * HEAD_DIM), lambda b: (0, 0, 0)),
              pl.BlockSpec((N_PAIR, SUBLANES, 2 * HEAD_DIM), lambda b: (0, 0, 0)),
              pl.BlockSpec((HEAD_DIM, HEAD_DIM), lambda b: (0, 0))]
    ospec = pl.BlockSpec((1, N_KV_HEADS, n_sub, HEAD_DIM), lambda b: (b, 0, 0, 0))
    osds = jax.ShapeDtypeStruct((batch, N_KV_HEADS, n_sub, HEAD_DIM), BF16)
    return pl.pallas_call(
        functools.partial(_compress_body, n_sub=n_sub),
        grid=(batch,),
        in_specs=[xspec, xspec] + wspecs + wspecs,
        out_specs=[ospec, ospec],
        out_shape=[osds, osds],
        scratch_shapes=[pltpu.VMEM((n_sub * N_KV_HEADS, HEAD_DIM), F32)],
        compiler_params=_cparams("parallel"),
        name="compress_prompt",
    )(x3(rows_k), x3(rows_v), *cmp_k_w, *cmp_v_w)


def _compress_weights(pe, w1, w2):
    w1r = w1.reshape(2, N_PAIR, 2, HEAD_DIM, HEAD_DIM)
    wpair = jnp.transpose(w1r, (1, 2, 3, 0, 4)).reshape(N_PAIR, 2 * HEAD_DIM, 2 * HEAD_DIM).astype(BF16)
    per = pe.reshape(2, N_PAIR, 2 * HEAD_DIM)
    pepair = jnp.zeros((N_PAIR, SUBLANES, 2 * HEAD_DIM), F32).at[:, 0:2, :].set(jnp.transpose(per, (1, 0, 2)))
    return wpair, pepair, w2.astype(BF16)


def _topk_select(score, valid, n_blk):
    lane = lax.broadcasted_iota(jnp.int32, score.shape, 1)
    rank = jnp.zeros(score.shape, F32)
    for i in range(n_blk):
        ci = score[:, i:i + 1]
        beats = (ci > score) | ((ci == score) & (lane > i))
        rank = rank + jnp.where(beats, 1.0, 0.0)
    return (rank < SEL_TOPK) & valid


def _split_dot(p, w):
    hi = p.astype(BF16)
    lo = (p - hi.astype(F32)).astype(BF16)
    return _dot(hi, w) + _dot(lo, w)


def _attn_prompt_tile(q, qraw_ref, qrot_ref, kcc_ref, vcc_ref, ks_ref, vs_ref, kw_ref, vw_ref, gates_ref, ovt_ref,
                      ext_ref, o_ref, *, tq, seq):
    q0 = q * tq
    kt = tq
    qpos = q0 + lax.broadcasted_iota(jnp.int32, (tq, 1), 0)
    qpos4 = jnp.concatenate([qpos] * GQA_GROUP, axis=0)
    stack = lambda ref: jnp.concatenate([ref[:, g * HEAD_DIM:(g + 1) * HEAD_DIM] for g in range(GQA_GROUP)], axis=0)
    q_raw = stack(qraw_ref)
    q_rot = stack(qrot_ref)

    n_cmp = seq // CMP_STRIDE
    sc = _dot_nt(q_raw, kcc_ref[0, 0])
    c_end = lax.broadcasted_iota(jnp.int32, (1, n_cmp), 1) * CMP_STRIDE + (CMP_BLOCK - 1)
    cmask = c_end <= qpos4
    scm = jnp.where(cmask, sc, NEG_INF)
    e = jnp.where(cmask, jnp.exp2(scm - jnp.max(scm, axis=-1, keepdims=True)), 0.0)
    pc = e / jnp.maximum(jnp.sum(e, axis=-1, keepdims=True), TINY)
    o_cmp = _dot(pc.astype(BF16), vcc_ref[0, 0])

    ranked = q0 + tq > SEL_TOPK * SEL_BLOCK
    if ranked:
        n_blk = seq // SEL_BLOCK
        psum = pc[0:tq] + pc[tq:2 * tq] + pc[2 * tq:3 * tq] + pc[3 * tq:4 * tq]
        hi = psum.astype(BF16)
        lo = (psum - hi.astype(F32)).astype(BF16)
        imp = _dot_nt(ovt_ref[...], hi) + _dot_nt(ovt_ref[...], lo)
        blk = lax.broadcasted_iota(jnp.int32, (n_blk, 1), 0)
        qlane = q0 + lax.broadcasted_iota(jnp.int32, (1, tq), 1)
        cur = lax.shift_right_logical(qlane, SEL_BLOCK.bit_length() - 1)
        valid = blk * SEL_BLOCK <= qlane
        forced = (blk == 0) | (blk == cur) | (blk == cur - 1)
        score = jnp.where(valid, jnp.where(forced, FORCE_SCORE, imp), NEG_INF)
        rank = jnp.zeros((n_blk, tq), F32)
        for i in range((q0 + tq) // SEL_BLOCK):
            ci = score[i:i + 1, :]
            beats = (ci > score) | ((ci == score) & (blk > i))
            rank = rank + jnp.where(beats, 1.0, 0.0)
        sel_t = jnp.where((rank < SEL_TOPK) & valid, 1.0, 0.0)
        sel = jnp.concatenate([sel_t, jnp.zeros((LANES - n_blk, tq), F32)], axis=0).T
        sel_neg = jnp.where(sel > 0.5, 0.0, NEG_INF).astype(BF16)
        q_sel = jnp.concatenate([q_rot, jnp.concatenate([sel_neg] * GQA_GROUP, axis=0)], axis=1)

    def attend_pair(first, second):
        def scores(k_ref, chunks, by_block):
            for c, bias in chunks:
                k = k_ref[0, c * kt:(c + 1) * kt, :]
                if by_block:
                    s = _dot_nt(q_sel, jnp.concatenate([k, ext_ref[c * kt:(c + 1) * kt, :]], axis=1))
                else:
                    s = _dot_nt(q_rot, k)
                yield s if bias is None else s + jnp.concatenate([bias] * GQA_GROUP, axis=0)

        def row_max(ss):
            top = ss[0]
            for s in ss[1:]:
                top = jnp.maximum(top, s)
            return jnp.max(top, axis=-1, keepdims=True)

        def probs(ss, m):
            for s in ss:
                yield jnp.exp2(s - m)

        ones = jnp.ones((kt, HEAD_DIM), BF16)

        def weighted(v_ref, chunks, ps):
            for (c, _), p in zip(chunks, ps):
                yield _dot(p.astype(BF16), jnp.concatenate([v_ref[0, c * kt:(c + 1) * kt, :], ones], axis=1))

        def alternate(lead, follow):
            a, b = [], []
            for item in lead:
                a.append(item)
                nxt = next(follow, None)
                if nxt is not None:
                    b.append(nxt)
            b.extend(follow)
            return a, b

        (k1, v1, ch1), (k2, v2, ch2) = first, second
        s1 = list(scores(k1, ch1, ranked))
        m1 = row_max(s1)
        p1, s2 = alternate(probs(s1, m1), scores(k2, ch2, False))
        m2 = row_max(s2)
        pv1, p2 = alternate(weighted(v1, ch1, p1), probs(s2, m2))
        pv2 = list(weighted(v2, ch2, p2))
        def out(pv):
            tot = sum(pv[1:], pv[0])
            return tot[:, :HEAD_DIM] / tot[:, HEAD_DIM:]

        return out(pv1), out(pv2)

    def mask_bias(c, ok):
        kpos = c * kt + lax.broadcasted_iota(jnp.int32, (1, kt), 1)
        return jnp.where(ok(kpos), 0.0, NEG_INF)

    sel_chunks = [(c, mask_bias(c, lambda kpos: kpos <= qpos) if c == q else None) for c in range(q + 1)]

    win_chunks = []
    for c in range(q + 1):
        if c * kt + kt - 1 <= q0 - WINDOW:
            continue
        whole = c * kt + kt - 1 <= q0 and c * kt > q0 + tq - 1 - WINDOW
        win_chunks.append((c, None if whole else mask_bias(c, lambda kpos: (kpos <= qpos) & (kpos > qpos - WINDOW))))
    o_sel, o_win = attend_pair((ks_ref, vs_ref, sel_chunks), (kw_ref, vw_ref, win_chunks))

    gates = gates_ref[0]
    for g in range(GQA_GROUP):
        r = slice(g * tq, (g + 1) * tq)
        gate = lambda br: gates[:, br * GQA_GROUP + g:br * GQA_GROUP + g + 1]
        o = gate(0) * o_cmp[r] + gate(1) * o_sel[r] + gate(2) * o_win[r]
        o_ref[:, g * HEAD_DIM:(g + 1) * HEAD_DIM] = o.astype(BF16)


def _attn_prompt_body(*refs, tq, seq, with_win):
    if with_win:
        (*refs, sk_ref, sv_ref, nk_ref, nv_ref, o_ref, ok_ref, ov_ref) = refs
        _win_update_body(sk_ref, sv_ref, nk_ref, nv_ref, ok_ref, ov_ref)
        refs = (*refs, o_ref)
    qi = pl.program_id(2)
    for q in range(seq // tq):
        @pl.when(qi == q)
        def _(q=q):
            _attn_prompt_tile(q, *refs, tq=tq, seq=seq)


def _attn_prompt(q_raw, q_rot, kcc, vcc, ks_hm, vs_hm, kw_hm, vw_hm, gates, ovt, ex, batch, seq, win=None, tq=ATTN_TQ):
    nq = seq // tq
    n_cmp = seq // CMP_STRIDE
    qspec = pl.BlockSpec((tq, KV_DIM), lambda b, h, i: (b * nq + i, h))
    cspec = pl.BlockSpec((1, 1, n_cmp, HEAD_DIM), lambda b, h, i: (b, h, 0, 0))
    kvspec = pl.BlockSpec((1, seq, HEAD_DIM), lambda b, h, i: (h, b, 0))
    in_specs = [qspec, qspec, cspec, cspec, kvspec, kvspec, kvspec, kvspec,
                pl.BlockSpec((1, tq, LANES), lambda b, h, i: (h, b * nq + i, 0)),
                pl.BlockSpec(ovt.shape, lambda b, h, i: (0, 0)),
                pl.BlockSpec((seq, LANES), lambda b, h, i: (0, 0))]
    args = [q_raw, q_rot, kcc, vcc, ks_hm, vs_hm, kw_hm, vw_hm, gates, ovt, ex]
    out_specs = [qspec]
    out_shape = [jax.ShapeDtypeStruct((batch * seq, Q_DIM), BF16)]
    if win is not None:
        state_k = win[0]
        tb = state_k.shape[0] // (batch * N_KV_HEADS * nq)
        step = lambda b, h, i: ((b * N_KV_HEADS + h) * nq + i, 0, 0, 0)
        big = pl.BlockSpec((tb,) + state_k.shape[1:], step)
        one = pl.BlockSpec((tb, 1) + state_k.shape[2:], step)
        in_specs += [big, big, one, one]
        args += list(win)
        out_specs += [big, big]
        out_shape += [jax.ShapeDtypeStruct(state_k.shape, state_k.dtype)] * 2
    return pl.pallas_call(
        functools.partial(_attn_prompt_body, tq=tq, seq=seq, with_win=win is not None),
        grid=(batch, N_KV_HEADS, nq),
        in_specs=in_specs,
        out_specs=out_specs,
        out_shape=out_shape,
        compiler_params=_cparams("parallel", "parallel", "arbitrary"),
        name="attn_prompt",
    )(*args)


def _matmul_res_body(a_ref, w_ref, r_ref, o_ref):
    o_ref[...] = r_ref[...] + _dot(a_ref[...], w_ref[...])


def _matmul_residual(a, w, res, tm):
    m, k = a.shape
    n = w.shape[1]
    return pl.pallas_call(
        _matmul_res_body,
        grid=(m // tm,),
        in_specs=[pl.BlockSpec((tm, k), lambda i: (i, 0)), pl.BlockSpec((k, n), lambda i: (0, 0)),
                  pl.BlockSpec((tm, n), lambda i: (i, 0))],
        out_specs=pl.BlockSpec((tm, n), lambda i: (i, 0)),
        out_shape=jax.ShapeDtypeStruct((m, n), F32),
        compiler_params=_cparams("parallel"),
        name="out_proj",
    )(a, w, res)


def _attn_sample_body(pt_ref, *refs, n_pages, past):
    del pt_ref
    pages = refs[:4 * n_pages]
    (wink_ref, winv_ref, qraw_ref, qrot_ref, kse_ref, vse_ref, kwe_ref, vwe_ref, gates_ref,
     wpk_ref, pek_ref, w2k_ref, wpv_ref, pev_ref, w2v_ref, ov_ref, gs_ref, o_ref, s_ref) = refs[4 * n_pages:]
    ck, cv, sk, sv = (pages[i * n_pages:(i + 1) * n_pages] for i in range(4))
    n_sub = past // CMP_STRIDE
    n_rows = n_sub * N_KV_HEADS
    sub_pp = PAGE_SIZE // CMP_STRIDE
    page_rows = PAGE_SIZE * N_KV_HEADS

    def rows_of(pg):
        return lambda k: jnp.concatenate(
            [r[0, :, k * SUBLANES:(k + 1) * SUBLANES, :].reshape(sub_pp * SUBLANES, HEAD_DIM) for r in pg], axis=0)

    kcc, vcc = _compress([(rows_of(ck), wpk_ref, pek_ref, w2k_ref), (rows_of(cv), wpv_ref, pev_ref, w2v_ref)], n_sub)
    kcc, vcc = kcc.astype(BF16), vcc.astype(BF16)

    q_raw = qraw_ref[0]
    q_rot = qrot_ref[0]
    q_rot32 = q_rot.astype(F32)
    hrow = lax.shift_right_logical(lax.broadcasted_iota(jnp.int32, (N_HEADS, 1), 0), 2)

    def softmax_parts(s_old, s_new):
        m = jnp.maximum(jnp.max(s_old, axis=-1, keepdims=True), s_new)
        p_old = jnp.exp2(s_old - m)
        p_new = jnp.exp2(s_new - m)
        return p_old, p_new, jnp.sum(p_old, axis=-1, keepdims=True) + p_new

    col = lax.broadcasted_iota(jnp.int32, (1, n_rows), 1)
    cmask = ((col & (N_KV_HEADS - 1)) == hrow) & (
        lax.shift_right_logical(col, 2) * CMP_STRIDE + (CMP_BLOCK - 1) <= past)
    scm = jnp.where(cmask, _dot_nt(q_raw, kcc), NEG_INF)
    e = jnp.where(cmask, jnp.exp2(scm - jnp.max(scm, axis=-1, keepdims=True)), 0.0)
    pc = e / jnp.maximum(jnp.sum(e, axis=-1, keepdims=True), TINY)
    o_cmp = _dot(pc.astype(BF16), vcc)

    imp16 = _split_dot(pc, ov_ref[...])
    hi = imp16.astype(BF16)
    lo = (imp16 - hi.astype(F32)).astype(BF16)
    imp = _dot(gs_ref[...], hi) + _dot(gs_ref[...], lo)
    n_blk = past // SEL_BLOCK + 1
    cur = past // SEL_BLOCK
    blk = lax.broadcasted_iota(jnp.int32, (N_HEADS, LANES), 1)
    valid = blk < n_blk
    forced = (blk == 0) | (blk == cur) | (blk == cur - 1)
    score = jnp.where(valid, jnp.where(forced, FORCE_SCORE, imp), NEG_INF)
    sel = jnp.where(_topk_select(score, valid, n_blk), 1.0, 0.0)

    pcol = lax.broadcasted_iota(jnp.int32, (1, page_rows), 1)
    head_ok = (pcol & (N_KV_HEADS - 1)) == hrow
    upper = pcol >= page_rows // 2
    for p in range(n_pages):
        s = _dot_nt(q_rot, sk[p][0].reshape(page_rows, HEAD_DIM).astype(BF16))
        chosen = jnp.where(upper, sel[:, 2 * p + 1:2 * p + 2], sel[:, 2 * p:2 * p + 1])
        s_ref[:, p * page_rows:(p + 1) * page_rows] = jnp.where(head_ok & (chosen > 0.5), s, NEG_INF)
    s_new = jnp.sum(q_rot32 * kse_ref[0], axis=-1, keepdims=True)
    p_old, p_new, denom = softmax_parts(s_ref[...], s_new)
    acc = p_new * vse_ref[0]
    for p in range(n_pages):
        acc = acc + _dot(p_old[:, p * page_rows:(p + 1) * page_rows].astype(BF16),
                         sv[p][0].reshape(page_rows, HEAD_DIM).astype(BF16))
    o_sel = acc / denom

    n_win = wink_ref.shape[1]
    wcol = lax.broadcasted_iota(jnp.int32, (1, n_win), 1)
    wmask = ((wcol & (N_KV_HEADS - 1)) == hrow) & (lax.shift_right_logical(wcol, 2) >= n_win // N_KV_HEADS + 1 - WINDOW)
    sw = jnp.where(wmask, _dot_nt(q_rot, wink_ref[0].astype(BF16)), NEG_INF)
    sw_new = jnp.sum(q_rot32 * kwe_ref[0], axis=-1, keepdims=True)
    p_old, p_new, denom = softmax_parts(sw, sw_new)
    o_win = (p_new * vwe_ref[0] + _dot(p_old.astype(BF16), winv_ref[0].astype(BF16))) / denom

    gates = gates_ref[0]
    o = gates[:, 0:1] * o_cmp + gates[:, 1:2] * o_sel + gates[:, 2:3] * o_win
    o_ref[0] = o.astype(BF16)


def _attn_sample(page_table, caches, win_k, win_v, q_raw, q_rot, new_rows, gates, cmp_k_w, cmp_v_w, ov, gs):
    batch, n_pages = page_table.shape
    past = n_pages * PAGE_SIZE
    chunk = CMP_STRIDE * N_KV_HEADS
    page_specs = [pl.BlockSpec((1, PAGE_SIZE // CMP_STRIDE, chunk, HEAD_DIM), lambda b, pt, p=p: (pt[b, p], 0, 0, 0))
                  for p in range(n_pages)]
    per_b = lambda a: pl.BlockSpec((1,) + a.shape[1:], lambda b, pt: (b,) + (0,) * (a.ndim - 1))
    whole = lambda a: pl.BlockSpec(a.shape, lambda b, pt: (0,) * a.ndim)
    ins, specs = [], []
    for c in caches:
        ins += [c] * n_pages
        specs += page_specs
    for a in (win_k, win_v, q_raw, q_rot, *new_rows, gates):
        ins.append(a)
        specs.append(per_b(a))
    for a in (*cmp_k_w, *cmp_v_w, ov, gs):
        ins.append(a)
        specs.append(whole(a))
    return pl.pallas_call(
        functools.partial(_attn_sample_body, n_pages=n_pages, past=past),
        grid_spec=pltpu.PrefetchScalarGridSpec(
            num_scalar_prefetch=1, grid=(batch,), in_specs=specs,
            out_specs=pl.BlockSpec((1, N_HEADS, HEAD_DIM), lambda b, pt: (b, 0, 0)),
            scratch_shapes=[pltpu.VMEM((N_HEADS, past * N_KV_HEADS), F32)]),
        out_shape=jax.ShapeDtypeStruct((batch, N_HEADS, HEAD_DIM), BF16),
        compiler_params=_cparams("arbitrary"),
        name="attn_sample",
    )(page_table, *ins)


def _win_update_body(sk_ref, sv_ref, nk_ref, nv_ref, ok_ref, ov_ref):
    width = sk_ref.shape[1]
    for s_ref, n_ref, o_ref in ((sk_ref, nk_ref, ok_ref), (sv_ref, nv_ref, ov_ref)):
        o_ref[:, 0:width - 1] = s_ref[:, 1:width]
        o_ref[:, width - 1:width] = n_ref[...]


def _win_update(state_k, state_v, new_k, new_v, tb=4):
    batch, width = state_k.shape[:2]
    big = pl.BlockSpec((tb, width, N_KV_HEADS, HEAD_DIM), lambda i: (i, 0, 0, 0))
    one = pl.BlockSpec((tb, 1, N_KV_HEADS, HEAD_DIM), lambda i: (i, 0, 0, 0))
    sds = jax.ShapeDtypeStruct(state_k.shape, state_k.dtype)
    return pl.pallas_call(
        _win_update_body,
        grid=(batch // tb,),
        in_specs=[big, big, one, one],
        out_specs=[big, big],
        out_shape=[sds, sds],
        compiler_params=_cparams("parallel"),
        name="win_update",
    )(state_k, state_v, new_k, new_v)


def _rope_tables(pos):
    half = HEAD_DIM // 2
    inv_freq = jnp.power(ROPE_THETA, -jnp.arange(half, dtype=F32) / half)
    ang = pos.astype(F32)[:, None] * inv_freq[None, :]
    cos, sin = jnp.cos(ang), jnp.sin(ang)
    return jnp.concatenate([cos, cos], axis=1), jnp.concatenate([-sin, sin], axis=1)


def _overlap(n_sub, n_cols):
    n = jnp.arange(n_sub)[:, None]
    j = jnp.arange(LANES)[None, :]
    c_start, s_start = n * CMP_STRIDE, j * SEL_BLOCK
    hit = (c_start <= s_start + SEL_BLOCK - 1) & (c_start + CMP_BLOCK - 1 >= s_start)
    return (hit & (n < n_sub - 1) & (j < n_cols)).astype(BF16)


def kernel(x_prompt, x_sample, state_pool, cache_cmp_k, cache_cmp_v, cache_sel_k, cache_sel_v, state_win_k, state_win_v, page_table, norm_ffn1, norm_mix, norm_ffn2, norm_final, w_ffn_gate, w_ffn_up, w_ffn_down, w_pool, pool_scale, w_nsa_in, w_nsa_out, cmp_pe_k, cmp_w1_k, cmp_w2_k, cmp_pe_v, cmp_w1_v, cmp_w2_v):
    batch, seq, _ = x_prompt.shape
    dec_batch, dec_seq, _ = x_sample.shape
    past = page_table.shape[1] * PAGE_SIZE
    win_buf = state_win_k.shape[2]
    assert dec_seq == 1 and past % SEL_BLOCK == 0 and win_buf == WINDOW and seq % ATTN_TQ == 0
    assert norm_ffn1.shape[0] == 2 and w_pool.shape[0] == 1 and w_nsa_in.shape[0] == 1

    vec = lambda v: v.reshape(1, D_MODEL)
    wpool = w_pool[0].astype(BF16)
    n_main = Q_DIM + 6 * KV_DIM
    w_main = w_nsa_in[0].astype(BF16)
    wgt = w_nsa_in[0][:, n_main:].reshape(D_MODEL, N_KV_HEADS, GQA_GROUP, N_BRANCHES)
    wgt = jnp.transpose(wgt, (0, 1, 3, 2)).reshape(D_MODEL, N_HEADS * N_BRANCHES)
    wgt = jnp.pad(wgt, ((0, 0), (0, LANES - N_HEADS * N_BRANCHES))).astype(BF16)
    w_out = w_nsa_out[0].astype(BF16)
    cmp_k_w = _compress_weights(cmp_pe_k[0], cmp_w1_k[0], cmp_w2_k[0])
    cmp_v_w = _compress_weights(cmp_pe_v[0], cmp_w1_v[0], cmp_w2_v[0])
    ffn = lambda xp, xs, g, layer, which, gf=None: _ffn(xp, xs, vec(g), w_ffn_gate, w_ffn_up, w_ffn_down, layer, which,
                                                        g_final=gf)

    xp = x_prompt.reshape(batch * seq, D_MODEL)
    xs = x_sample.reshape(dec_batch, D_MODEL)
    xp, xs = ffn(xp, xs, norm_ffn1[0], 0, 0)

    xp, pool_p = _pool_prompt(xp, vec(norm_mix[0]), wpool, vec(pool_scale[0]), batch, seq)
    xs, h_s = _pool_sample(xs, jnp.transpose(state_pool[0], (1, 0, 2)), vec(norm_mix[0]), wpool, vec(pool_scale[0]))
    pool_s = jnp.concatenate([state_pool[0][:, 1:], h_s[:, None]], axis=1)
    xp, xs = ffn(xp, xs, norm_ffn2[0], 0, 1)
    xp, xs = ffn(xp, xs, norm_ffn1[1], 1, 0)

    tm = 256
    cos_p, sin_p = _rope_tables(jnp.arange(seq))
    (q_raw, q_rot, kc_p, vc_p, ks_p, vs_p, kw_p, vw_p, ks_hm, vs_hm, kw_hm, vw_hm, gates_p) = _nsa_project(
        xp, vec(norm_mix[1]), w_main, wgt, cos_p, sin_p, seq // tm, tm)
    cos_s, sin_s = _rope_tables(jnp.full((dec_batch,), past))
    (q_raw_s, q_rot_s, kc_s, vc_s, ks_s, vs_s, kw_s, vw_s, _, _, _, _, gates_s) = _nsa_project(
        xs, vec(norm_mix[1]), w_main, wgt, cos_s, sin_s, 1, dec_batch)

    kcc, vcc = _compress_prompt(kc_p, vc_p, cmp_k_w, cmp_v_w, batch, seq)
    n_sub = seq // CMP_STRIDE
    expand = (jnp.arange(seq)[:, None] // SEL_BLOCK == jnp.arange(LANES)[None, :]).astype(BF16)
    ovt = jnp.transpose(_overlap(n_sub, seq // SEL_BLOCK))[:seq // SEL_BLOCK]
    new_row = lambda a: a.reshape(dec_batch, 1, N_KV_HEADS, HEAD_DIM)
    win = (state_win_k[0], state_win_v[0], new_row(kw_s), new_row(vw_s))
    attn_steps = batch * N_KV_HEADS * (seq // ATTN_TQ)
    if dec_batch % attn_steps == 0:
        o_p, win_k_s, win_v_s = _attn_prompt(q_raw, q_rot, kcc, vcc, ks_hm, vs_hm, kw_hm, vw_hm, gates_p, ovt, expand,
                                             batch, seq, win=win)
    else:
        (o_p,) = _attn_prompt(q_raw, q_rot, kcc, vcc, ks_hm, vs_hm, kw_hm, vw_hm, gates_p, ovt, expand, batch, seq)
        win_k_s, win_v_s = _win_update(*win)
    xp = _matmul_residual(o_p, w_out, xp, 2 * tm)

    per_head = lambda a: jnp.repeat(a.reshape(dec_batch, N_KV_HEADS, HEAD_DIM), GQA_GROUP, axis=1)
    g3 = gates_s[:, :, :N_BRANCHES * GQA_GROUP].reshape(N_KV_HEADS, dec_batch, N_BRANCHES, GQA_GROUP)
    g3 = jnp.transpose(g3, (1, 0, 3, 2)).reshape(dec_batch, N_HEADS, N_BRANCHES)
    chunk = CMP_STRIDE * N_KV_HEADS
    paged = lambda c: c[0].reshape(c.shape[1], PAGE_SIZE // CMP_STRIDE, chunk, HEAD_DIM)
    win_rows = lambda s: s[0].reshape(dec_batch, win_buf * N_KV_HEADS, HEAD_DIM)
    gs = (jnp.arange(N_HEADS)[:, None] // GQA_GROUP == jnp.arange(N_HEADS)[None, :] // GQA_GROUP).astype(BF16)
    n_sub_s = past // CMP_STRIDE
    ov_s = jnp.repeat(_overlap(n_sub_s, past // SEL_BLOCK + 1), N_KV_HEADS, axis=0)
    o_s = _attn_sample(page_table, [paged(c) for c in (cache_cmp_k, cache_cmp_v, cache_sel_k, cache_sel_v)],
                       win_rows(state_win_k), win_rows(state_win_v),
                       q_raw_s.reshape(dec_batch, N_HEADS, HEAD_DIM), q_rot_s.reshape(dec_batch, N_HEADS, HEAD_DIM),
                       [per_head(a) for a in (ks_s, vs_s, kw_s, vw_s)], g3, cmp_k_w, cmp_v_w, ov_s, gs)
    xs = _matmul_residual(o_s.reshape(dec_batch, Q_DIM), w_out, xs, dec_batch)

    yp, ys = ffn(xp, xs, norm_ffn2[1], 1, 1, vec(norm_final))
    y_prompt = yp.reshape(batch, seq, D_MODEL)
    y_sample = ys.reshape(dec_batch, 1, D_MODEL)

    kv5 = lambda a, b, s: a.reshape(1, b, s, N_KV_HEADS, HEAD_DIM)
    keep = min(WINDOW, seq)
    return (y_prompt, y_sample, pool_p[None], pool_s[None],
            kv5(kc_p, batch, seq), kv5(kc_s, dec_batch, 1), kv5(vc_p, batch, seq), kv5(vc_s, dec_batch, 1),
            kv5(ks_p, batch, seq), kv5(ks_s, dec_batch, 1), kv5(vs_p, batch, seq), kv5(vs_s, dec_batch, 1),
            kv5(kw_p, batch, seq)[:, :, seq - keep:], win_k_s[None],
            kv5(vw_p, batch, seq)[:, :, seq - keep:], win_v_s[None])
```

```python
import functools

import jax
import jax.numpy as jnp
from jax import lax
from jax.experimental import pallas as pl
from jax.experimental.pallas import tpu as pltpu

F32 = jnp.float32
BF16 = jnp.bfloat16

D_MODEL = 2048
D_FF = 5632
POOL_WINDOWS = (2, 4, 8, 16)
POOL_GROUP_DIM = D_MODEL // len(POOL_WINDOWS)
POOL_STATE_LEN = max(POOL_WINDOWS) - 1
N_HEADS = 16
HEAD_DIM = 128
N_KV_HEADS = 4
GQA_GROUP = N_HEADS // N_KV_HEADS
Q_DIM = N_HEADS * HEAD_DIM
KV_DIM = N_KV_HEADS * HEAD_DIM
N_BRANCHES = 3
CMP_BLOCK = 32
CMP_STRIDE = 16
SEL_BLOCK = 64
SEL_TOPK = 16
ATTN_TQ = 256
WINDOW = 512
PAGE_SIZE = 128
SCALE = HEAD_DIM ** -0.5
LOG2E = 1.4426950408889634
ROPE_THETA = 10000.0
RMS_EPS = 1e-6
NEG_INF = -1e30
FORCE_SCORE = 1e9
TINY = 1e-30

LANES = 128
SUBLANES = 8
VMEM_LIMIT = 56 * 1024 * 1024


def _cparams(*sem):
    return pltpu.CompilerParams(dimension_semantics=sem, vmem_limit_bytes=VMEM_LIMIT)


def _rms(x, g):
    y = x * lax.rsqrt(jnp.mean(x * x, axis=-1, keepdims=True) + RMS_EPS)
    return y * g


def _dot(a, b):
    return jnp.dot(a, b, preferred_element_type=F32)


def _div_pow2(x, n):
    return lax.shift_right_logical(x, n.bit_length() - 1)


def _dot_nt(a, b):
    return lax.dot_general(a, b, (((1,), (1,)), ((), ())), preferred_element_type=F32)


def _ffn_body(xp_ref, xs_ref, g_ref, wg_ref, wu_ref, wd_ref, *rest, final_norm, n_stage, nf, tm, ts):
    if final_norm:
        gf_ref, op_ref, os_ref, xn_ref = rest
    else:
        op_ref, os_ref, xn_ref = rest
    i, f = pl.program_id(0), pl.program_id(1)
    last = pl.num_programs(1) - 1

    for s in range(n_stage):
        @pl.when(f == s * nf)
        def _(s=s):
            x = xp_ref[...] if s == 0 else op_ref[...]
            xn_ref[0:tm] = _rms(x, g_ref[s:s + 1, :]).astype(BF16)
            if s == 0:
                op_ref[...] = x

        @pl.when((f == s * nf) & (i == 0))
        def _(s=s):
            x = xs_ref[...] if s == 0 else os_ref[...]
            xn_ref[tm:tm + ts] = _rms(x, g_ref[s:s + 1, :]).astype(BF16)
            if s == 0:
                os_ref[...] = x

    def half_step(rows):
        xn = xn_ref[0:rows]
        gate = _dot(xn, wg_ref[...].astype(BF16))
        up = _dot(xn, wu_ref[...].astype(BF16))
        h = (gate * jax.nn.sigmoid(gate)) * up
        return _dot((0.5 * h).astype(BF16), wd_ref[...].astype(BF16))

    @pl.when(i == 0)
    def _():
        y = half_step(tm + ts)
        op_ref[...] += y[0:tm]
        os_ref[...] += y[tm:tm + ts]

    @pl.when(i > 0)
    def _():
        op_ref[...] += half_step(tm)

    if final_norm:
        @pl.when(f == last)
        def _():
            op_ref[...] = _rms(op_ref[...], gf_ref[...])

        @pl.when((f == last) & (i == 0))
        def _():
            os_ref[...] = _rms(os_ref[...], gf_ref[...])


def _ffn(xp, xs, stages, wg, wu, wd, g_final=None, tm=1024, tf=256):
    m, ts = xp.shape[0], xs.shape[0]
    nf = D_FF // tf
    n_stage = len(stages)
    assert n_stage in (1, 2)
    final_norm = g_final is not None
    g = jnp.stack([s[2] for s in stages])
    row = lambda i, f: (i, 0)
    const = lambda i, f: (0, 0)
    pick = lambda f, k: stages[0][k] if n_stage == 1 else jnp.where(f >= nf, stages[1][k], stages[0][k])
    in_specs = [
        pl.BlockSpec((tm, D_MODEL), row),
        pl.BlockSpec((ts, D_MODEL), const),
        pl.BlockSpec((n_stage, D_MODEL), const),
        pl.BlockSpec((None, None, D_MODEL, tf), lambda i, f: (pick(f, 0), pick(f, 1), 0, f % nf)),
        pl.BlockSpec((None, None, D_MODEL, tf), lambda i, f: (pick(f, 0), pick(f, 1), 0, f % nf)),
        pl.BlockSpec((None, None, tf, D_MODEL), lambda i, f: (pick(f, 0), pick(f, 1), f % nf, 0)),
    ]
    args = [xp, xs, g, wg, wu, wd]
    if final_norm:
        in_specs.append(pl.BlockSpec((1, D_MODEL), const))
        args.append(g_final)
    return pl.pallas_call(
        functools.partial(_ffn_body, final_norm=final_norm, n_stage=n_stage, nf=nf, tm=tm, ts=ts),
        grid=(m // tm, n_stage * nf),
        in_specs=in_specs,
        out_specs=[pl.BlockSpec((tm, D_MODEL), row), pl.BlockSpec((ts, D_MODEL), const)],
        out_shape=[jax.ShapeDtypeStruct((m, D_MODEL), F32), jax.ShapeDtypeStruct((ts, D_MODEL), F32)],
        scratch_shapes=[pltpu.VMEM((tm + ts, D_MODEL), BF16)],
        compiler_params=_cparams("arbitrary", "arbitrary"),
        name="ffn",
    )(*args)


def _pool_groups(h_cols, x, w_ref, sc_ref, o_ref, window_sum, inv_cnt):
    for gi, w in enumerate(POOL_WINDOWS):
        c0 = gi * POOL_GROUP_DIM
        cols = slice(c0, c0 + POOL_GROUP_DIM)
        hc = h_cols(cols)
        d = window_sum(w, cols, hc) * inv_cnt(w) - hc
        y = _dot(d.astype(BF16), w_ref[gi])
        o_ref[:, cols] = x[:, cols] + y * sc_ref[:, cols]


def _pool_prompt_body(x_ref, g_ref, w_ref, sc_ref, o_ref, st_ref, hbuf, *, ts):
    s = pl.program_id(1)
    halo = 2 * SUBLANES

    @pl.when(s == 0)
    def _():
        hbuf[0:halo, :] = jnp.zeros((halo, D_MODEL), F32)

    @pl.when(s > 0)
    def _():
        hbuf[0:halo, :] = hbuf[ts:ts + halo, :]

    x = x_ref[...]
    hbuf[halo:halo + ts, :] = _rms(x, g_ref[...])
    pos = s * ts + lax.broadcasted_iota(jnp.int32, (ts, 1), 0)

    def window_sum(w, cols, hc):
        acc = hc
        for k in range(1, w):
            acc = acc + hbuf[halo - k:halo - k + ts, cols]
        return acc

    _pool_groups(lambda cols: hbuf[halo:halo + ts, cols], x, w_ref, sc_ref, o_ref, window_sum,
                 lambda w: 1.0 / jnp.minimum(w, pos + 1).astype(F32))

    @pl.when(s == pl.num_programs(1) - 1)
    def _():
        st_ref[0] = hbuf[halo + ts - POOL_STATE_LEN:halo + ts, :]


def _pool_prompt(x, g, w_pool, scale, batch, seq, ts=512):
    nt = seq // ts
    return pl.pallas_call(
        functools.partial(_pool_prompt_body, ts=ts),
        grid=(batch, nt),
        in_specs=[
            pl.BlockSpec((ts, D_MODEL), lambda b, s: (b * nt + s, 0)),
            pl.BlockSpec((1, D_MODEL), lambda b, s: (0, 0)),
            pl.BlockSpec((len(POOL_WINDOWS), POOL_GROUP_DIM, POOL_GROUP_DIM), lambda b, s: (0, 0, 0)),
            pl.BlockSpec((1, D_MODEL), lambda b, s: (0, 0)),
        ],
        out_specs=[
            pl.BlockSpec((ts, D_MODEL), lambda b, s: (b * nt + s, 0)),
            pl.BlockSpec((1, POOL_STATE_LEN, D_MODEL), lambda b, s: (b, 0, 0)),
        ],
        out_shape=[
            jax.ShapeDtypeStruct((batch * seq, D_MODEL), F32),
            jax.ShapeDtypeStruct((batch, POOL_STATE_LEN, D_MODEL), F32),
        ],
        scratch_shapes=[pltpu.VMEM((ts + 2 * SUBLANES, D_MODEL), F32)],
        compiler_params=_cparams("arbitrary", "arbitrary"),
        name="pool_prompt",
    )(x, g, w_pool, scale)


def _pool_sample_body(x_ref, st_ref, g_ref, w_ref, sc_ref, o_ref, h_ref):
    x = x_ref[...]
    h_ref[...] = _rms(x, g_ref[...])

    def window_sum(w, cols, hc):
        acc = hc
        for k in range(1, w):
            acc = acc + st_ref[POOL_STATE_LEN - k, :, cols]
        return acc

    _pool_groups(lambda cols: h_ref[:, cols], x, w_ref, sc_ref, o_ref, window_sum, lambda w: 1.0 / w)


def _pool_sample(x, state_t, g, w_pool, scale, tb=32):
    b = x.shape[0]
    return pl.pallas_call(
        _pool_sample_body,
        grid=(b // tb,),
        in_specs=[
            pl.BlockSpec((tb, D_MODEL), lambda i: (i, 0)),
            pl.BlockSpec((POOL_STATE_LEN, tb, D_MODEL), lambda i: (0, i, 0)),
            pl.BlockSpec((1, D_MODEL), lambda i: (0, 0)),
            pl.BlockSpec((len(POOL_WINDOWS), POOL_GROUP_DIM, POOL_GROUP_DIM), lambda i: (0, 0, 0)),
            pl.BlockSpec((1, D_MODEL), lambda i: (0, 0)),
        ],
        out_specs=[pl.BlockSpec((tb, D_MODEL), lambda i: (i, 0)), pl.BlockSpec((tb, D_MODEL), lambda i: (i, 0))],
        out_shape=[jax.ShapeDtypeStruct((b, D_MODEL), F32), jax.ShapeDtypeStruct((b, D_MODEL), F32)],
        compiler_params=_cparams("parallel"),
        name="pool_sample",
    )(x, state_t, g, w_pool, scale)


def _proj_body(x_ref, g_ref, w_ref, wg_ref, cos_ref, sin_ref,
               qraw_ref, qrot_ref, kc_ref, vc_ref, ks_ref, vs_ref, kw_ref, vw_ref,
               kshm_ref, vshm_ref, kwhm_ref, vwhm_ref, gates_ref, *, tm):
    xn = _rms(x_ref[...], g_ref[...]).astype(BF16)
    gall = jax.nn.sigmoid(_dot(xn, wg_ref[...]))
    per_head = N_BRANCHES * GQA_GROUP
    for kvh in range(N_KV_HEADS):
        gates_ref[kvh] = gall if kvh == 0 else pltpu.roll(gall, LANES - kvh * per_head, 1)

    cos, sin = cos_ref[...], sin_ref[...]
    rope = lambda s: s * cos + pltpu.roll(s, HEAD_DIM // 2, 1) * sin
    tile = lambda j: _dot(xn, w_ref[:, j * KV_DIM:(j + 1) * KV_DIM])
    heads = lambda a: [a[:, h * HEAD_DIM:(h + 1) * HEAD_DIM] for h in range(N_KV_HEADS)]

    nq = Q_DIM // KV_DIM
    for j in range(nq):
        zs = tile(j) * (SCALE * LOG2E)
        qraw_ref[:, j * KV_DIM:(j + 1) * KV_DIM] = zs.astype(BF16)
        for h, s in enumerate(heads(zs)):
            c0 = j * KV_DIM + h * HEAD_DIM
            qrot_ref[:, c0:c0 + HEAD_DIM] = rope(s).astype(BF16)

    kinds = [(kc_ref, None, False), (vc_ref, None, False), (ks_ref, kshm_ref, True), (vs_ref, vshm_ref, False),
             (kw_ref, kwhm_ref, True), (vw_ref, vwhm_ref, False)]
    for off, (ref, hm_ref, rot) in enumerate(kinds):
        for h, s in enumerate(heads(tile(nq + off))):
            s = rope(s) if rot else s
            ref[pl.ds(h, tm, stride=N_KV_HEADS), :] = s
            if hm_ref is not None:
                hm_ref[h] = s.astype(BF16)


def _nsa_project(x, g, w_main, w_gates, cos, sin, rope_period_tiles, tm):
    m = x.shape[0]
    row = lambda i: (i, 0)
    const = lambda i: (0, 0)
    kv_sds = jax.ShapeDtypeStruct((m * N_KV_HEADS, HEAD_DIM), F32)
    hm_sds = jax.ShapeDtypeStruct((N_KV_HEADS, m, HEAD_DIM), BF16)
    q_sds = jax.ShapeDtypeStruct((m, Q_DIM), BF16)
    kv_spec = pl.BlockSpec((tm * N_KV_HEADS, HEAD_DIM), row)
    hm_spec = pl.BlockSpec((N_KV_HEADS, tm, HEAD_DIM), lambda i: (0, i, 0))
    q_spec = pl.BlockSpec((tm, Q_DIM), row)
    return pl.pallas_call(
        functools.partial(_proj_body, tm=tm),
        grid=(m // tm,),
        in_specs=[
            pl.BlockSpec((tm, D_MODEL), row),
            pl.BlockSpec((1, D_MODEL), const),
            pl.BlockSpec(w_main.shape, const, pipeline_mode=pl.Buffered(1)),
            pl.BlockSpec(w_gates.shape, const, pipeline_mode=pl.Buffered(1)),
            pl.BlockSpec((tm, HEAD_DIM), lambda i: (i % rope_period_tiles, 0)),
            pl.BlockSpec((tm, HEAD_DIM), lambda i: (i % rope_period_tiles, 0)),
        ],
        out_specs=[q_spec, q_spec] + [kv_spec] * 6 + [hm_spec] * 4
        + [pl.BlockSpec((N_KV_HEADS, tm, LANES), lambda i: (0, i, 0))],
        out_shape=[q_sds, q_sds] + [kv_sds] * 6 + [hm_sds] * 4
        + [jax.ShapeDtypeStruct((N_KV_HEADS, m, LANES), F32)],
        compiler_params=_cparams("parallel"),
        name="nsa_project",
    )(x, g, w_main, w_gates, cos, sin)


N_PAIR = CMP_STRIDE // 2


def _compress(streams, n_sub):
    n_rows = n_sub * N_KV_HEADS
    low = lax.broadcasted_iota(jnp.int32, (1, SUBLANES, 1), 1) < N_KV_HEADS

    def repack(v):
        pairs = lambda a: a.reshape(n_sub // 2, 2, SUBLANES, HEAD_DIM)
        v4, r4 = pairs(v), pairs(pltpu.roll(v, N_KV_HEADS, 0))
        ev = jnp.where(low, v4[:, 0], r4[:, 1]).reshape(n_rows, HEAD_DIM)
        od = jnp.where(low, r4[:, 1], v4[:, 1]).reshape(n_rows, HEAD_DIM)
        return jnp.concatenate([ev, od], axis=1).astype(BF16)

    accs = [jnp.zeros((n_rows, 2 * HEAD_DIM), F32) for _ in streams]
    biases = [jnp.zeros((SUBLANES, 2 * HEAD_DIM), F32) for _ in streams]
    for k in range(N_PAIR):
        for i, (get_rows, wpair_ref, pepair_ref, _) in enumerate(streams):
            accs[i] = accs[i] + _dot(repack(get_rows(k)), wpair_ref[k])
            biases[i] = biases[i] + _dot(pepair_ref[k].astype(BF16), wpair_ref[k])
    outs = []
    for acc, bias, (_, _, _, w2_ref) in zip(accs, biases, streams):
        first = acc[:, :HEAD_DIM] + bias[0:1, :HEAD_DIM]
        second = acc[:, HEAD_DIM:] + bias[1:2, HEAD_DIM:]
        pre = first + pltpu.roll(second, n_rows - N_KV_HEADS, 0)
        outs.append(_dot(jax.nn.gelu(pre, approximate=True).astype(BF16), w2_ref[...]))
    return outs


def _compress_body(xk_ref, xv_ref, wpk_ref, pek_ref, w2k_ref, wpv_ref, pev_ref, w2v_ref, ok_ref, ov_ref, res_ref, *,
                   n_sub):
    rows = lambda x_ref: lambda k: x_ref[:, k * SUBLANES:(k + 1) * SUBLANES, :].reshape(n_sub * SUBLANES, HEAD_DIM)
    outs = _compress([(rows(xk_ref), wpk_ref, pek_ref, w2k_ref), (rows(xv_ref), wpv_ref, pev_ref, w2v_ref)], n_sub)
    for res, o_ref in zip(outs, (ok_ref, ov_ref)):
        res_ref[...] = res
        for h in range(N_KV_HEADS):
            o_ref[0, h] = res_ref[pl.ds(h, n_sub, stride=N_KV_HEADS), :].astype(BF16)


def _compress_prompt(rows_k, rows_v, cmp_k_w, cmp_v_w, batch, seq):
    n_sub = seq // CMP_STRIDE
    chunk = CMP_STRIDE * N_KV_HEADS
    x3 = lambda rows: rows.reshape(batch * n_sub, chunk, HEAD_DIM)
    xspec = pl.BlockSpec((n_sub, chunk, HEAD_DIM), lambda b: (b, 0, 0))
    wspecs = [pl.BlockSpec((N_PAIR, 2 * HEAD_DIM, 2 * HEAD_DIM), lambda b: (0, 0, 0)),
              pl.BlockSpec((N_PAIR, SUBLANES, 2 * HEAD_DIM), lambda b: (0, 0, 0)),
              pl.BlockSpec((HEAD_DIM, HEAD_DIM), lambda b: (0, 0))]
    ospec = pl.BlockSpec((1, N_KV_HEADS, n_sub, HEAD_DIM), lambda b: (b, 0, 0, 0))
    osds = jax.ShapeDtypeStruct((batch, N_KV_HEADS, n_sub, HEAD_DIM), BF16)
    return pl.pallas_call(
        functools.partial(_compress_body, n_sub=n_sub),
        grid=(batch,),
        in_specs=[xspec, xspec] + wspecs + wspecs,
        out_specs=[ospec, ospec],
        out_shape=[osds, osds],
        scratch_shapes=[pltpu.VMEM((n_sub * N_KV_HEADS, HEAD_DIM), F32)],
        compiler_params=_cparams("parallel"),
        name="compress_prompt",
    )(x3(rows_k), x3(rows_v), *cmp_k_w, *cmp_v_w)


def _compress_weights(pe, w1, w2):
    w1r = w1.reshape(2, N_PAIR, 2, HEAD_DIM, HEAD_DIM)
    wpair = jnp.transpose(w1r, (1, 2, 3, 0, 4)).reshape(N_PAIR, 2 * HEAD_DIM, 2 * HEAD_DIM).astype(BF16)
    per = pe.reshape(2, N_PAIR, 2 * HEAD_DIM)
    pepair = jnp.zeros((N_PAIR, SUBLANES, 2 * HEAD_DIM), F32).at[:, 0:2, :].set(jnp.transpose(per, (1, 0, 2)))
    return wpair, pepair, w2.astype(BF16)


def _topk_select(score, valid, n_blk):
    lane = lax.broadcasted_iota(jnp.int32, score.shape, 1)
    rank = jnp.zeros(score.shape, F32)
    for i in range(n_blk):
        ci = score[:, i:i + 1]
        beats = (ci > score) | ((ci == score) & (lane > i))
        rank = rank + jnp.where(beats, 1.0, 0.0)
    return (rank < SEL_TOPK) & valid


def _split_dot(p, w):
    hi = p.astype(BF16)
    lo = (p - hi.astype(F32)).astype(BF16)
    return _dot(hi, w) + _dot(lo, w)


def _attn_prompt_tile(q, qraw_ref, qrot_ref, kcc_ref, vcc_ref, ks_ref, vs_ref, kw_ref, vw_ref, gates_ref, ovt_ref,
                      ext_ref, o_ref, *, tq, seq):
    q0 = q * tq
    kt = tq
    qpos = q0 + lax.broadcasted_iota(jnp.int32, (tq, 1), 0)
    qpos4 = jnp.concatenate([qpos] * GQA_GROUP, axis=0)
    stack = lambda ref: jnp.concatenate([ref[:, g * HEAD_DIM:(g + 1) * HEAD_DIM] for g in range(GQA_GROUP)], axis=0)
    q_raw = stack(qraw_ref)
    q_rot = stack(qrot_ref)

    n_cmp = seq // CMP_STRIDE
    sc = _dot_nt(q_raw, kcc_ref[0, 0])
    c_end = lax.broadcasted_iota(jnp.int32, (1, n_cmp), 1) * CMP_STRIDE + (CMP_BLOCK - 1)
    cmask = c_end <= qpos4
    scm = jnp.where(cmask, sc, NEG_INF)
    e = jnp.where(cmask, jnp.exp2(scm - jnp.max(scm, axis=-1, keepdims=True)), 0.0)
    pc = e / jnp.maximum(jnp.sum(e, axis=-1, keepdims=True), TINY)
    o_cmp = _dot(pc.astype(BF16), vcc_ref[0, 0])

    ranked = q0 + tq > SEL_TOPK * SEL_BLOCK
    if ranked:
        n_blk = seq // SEL_BLOCK
        psum = pc[0:tq] + pc[tq:2 * tq] + pc[2 * tq:3 * tq] + pc[3 * tq:4 * tq]
        hi = psum.astype(BF16)
        lo = (psum - hi.astype(F32)).astype(BF16)
        imp = _dot_nt(ovt_ref[...], hi) + _dot_nt(ovt_ref[...], lo)
        blk = lax.broadcasted_iota(jnp.int32, (n_blk, 1), 0)
        qlane = q0 + lax.broadcasted_iota(jnp.int32, (1, tq), 1)
        cur = _div_pow2(qlane, SEL_BLOCK)
        valid = blk * SEL_BLOCK <= qlane
        forced = (blk == 0) | (blk == cur) | (blk == cur - 1)
        score = jnp.where(valid, jnp.where(forced, FORCE_SCORE, imp), NEG_INF)
        rank = jnp.zeros((n_blk, tq), F32)
        for i in range((q0 + tq) // SEL_BLOCK):
            ci = score[i:i + 1, :]
            beats = (ci > score) | ((ci == score) & (blk > i))
            rank = rank + jnp.where(beats, 1.0, 0.0)
        sel_t = jnp.where((rank < SEL_TOPK) & valid, 1.0, 0.0)
        sel = jnp.concatenate([sel_t, jnp.zeros((LANES - n_blk, tq), F32)], axis=0).T
        sel_neg = jnp.where(sel > 0.5, 0.0, NEG_INF).astype(BF16)
        q_sel = jnp.concatenate([q_rot, jnp.concatenate([sel_neg] * GQA_GROUP, axis=0)], axis=1)

    def attend_pair(first, second):
        def scores(k_ref, chunks, by_block):
            for c, bias in chunks:
                k = k_ref[0, c * kt:(c + 1) * kt, :]
                if by_block:
                    s = _dot_nt(q_sel, jnp.concatenate([k, ext_ref[c * kt:(c + 1) * kt, :]], axis=1))
                else:
                    s = _dot_nt(q_rot, k)
                yield s if bias is None else s + jnp.concatenate([bias] * GQA_GROUP, axis=0)

        def row_max(ss):
            top = ss[0]
            for s in ss[1:]:
                top = jnp.maximum(top, s)
            return jnp.max(top, axis=-1, keepdims=True)

        def probs(ss, m):
            for s in ss:
                yield jnp.exp2(s - m)

        ones = jnp.ones((kt, HEAD_DIM), BF16)

        def weighted(v_ref, chunks, ps):
            for (c, _), p in zip(chunks, ps):
                yield _dot(p.astype(BF16), jnp.concatenate([v_ref[0, c * kt:(c + 1) * kt, :], ones], axis=1))

        def alternate(lead, follow):
            a, b = [], []
            for item in lead:
                a.append(item)
                nxt = next(follow, None)
                if nxt is not None:
                    b.append(nxt)
            b.extend(follow)
            return a, b

        (k1, v1, ch1), (k2, v2, ch2) = first, second
        s1 = list(scores(k1, ch1, ranked))
        m1 = row_max(s1)
        p1, s2 = alternate(probs(s1, m1), scores(k2, ch2, False))
        m2 = row_max(s2)
        pv1, p2 = alternate(weighted(v1, ch1, p1), probs(s2, m2))
        pv2 = list(weighted(v2, ch2, p2))
        def out(pv):
            tot = sum(pv[1:], pv[0])
            return tot[:, :HEAD_DIM] / tot[:, HEAD_DIM:]

        return out(pv1), out(pv2)

    def mask_bias(c, ok):
        kpos = c * kt + lax.broadcasted_iota(jnp.int32, (1, kt), 1)
        return jnp.where(ok(kpos), 0.0, NEG_INF)

    sel_chunks = [(c, mask_bias(c, lambda kpos: kpos <= qpos) if c == q else None) for c in range(q + 1)]

    win_chunks = []
    for c in range(q + 1):
        if c * kt + kt - 1 <= q0 - WINDOW:
            continue
        whole = c * kt + kt - 1 <= q0 and c * kt > q0 + tq - 1 - WINDOW
        win_chunks.append((c, None if whole else mask_bias(c, lambda kpos: (kpos <= qpos) & (kpos > qpos - WINDOW))))
    o_sel, o_win = attend_pair((ks_ref, vs_ref, sel_chunks), (kw_ref, vw_ref, win_chunks))

    gates = gates_ref[0]
    for g in range(GQA_GROUP):
        r = slice(g * tq, (g + 1) * tq)
        gate = lambda br: gates[:, br * GQA_GROUP + g:br * GQA_GROUP + g + 1]
        o = gate(0) * o_cmp[r] + gate(1) * o_sel[r] + gate(2) * o_win[r]
        o_ref[:, g * HEAD_DIM:(g + 1) * HEAD_DIM] = o.astype(BF16)


def _attn_prompt_body(*refs, tq, seq, with_win):
    if with_win:
        (*refs, sk_ref, sv_ref, nk_ref, nv_ref, o_ref, ok_ref, ov_ref) = refs
        _win_update_body(sk_ref, sv_ref, nk_ref, nv_ref, ok_ref, ov_ref)
        refs = (*refs, o_ref)
    qi = pl.program_id(2)
    for q in range(seq // tq):
        @pl.when(qi == q)
        def _(q=q):
            _attn_prompt_tile(q, *refs, tq=tq, seq=seq)


def _attn_prompt(q_raw, q_rot, kcc, vcc, ks_hm, vs_hm, kw_hm, vw_hm, gates, ovt, ex, batch, seq, win=None, tq=ATTN_TQ):
    nq = seq // tq
    n_cmp = seq // CMP_STRIDE
    qspec = pl.BlockSpec((tq, KV_DIM), lambda b, h, i: (b * nq + i, h))
    cspec = pl.BlockSpec((1, 1, n_cmp, HEAD_DIM), lambda b, h, i: (b, h, 0, 0))
    kvspec = pl.BlockSpec((1, seq, HEAD_DIM), lambda b, h, i: (h, b, 0))
    in_specs = [qspec, qspec, cspec, cspec, kvspec, kvspec, kvspec, kvspec,
                pl.BlockSpec((1, tq, LANES), lambda b, h, i: (h, b * nq + i, 0)),
                pl.BlockSpec(ovt.shape, lambda b, h, i: (0, 0)),
                pl.BlockSpec((seq, LANES), lambda b, h, i: (0, 0))]
    args = [q_raw, q_rot, kcc, vcc, ks_hm, vs_hm, kw_hm, vw_hm, gates, ovt, ex]
    out_specs = [qspec]
    out_shape = [jax.ShapeDtypeStruct((batch * seq, Q_DIM), BF16)]
    if win is not None:
        state_k = win[0]
        tb = state_k.shape[0] // (batch * N_KV_HEADS * nq)
        step = lambda b, h, i: ((b * N_KV_HEADS + h) * nq + i, 0, 0, 0)
        big = pl.BlockSpec((tb,) + state_k.shape[1:], step)
        one = pl.BlockSpec((tb, 1) + state_k.shape[2:], step)
        in_specs += [big, big, one, one]
        args += list(win)
        out_specs += [big, big]
        out_shape += [jax.ShapeDtypeStruct(state_k.shape, state_k.dtype)] * 2
    return pl.pallas_call(
        functools.partial(_attn_prompt_body, tq=tq, seq=seq, with_win=win is not None),
        grid=(batch, N_KV_HEADS, nq),
        in_specs=in_specs,
        out_specs=out_specs,
        out_shape=out_shape,
        compiler_params=_cparams("parallel", "parallel", "arbitrary"),
        name="attn_prompt",
    )(*args)


def _matmul_res_body(a_ref, w_ref, r_ref, o_ref):
    o_ref[...] = r_ref[...] + _dot(a_ref[...], w_ref[...])


def _matmul_residual(a, w, res, tm):
    m, k = a.shape
    n = w.shape[1]
    return pl.pallas_call(
        _matmul_res_body,
        grid=(m // tm,),
        in_specs=[pl.BlockSpec((tm, k), lambda i: (i, 0)), pl.BlockSpec((k, n), lambda i: (0, 0)),
                  pl.BlockSpec((tm, n), lambda i: (i, 0))],
        out_specs=pl.BlockSpec((tm, n), lambda i: (i, 0)),
        out_shape=jax.ShapeDtypeStruct((m, n), F32),
        compiler_params=_cparams("parallel"),
        name="out_proj",
    )(a, w, res)


def _attn_sample_body(pt_ref, *refs, n_pages, past):
    del pt_ref
    pages = refs[:4 * n_pages]
    (wink_ref, winv_ref, qraw_ref, qrot_ref, kse_ref, vse_ref, kwe_ref, vwe_ref, gates_ref,
     wpk_ref, pek_ref, w2k_ref, wpv_ref, pev_ref, w2v_ref, ov_ref, gs_ref, o_ref, s_ref) = refs[4 * n_pages:]
    ck, cv, sk, sv = (pages[i * n_pages:(i + 1) * n_pages] for i in range(4))
    n_sub = past // CMP_STRIDE
    n_rows = n_sub * N_KV_HEADS
    sub_pp = PAGE_SIZE // CMP_STRIDE
    page_rows = PAGE_SIZE * N_KV_HEADS

    def rows_of(pg):
        return lambda k: jnp.concatenate(
            [r[0, :, k * SUBLANES:(k + 1) * SUBLANES, :].reshape(sub_pp * SUBLANES, HEAD_DIM) for r in pg], axis=0)

    kcc, vcc = _compress([(rows_of(ck), wpk_ref, pek_ref, w2k_ref), (rows_of(cv), wpv_ref, pev_ref, w2v_ref)], n_sub)
    kcc, vcc = kcc.astype(BF16), vcc.astype(BF16)

    q_raw = qraw_ref[0]
    q_rot = qrot_ref[0]
    q_rot32 = q_rot.astype(F32)
    hrow = _div_pow2(lax.broadcasted_iota(jnp.int32, (N_HEADS, 1), 0), GQA_GROUP)

    def softmax_parts(s_old, s_new):
        m = jnp.maximum(jnp.max(s_old, axis=-1, keepdims=True), s_new)
        p_old = jnp.exp2(s_old - m)
        p_new = jnp.exp2(s_new - m)
        return p_old, p_new, jnp.sum(p_old, axis=-1, keepdims=True) + p_new

    col = lax.broadcasted_iota(jnp.int32, (1, n_rows), 1)
    cmask = ((col & (N_KV_HEADS - 1)) == hrow) & (
        _div_pow2(col, N_KV_HEADS) * CMP_STRIDE + (CMP_BLOCK - 1) <= past)
    scm = jnp.where(cmask, _dot_nt(q_raw, kcc), NEG_INF)
    e = jnp.where(cmask, jnp.exp2(scm - jnp.max(scm, axis=-1, keepdims=True)), 0.0)
    pc = e / jnp.maximum(jnp.sum(e, axis=-1, keepdims=True), TINY)
    o_cmp = _dot(pc.astype(BF16), vcc)

    imp16 = _split_dot(pc, ov_ref[...])
    hi = imp16.astype(BF16)
    lo = (imp16 - hi.astype(F32)).astype(BF16)
    imp = _dot(gs_ref[...], hi) + _dot(gs_ref[...], lo)
    n_blk = past // SEL_BLOCK + 1
    cur = past // SEL_BLOCK
    blk = lax.broadcasted_iota(jnp.int32, (N_HEADS, LANES), 1)
    valid = blk < n_blk
    forced = (blk == 0) | (blk == cur) | (blk == cur - 1)
    score = jnp.where(valid, jnp.where(forced, FORCE_SCORE, imp), NEG_INF)
    sel = jnp.where(_topk_select(score, valid, n_blk), 1.0, 0.0)

    pcol = lax.broadcasted_iota(jnp.int32, (1, page_rows), 1)
    head_ok = (pcol & (N_KV_HEADS - 1)) == hrow
    upper = pcol >= page_rows // 2
    for p in range(n_pages):
        s = _dot_nt(q_rot, sk[p][0].reshape(page_rows, HEAD_DIM).astype(BF16))
        chosen = jnp.where(upper, sel[:, 2 * p + 1:2 * p + 2], sel[:, 2 * p:2 * p + 1])
        s_ref[:, p * page_rows:(p + 1) * page_rows] = jnp.where(head_ok & (chosen > 0.5), s, NEG_INF)
    s_new = jnp.sum(q_rot32 * kse_ref[0], axis=-1, keepdims=True)
    p_old, p_new, denom = softmax_parts(s_ref[...], s_new)
    acc = p_new * vse_ref[0]
    for p in range(n_pages):
        acc = acc + _dot(p_old[:, p * page_rows:(p + 1) * page_rows].astype(BF16),
                         sv[p][0].reshape(page_rows, HEAD_DIM).astype(BF16))
    o_sel = acc / denom

    n_win = wink_ref.shape[1]
    wcol = lax.broadcasted_iota(jnp.int32, (1, n_win), 1)
    wmask = ((wcol & (N_KV_HEADS - 1)) == hrow) & (_div_pow2(wcol, N_KV_HEADS) >= n_win // N_KV_HEADS + 1 - WINDOW)
    sw = jnp.where(wmask, _dot_nt(q_rot, wink_ref[0].astype(BF16)), NEG_INF)
    sw_new = jnp.sum(q_rot32 * kwe_ref[0], axis=-1, keepdims=True)
    p_old, p_new, denom = softmax_parts(sw, sw_new)
    o_win = (p_new * vwe_ref[0] + _dot(p_old.astype(BF16), winv_ref[0].astype(BF16))) / denom

    gates = gates_ref[0]
    o = gates[:, 0:1] * o_cmp + gates[:, 1:2] * o_sel + gates[:, 2:3] * o_win
    o_ref[0] = o.astype(BF16)


def _attn_sample(page_table, caches, win_k, win_v, q_raw, q_rot, new_rows, gates, cmp_k_w, cmp_v_w, ov, gs):
    batch, n_pages = page_table.shape
    past = n_pages * PAGE_SIZE
    chunk = CMP_STRIDE * N_KV_HEADS
    page_specs = [pl.BlockSpec((1, PAGE_SIZE // CMP_STRIDE, chunk, HEAD_DIM), lambda b, pt, p=p: (pt[b, p], 0, 0, 0))
                  for p in range(n_pages)]
    per_b = lambda a: pl.BlockSpec((1,) + a.shape[1:], lambda b, pt: (b,) + (0,) * (a.ndim - 1))
    whole = lambda a: pl.BlockSpec(a.shape, lambda b, pt: (0,) * a.ndim)
    ins, specs = [], []
    for c in caches:
        ins += [c] * n_pages
        specs += page_specs
    for a in (win_k, win_v, q_raw, q_rot, *new_rows, gates):
        ins.append(a)
        specs.append(per_b(a))
    for a in (*cmp_k_w, *cmp_v_w, ov, gs):
        ins.append(a)
        specs.append(whole(a))
    return pl.pallas_call(
        functools.partial(_attn_sample_body, n_pages=n_pages, past=past),
        grid_spec=pltpu.PrefetchScalarGridSpec(
            num_scalar_prefetch=1, grid=(batch,), in_specs=specs,
            out_specs=pl.BlockSpec((1, N_HEADS, HEAD_DIM), lambda b, pt: (b, 0, 0)),
            scratch_shapes=[pltpu.VMEM((N_HEADS, past * N_KV_HEADS), F32)]),
        out_shape=jax.ShapeDtypeStruct((batch, N_HEADS, HEAD_DIM), BF16),
        compiler_params=_cparams("arbitrary"),
        name="attn_sample",
    )(page_table, *ins)


def _win_update_body(sk_ref, sv_ref, nk_ref, nv_ref, ok_ref, ov_ref):
    width = sk_ref.shape[1]
    for s_ref, n_ref, o_ref in ((sk_ref, nk_ref, ok_ref), (sv_ref, nv_ref, ov_ref)):
        o_ref[:, 0:width - 1] = s_ref[:, 1:width]
        o_ref[:, width - 1:width] = n_ref[...]


def _win_update(state_k, state_v, new_k, new_v, tb=4):
    batch, width = state_k.shape[:2]
    big = pl.BlockSpec((tb, width, N_KV_HEADS, HEAD_DIM), lambda i: (i, 0, 0, 0))
    one = pl.BlockSpec((tb, 1, N_KV_HEADS, HEAD_DIM), lambda i: (i, 0, 0, 0))
    sds = jax.ShapeDtypeStruct(state_k.shape, state_k.dtype)
    return pl.pallas_call(
        _win_update_body,
        grid=(batch // tb,),
        in_specs=[big, big, one, one],
        out_specs=[big, big],
        out_shape=[sds, sds],
        compiler_params=_cparams("parallel"),
        name="win_update",
    )(state_k, state_v, new_k, new_v)


def _rope_tables(pos):
    half = HEAD_DIM // 2
    inv_freq = jnp.power(ROPE_THETA, -jnp.arange(half, dtype=F32) / half)
    ang = pos.astype(F32)[:, None] * inv_freq[None, :]
    cos, sin = jnp.cos(ang), jnp.sin(ang)
    return jnp.concatenate([cos, cos], axis=1), jnp.concatenate([-sin, sin], axis=1)


def _overlap(n_sub, n_cols):
    n = jnp.arange(n_sub)[:, None]
    j = jnp.arange(LANES)[None, :]
    c_start, s_start = n * CMP_STRIDE, j * SEL_BLOCK
    hit = (c_start <= s_start + SEL_BLOCK - 1) & (c_start + CMP_BLOCK - 1 >= s_start)
    return (hit & (n < n_sub - 1) & (j < n_cols)).astype(BF16)


def kernel(x_prompt, x_sample, state_pool, cache_cmp_k, cache_cmp_v, cache_sel_k, cache_sel_v, state_win_k, state_win_v, page_table, norm_ffn1, norm_mix, norm_ffn2, norm_final, w_ffn_gate, w_ffn_up, w_ffn_down, w_pool, pool_scale, w_nsa_in, w_nsa_out, cmp_pe_k, cmp_w1_k, cmp_w2_k, cmp_pe_v, cmp_w1_v, cmp_w2_v):
    batch, seq, _ = x_prompt.shape
    dec_batch, dec_seq, _ = x_sample.shape
    past = page_table.shape[1] * PAGE_SIZE
    win_buf = state_win_k.shape[2]
    assert dec_seq == 1 and past % SEL_BLOCK == 0 and win_buf == WINDOW and seq % ATTN_TQ == 0
    assert norm_ffn1.shape[0] == 2 and w_pool.shape[0] == 1 and w_nsa_in.shape[0] == 1

    vec = lambda v: v.reshape(1, D_MODEL)
    wpool = w_pool[0].astype(BF16)
    n_main = Q_DIM + 6 * KV_DIM
    w_main = w_nsa_in[0].astype(BF16)
    wgt = w_nsa_in[0][:, n_main:].reshape(D_MODEL, N_KV_HEADS, GQA_GROUP, N_BRANCHES)
    wgt = jnp.transpose(wgt, (0, 1, 3, 2)).reshape(D_MODEL, N_HEADS * N_BRANCHES)
    wgt = jnp.pad(wgt, ((0, 0), (0, LANES - N_HEADS * N_BRANCHES))).astype(BF16)
    w_out = w_nsa_out[0].astype(BF16)
    cmp_k_w = _compress_weights(cmp_pe_k[0], cmp_w1_k[0], cmp_w2_k[0])
    cmp_v_w = _compress_weights(cmp_pe_v[0], cmp_w1_v[0], cmp_w2_v[0])
    ffn = lambda xp, xs, stages, gf=None: _ffn(xp, xs, stages, w_ffn_gate, w_ffn_up, w_ffn_down, g_final=gf)

    xp = x_prompt.reshape(batch * seq, D_MODEL)
    xs = x_sample.reshape(dec_batch, D_MODEL)
    xp, xs = ffn(xp, xs, [(0, 0, norm_ffn1[0])])

    xp, pool_p = _pool_prompt(xp, vec(norm_mix[0]), wpool, vec(pool_scale[0]), batch, seq)
    xs, h_s = _pool_sample(xs, jnp.transpose(state_pool[0], (1, 0, 2)), vec(norm_mix[0]), wpool, vec(pool_scale[0]))
    pool_s = jnp.concatenate([state_pool[0][:, 1:], h_s[:, None]], axis=1)
    xp, xs = ffn(xp, xs, [(0, 1, norm_ffn2[0]), (1, 0, norm_ffn1[1])])

    tm = 256
    cos_p, sin_p = _rope_tables(jnp.arange(seq))
    (q_raw, q_rot, kc_p, vc_p, ks_p, vs_p, kw_p, vw_p, ks_hm, vs_hm, kw_hm, vw_hm, gates_p) = _nsa_project(
        xp, vec(norm_mix[1]), w_main, wgt, cos_p, sin_p, seq // tm, tm)
    cos_s, sin_s = _rope_tables(jnp.full((dec_batch,), past))
    (q_raw_s, q_rot_s, kc_s, vc_s, ks_s, vs_s, kw_s, vw_s, _, _, _, _, gates_s) = _nsa_project(
        xs, vec(norm_mix[1]), w_main, wgt, cos_s, sin_s, 1, dec_batch)

    kcc, vcc = _compress_prompt(kc_p, vc_p, cmp_k_w, cmp_v_w, batch, seq)
    n_sub = seq // CMP_STRIDE
    expand = (jnp.arange(seq)[:, None] // SEL_BLOCK == jnp.arange(LANES)[None, :]).astype(BF16)
    ovt = jnp.transpose(_overlap(n_sub, seq // SEL_BLOCK))[:seq // SEL_BLOCK]
    new_row = lambda a: a.reshape(dec_batch, 1, N_KV_HEADS, HEAD_DIM)
    win = (state_win_k[0], state_win_v[0], new_row(kw_s), new_row(vw_s))
    attn_steps = batch * N_KV_HEADS * (seq // ATTN_TQ)
    if dec_batch % attn_steps == 0:
        o_p, win_k_s, win_v_s = _attn_prompt(q_raw, q_rot, kcc, vcc, ks_hm, vs_hm, kw_hm, vw_hm, gates_p, ovt, expand,
                                             batch, seq, win=win)
    else:
        (o_p,) = _attn_prompt(q_raw, q_rot, kcc, vcc, ks_hm, vs_hm, kw_hm, vw_hm, gates_p, ovt, expand, batch, seq)
        win_k_s, win_v_s = _win_update(*win)
    xp = _matmul_residual(o_p, w_out, xp, 2 * tm)

    per_head = lambda a: jnp.repeat(a.reshape(dec_batch, N_KV_HEADS, HEAD_DIM), GQA_GROUP, axis=1)
    g3 = gates_s[:, :, :N_BRANCHES * GQA_GROUP].reshape(N_KV_HEADS, dec_batch, N_BRANCHES, GQA_GROUP)
    g3 = jnp.transpose(g3, (1, 0, 3, 2)).reshape(dec_batch, N_HEADS, N_BRANCHES)
    chunk = CMP_STRIDE * N_KV_HEADS
    paged = lambda c: c[0].reshape(c.shape[1], PAGE_SIZE // CMP_STRIDE, chunk, HEAD_DIM)
    win_rows = lambda s: s[0].reshape(dec_batch, win_buf * N_KV_HEADS, HEAD_DIM)
    gs = (jnp.arange(N_HEADS)[:, None] // GQA_GROUP == jnp.arange(N_HEADS)[None, :] // GQA_GROUP).astype(BF16)
    n_sub_s = past // CMP_STRIDE
    ov_s = jnp.repeat(_overlap(n_sub_s, past // SEL_BLOCK + 1), N_KV_HEADS, axis=0)
    o_s = _attn_sample(page_table, [paged(c) for c in (cache_cmp_k, cache_cmp_v, cache_sel_k, cache_sel_v)],
                       win_rows(state_win_k), win_rows(state_win_v),
                       q_raw_s.reshape(dec_batch, N_HEADS, HEAD_DIM), q_rot_s.reshape(dec_batch, N_HEADS, HEAD_DIM),
                       [per_head(a) for a in (ks_s, vs_s, kw_s, vw_s)], g3, cmp_k_w, cmp_v_w, ov_s, gs)
    xs = _matmul_residual(o_s.reshape(dec_batch, Q_DIM), w_out, xs, dec_batch)

    yp, ys = ffn(xp, xs, [(1, 1, norm_ffn2[1])], vec(norm_final))
    y_prompt = yp.reshape(batch, seq, D_MODEL)
    y_sample = ys.reshape(dec_batch, 1, D_MODEL)

    kv5 = lambda a, b, s: a.reshape(1, b, s, N_KV_HEADS, HEAD_DIM)
    keep = min(WINDOW, seq)
    return (y_prompt, y_sample, pool_p[None], pool_s[None],
            kv5(kc_p, batch, seq), kv5(kc_s, dec_batch, 1), kv5(vc_p, batch, seq), kv5(vc_s, dec_batch, 1),
            kv5(ks_p, batch, seq), kv5(ks_s, dec_batch, 1), kv5(vs_p, batch, seq), kv5(vs_s, dec_batch, 1),
            kv5(kw_p, batch, seq)[:, :, seq - keep:], win_k_s[None],
            kv5(vw_p, batch, seq)[:, :, seq - keep:], win_v_s[None])
```

```python
import functools

import jax
import jax.numpy as jnp
from jax import lax
from jax.experimental import pallas as pl
from jax.experimental.pallas import tpu as pltpu

F32 = jnp.float32
BF16 = jnp.bfloat16

D_MODEL = 2048
D_FF = 5632
POOL_WINDOWS = (2, 4, 8, 16)
POOL_GROUP_DIM = D_MODEL // len(POOL_WINDOWS)
POOL_STATE_LEN = max(POOL_WINDOWS) - 1
N_HEADS = 16
HEAD_DIM = 128
N_KV_HEADS = 4
GQA_GROUP = N_HEADS // N_KV_HEADS
Q_DIM = N_HEADS * HEAD_DIM
KV_DIM = N_KV_HEADS * HEAD_DIM
N_BRANCHES = 3
CMP_BLOCK = 32
CMP_STRIDE = 16
SEL_BLOCK = 64
SEL_TOPK = 16
ATTN_TQ = 256
WINDOW = 512
PAGE_SIZE = 128
SCALE = HEAD_DIM ** -0.5
LOG2E = 1.4426950408889634
ROPE_THETA = 10000.0
RMS_EPS = 1e-6
NEG_INF = -1e30
FORCE_SCORE = 1e9
TINY = 1e-30

LANES = 128
SUBLANES = 8
VMEM_LIMIT = 56 * 1024 * 1024


def _cparams(*sem):
    return pltpu.CompilerParams(dimension_semantics=sem, vmem_limit_bytes=VMEM_LIMIT)


def _rms(x, g):
    y = x * lax.rsqrt(jnp.mean(x * x, axis=-1, keepdims=True) + RMS_EPS)
    return y * g


def _dot(a, b):
    return jnp.dot(a, b, preferred_element_type=F32)


def _div_pow2(x, n):
    return lax.shift_right_logical(x, n.bit_length() - 1)


def _dot_nt(a, b):
    return lax.dot_general(a, b, (((1,), (1,)), ((), ())), preferred_element_type=F32)


def _ffn_body(xp_ref, xs_ref, g_ref, wg_ref, wu_ref, wd_ref, *rest, final_norm, n_stage, nf, tm, ts):
    if final_norm:
        gf_ref, op_ref, os_ref, xn_ref = rest
    else:
        op_ref, os_ref, xn_ref = rest
    i, f = pl.program_id(0), pl.program_id(1)
    last = pl.num_programs(1) - 1

    for s in range(n_stage):
        @pl.when(f == s * nf)
        def _(s=s):
            x = xp_ref[...] if s == 0 else op_ref[...]
            xn_ref[0:tm] = _rms(x, g_ref[s:s + 1, :]).astype(BF16)
            if s == 0:
                op_ref[...] = x

        @pl.when((f == s * nf) & (i == 0))
        def _(s=s):
            x = xs_ref[...] if s == 0 else os_ref[...]
            xn_ref[tm:tm + ts] = _rms(x, g_ref[s:s + 1, :]).astype(BF16)
            if s == 0:
                os_ref[...] = x

    def half_step(rows):
        xn = xn_ref[0:rows]
        gate = _dot(xn, wg_ref[...].astype(BF16))
        up = _dot(xn, wu_ref[...].astype(BF16))
        h = (gate * jax.nn.sigmoid(gate)) * up
        return _dot((0.5 * h).astype(BF16), wd_ref[...].astype(BF16))

    @pl.when(i == 0)
    def _():
        y = half_step(tm + ts)
        op_ref[...] += y[0:tm]
        os_ref[...] += y[tm:tm + ts]

    @pl.when(i > 0)
    def _():
        op_ref[...] += half_step(tm)

    if final_norm:
        @pl.when(f == last)
        def _():
            op_ref[...] = _rms(op_ref[...], gf_ref[...])

        @pl.when((f == last) & (i == 0))
        def _():
            os_ref[...] = _rms(os_ref[...], gf_ref[...])


def _ffn(xp, xs, stages, wg, wu, wd, g_final=None, tm=1024, tf=256):
    m, ts = xp.shape[0], xs.shape[0]
    nf = D_FF // tf
    n_stage = len(stages)
    assert n_stage in (1, 2)
    final_norm = g_final is not None
    g = jnp.stack([s[2] for s in stages])
    row = lambda i, f: (i, 0)
    const = lambda i, f: (0, 0)
    pick = lambda f, k: stages[0][k] if n_stage == 1 else jnp.where(f >= nf, stages[1][k], stages[0][k])
    in_specs = [
        pl.BlockSpec((tm, D_MODEL), row),
        pl.BlockSpec((ts, D_MODEL), const),
        pl.BlockSpec((n_stage, D_MODEL), const),
        pl.BlockSpec((None, None, D_MODEL, tf), lambda i, f: (pick(f, 0), pick(f, 1), 0, f % nf)),
        pl.BlockSpec((None, None, D_MODEL, tf), lambda i, f: (pick(f, 0), pick(f, 1), 0, f % nf)),
        pl.BlockSpec((None, None, tf, D_MODEL), lambda i, f: (pick(f, 0), pick(f, 1), f % nf, 0)),
    ]
    args = [xp, xs, g, wg, wu, wd]
    if final_norm:
        in_specs.append(pl.BlockSpec((1, D_MODEL), const))
        args.append(g_final)
    return pl.pallas_call(
        functools.partial(_ffn_body, final_norm=final_norm, n_stage=n_stage, nf=nf, tm=tm, ts=ts),
        grid=(m // tm, n_stage * nf),
        in_specs=in_specs,
        out_specs=[pl.BlockSpec((tm, D_MODEL), row), pl.BlockSpec((ts, D_MODEL), const)],
        out_shape=[jax.ShapeDtypeStruct((m, D_MODEL), F32), jax.ShapeDtypeStruct((ts, D_MODEL), F32)],
        scratch_shapes=[pltpu.VMEM((tm + ts, D_MODEL), BF16)],
        compiler_params=_cparams("arbitrary", "arbitrary"),
        name="ffn",
    )(*args)


def _pool_groups(h_cols, x, w_ref, sc_ref, o_ref, window_sum, inv_cnt):
    for gi, w in enumerate(POOL_WINDOWS):
        c0 = gi * POOL_GROUP_DIM
        cols = slice(c0, c0 + POOL_GROUP_DIM)
        hc = h_cols(cols)
        d = window_sum(w, cols, hc) * inv_cnt(w) - hc
        y = _dot(d.astype(BF16), w_ref[gi])
        o_ref[:, cols] = x[:, cols] + y * sc_ref[:, cols]


def _pool_prompt_body(x_ref, g_ref, w_ref, sc_ref, o_ref, st_ref, hbuf, *, ts):
    s = pl.program_id(1)
    halo = 2 * SUBLANES

    @pl.when(s == 0)
    def _():
        hbuf[0:halo, :] = jnp.zeros((halo, D_MODEL), F32)

    @pl.when(s > 0)
    def _():
        hbuf[0:halo, :] = hbuf[ts:ts + halo, :]

    x = x_ref[...]
    hbuf[halo:halo + ts, :] = _rms(x, g_ref[...])
    pos = s * ts + lax.broadcasted_iota(jnp.int32, (ts, 1), 0)

    def window_sum(w, cols, hc):
        acc = hc
        for k in range(1, w):
            acc = acc + hbuf[halo - k:halo - k + ts, cols]
        return acc

    _pool_groups(lambda cols: hbuf[halo:halo + ts, cols], x, w_ref, sc_ref, o_ref, window_sum,
                 lambda w: 1.0 / jnp.minimum(w, pos + 1).astype(F32))

    @pl.when(s == pl.num_programs(1) - 1)
    def _():
        st_ref[0] = hbuf[halo + ts - POOL_STATE_LEN:halo + ts, :]


def _pool_prompt(x, g, w_pool, scale, batch, seq, ts=512):
    nt = seq // ts
    return pl.pallas_call(
        functools.partial(_pool_prompt_body, ts=ts),
        grid=(batch, nt),
        in_specs=[
            pl.BlockSpec((ts, D_MODEL), lambda b, s: (b * nt + s, 0)),
            pl.BlockSpec((1, D_MODEL), lambda b, s: (0, 0)),
            pl.BlockSpec((len(POOL_WINDOWS), POOL_GROUP_DIM, POOL_GROUP_DIM), lambda b, s: (0, 0, 0)),
            pl.BlockSpec((1, D_MODEL), lambda b, s: (0, 0)),
        ],
        out_specs=[
            pl.BlockSpec((ts, D_MODEL), lambda b, s: (b * nt + s, 0)),
            pl.BlockSpec((1, POOL_STATE_LEN, D_MODEL), lambda b, s: (b, 0, 0)),
        ],
        out_shape=[
            jax.ShapeDtypeStruct((batch * seq, D_MODEL), F32),
            jax.ShapeDtypeStruct((batch, POOL_STATE_LEN, D_MODEL), F32),
        ],
        scratch_shapes=[pltpu.VMEM((ts + 2 * SUBLANES, D_MODEL), F32)],
        compiler_params=_cparams("arbitrary", "arbitrary"),
        name="pool_prompt",
    )(x, g, w_pool, scale)


def _pool_sample_body(x_ref, st_ref, g_ref, w_ref, sc_ref, o_ref, h_ref):
    x = x_ref[...]
    h_ref[...] = _rms(x, g_ref[...])

    def window_sum(w, cols, hc):
        acc = hc
        for k in range(1, w):
            acc = acc + st_ref[POOL_STATE_LEN - k, :, cols]
        return acc

    _pool_groups(lambda cols: h_ref[:, cols], x, w_ref, sc_ref, o_ref, window_sum, lambda w: 1.0 / w)


def _pool_sample(x, state_t, g, w_pool, scale, tb=32):
    b = x.shape[0]
    return pl.pallas_call(
        _pool_sample_body,
        grid=(b // tb,),
        in_specs=[
            pl.BlockSpec((tb, D_MODEL), lambda i: (i, 0)),
            pl.BlockSpec((POOL_STATE_LEN, tb, D_MODEL), lambda i: (0, i, 0)),
            pl.BlockSpec((1, D_MODEL), lambda i: (0, 0)),
            pl.BlockSpec((len(POOL_WINDOWS), POOL_GROUP_DIM, POOL_GROUP_DIM), lambda i: (0, 0, 0)),
            pl.BlockSpec((1, D_MODEL), lambda i: (0, 0)),
        ],
        out_specs=[pl.BlockSpec((tb, D_MODEL), lambda i: (i, 0)), pl.BlockSpec((tb, D_MODEL), lambda i: (i, 0))],
        out_shape=[jax.ShapeDtypeStruct((b, D_MODEL), F32), jax.ShapeDtypeStruct((b, D_MODEL), F32)],
        compiler_params=_cparams("parallel"),
        name="pool_sample",
    )(x, state_t, g, w_pool, scale)


def _proj_body(x_ref, g_ref, w_ref, wg_ref, cos_ref, sin_ref,
               qraw_ref, qrot_ref, kc_ref, vc_ref, ks_ref, vs_ref, kw_ref, vw_ref,
               kshm_ref, vshm_ref, kwhm_ref, vwhm_ref, gates_ref, *, tm):
    xn = _rms(x_ref[...], g_ref[...]).astype(BF16)
    gall = jax.nn.sigmoid(_dot(xn, wg_ref[...]))
    per_head = N_BRANCHES * GQA_GROUP
    for kvh in range(N_KV_HEADS):
        gates_ref[kvh] = gall if kvh == 0 else pltpu.roll(gall, LANES - kvh * per_head, 1)

    cos, sin = cos_ref[...], sin_ref[...]
    rope = lambda s: s * cos + pltpu.roll(s, HEAD_DIM // 2, 1) * sin
    tile = lambda j: _dot(xn, w_ref[:, j * KV_DIM:(j + 1) * KV_DIM])
    heads = lambda a: [a[:, h * HEAD_DIM:(h + 1) * HEAD_DIM] for h in range(N_KV_HEADS)]

    nq = Q_DIM // KV_DIM
    for j in range(nq):
        zs = tile(j) * (SCALE * LOG2E)
        qraw_ref[:, j * KV_DIM:(j + 1) * KV_DIM] = zs.astype(BF16)
        for h, s in enumerate(heads(zs)):
            c0 = j * KV_DIM + h * HEAD_DIM
            qrot_ref[:, c0:c0 + HEAD_DIM] = rope(s).astype(BF16)

    kinds = [(kc_ref, None, False), (vc_ref, None, False), (ks_ref, kshm_ref, True), (vs_ref, vshm_ref, False),
             (kw_ref, kwhm_ref, True), (vw_ref, vwhm_ref, False)]
    for off, (ref, hm_ref, rot) in enumerate(kinds):
        for h, s in enumerate(heads(tile(nq + off))):
            s = rope(s) if rot else s
            ref[pl.ds(h, tm, stride=N_KV_HEADS), :] = s
            if hm_ref is not None:
                hm_ref[h] = s.astype(BF16)


def _nsa_project(x, g, w_main, w_gates, cos, sin, rope_period_tiles, tm):
    m = x.shape[0]
    row = lambda i: (i, 0)
    const = lambda i: (0, 0)
    kv_sds = jax.ShapeDtypeStruct((m * N_KV_HEADS, HEAD_DIM), F32)
    hm_sds = jax.ShapeDtypeStruct((N_KV_HEADS, m, HEAD_DIM), BF16)
    q_sds = jax.ShapeDtypeStruct((m, Q_DIM), BF16)
    kv_spec = pl.BlockSpec((tm * N_KV_HEADS, HEAD_DIM), row)
    hm_spec = pl.BlockSpec((N_KV_HEADS, tm, HEAD_DIM), lambda i: (0, i, 0))
    q_spec = pl.BlockSpec((tm, Q_DIM), row)
    return pl.pallas_call(
        functools.partial(_proj_body, tm=tm),
        grid=(m // tm,),
        in_specs=[
            pl.BlockSpec((tm, D_MODEL), row),
            pl.BlockSpec((1, D_MODEL), const),
            pl.BlockSpec(w_main.shape, const, pipeline_mode=pl.Buffered(1)),
            pl.BlockSpec(w_gates.shape, const, pipeline_mode=pl.Buffered(1)),
            pl.BlockSpec((tm, HEAD_DIM), lambda i: (i % rope_period_tiles, 0)),
            pl.BlockSpec((tm, HEAD_DIM), lambda i: (i % rope_period_tiles, 0)),
        ],
        out_specs=[q_spec, q_spec] + [kv_spec] * 6 + [hm_spec] * 4
        + [pl.BlockSpec((N_KV_HEADS, tm, LANES), lambda i: (0, i, 0))],
        out_shape=[q_sds, q_sds] + [kv_sds] * 6 + [hm_sds] * 4
        + [jax.ShapeDtypeStruct((N_KV_HEADS, m, LANES), F32)],
        compiler_params=_cparams("parallel"),
        name="nsa_project",
    )(x, g, w_main, w_gates, cos, sin)


N_PAIR = CMP_STRIDE // 2


def _compress(streams, n_sub):
    n_rows = n_sub * N_KV_HEADS
    low = lax.broadcasted_iota(jnp.int32, (1, SUBLANES, 1), 1) < N_KV_HEADS

    def repack(v):
        pairs = lambda a: a.reshape(n_sub // 2, 2, SUBLANES, HEAD_DIM)
        v4, r4 = pairs(v), pairs(pltpu.roll(v, N_KV_HEADS, 0))
        ev = jnp.where(low, v4[:, 0], r4[:, 1]).reshape(n_rows, HEAD_DIM)
        od = jnp.where(low, r4[:, 1], v4[:, 1]).reshape(n_rows, HEAD_DIM)
        return jnp.concatenate([ev, od], axis=1).astype(BF16)

    accs = [jnp.zeros((n_rows, 2 * HEAD_DIM), F32) for _ in streams]
    biases = [jnp.zeros((SUBLANES, 2 * HEAD_DIM), F32) for _ in streams]
    for k in range(N_PAIR):
        for i, (get_rows, wpair_ref, pepair_ref, _) in enumerate(streams):
            accs[i] = accs[i] + _dot(repack(get_rows(k)), wpair_ref[k])
            biases[i] = biases[i] + _dot(pepair_ref[k].astype(BF16), wpair_ref[k])
    outs = []
    for acc, bias, (_, _, _, w2_ref) in zip(accs, biases, streams):
        first = acc[:, :HEAD_DIM] + bias[0:1, :HEAD_DIM]
        second = acc[:, HEAD_DIM:] + bias[1:2, HEAD_DIM:]
        pre = first + pltpu.roll(second, n_rows - N_KV_HEADS, 0)
        outs.append(_dot(jax.nn.gelu(pre, approximate=True).astype(BF16), w2_ref[...]))
    return outs


def _compress_body(xk_ref, xv_ref, wpk_ref, pek_ref, w2k_ref, wpv_ref, pev_ref, w2v_ref, ok_ref, ov_ref, res_ref, *,
                   n_sub):
    rows = lambda x_ref: lambda k: x_ref[:, k * SUBLANES:(k + 1) * SUBLANES, :].reshape(n_sub * SUBLANES, HEAD_DIM)
    outs = _compress([(rows(xk_ref), wpk_ref, pek_ref, w2k_ref), (rows(xv_ref), wpv_ref, pev_ref, w2v_ref)], n_sub)
    for res, o_ref in zip(outs, (ok_ref, ov_ref)):
        res_ref[...] = res
        for h in range(N_KV_HEADS):
            o_ref[0, h] = res_ref[pl.ds(h, n_sub, stride=N_KV_HEADS), :].astype(BF16)


def _compress_prompt(rows_k, rows_v, cmp_k_w, cmp_v_w, batch, seq):
    n_sub = seq // CMP_STRIDE
    chunk = CMP_STRIDE * N_KV_HEADS
    x3 = lambda rows: rows.reshape(batch * n_sub, chunk, HEAD_DIM)
    xspec = pl.BlockSpec((n_sub, chunk, HEAD_DIM), lambda b: (b, 0, 0))
    wspecs = [pl.BlockSpec((N_PAIR, 2 * HEAD_DIM, 2 * HEAD_DIM), lambda b: (0, 0, 0)),
              pl.BlockSpec((N_PAIR, SUBLANES, 2 * HEAD_DIM), lambda b: (0, 0, 0)),
              pl.BlockSpec((HEAD_DIM, HEAD_DIM), lambda b: (0, 0))]
    ospec = pl.BlockSpec((1, N_KV_HEADS, n_sub, HEAD_DIM), lambda b: (b, 0, 0, 0))
    osds = jax.ShapeDtypeStruct((batch, N_KV_HEADS, n_sub, HEAD_DIM), BF16)
    return pl.pallas_call(
        functools.partial(_compress_body, n_sub=n_sub),
        grid=(batch,),
        in_specs=[xspec, xspec] + wspecs + wspecs,
        out_specs=[ospec, ospec],
        out_shape=[osds, osds],
        scratch_shapes=[pltpu.VMEM((n_sub * N_KV_HEADS, HEAD_DIM), F32)],
        compiler_params=_cparams("parallel"),
        name="compress_prompt",
    )(x3(rows_k), x3(rows_v), *cmp_k_w, *cmp_v_w)


def _compress_weights(pe, w1, w2):
    w1r = w1.reshape(2, N_PAIR, 2, HEAD_DIM, HEAD_DIM)
    wpair = jnp.transpose(w1r, (1, 2, 3, 0, 4)).reshape(N_PAIR, 2 * HEAD_DIM, 2 * HEAD_DIM).astype(BF16)
    per = pe.reshape(2, N_PAIR, 2 * HEAD_DIM)
    pepair = jnp.zeros((N_PAIR, SUBLANES, 2 * HEAD_DIM), F32).at[:, 0:2, :].set(jnp.transpose(per, (1, 0, 2)))
    return wpair, pepair, w2.astype(BF16)


def _topk_select(score, valid, n_blk):
    lane = lax.broadcasted_iota(jnp.int32, score.shape, 1)
    rank = jnp.zeros(score.shape, F32)
    for i in range(n_blk):
        ci = score[:, i:i + 1]
        beats = (ci > score) | ((ci == score) & (lane > i))
        rank = rank + jnp.where(beats, 1.0, 0.0)
    return (rank < SEL_TOPK) & valid


def _split_dot(p, w):
    hi = p.astype(BF16)
    lo = (p - hi.astype(F32)).astype(BF16)
    return _dot(hi, w) + _dot(lo, w)


def _attn_prompt_tile(q, qraw_ref, qrot_ref, kcc_ref, vcc_ref, ks_ref, vs_ref, kw_ref, vw_ref, gates_ref, ovt_ref,
                      ext_ref, o_ref, *, tq, seq):
    q0 = q * tq
    kt = tq
    qpos = q0 + lax.broadcasted_iota(jnp.int32, (tq, 1), 0)
    qpos4 = jnp.concatenate([qpos] * GQA_GROUP, axis=0)
    stack = lambda ref: jnp.concatenate([ref[:, g * HEAD_DIM:(g + 1) * HEAD_DIM] for g in range(GQA_GROUP)], axis=0)
    q_raw = stack(qraw_ref)
    q_rot = stack(qrot_ref)

    n_cmp = seq // CMP_STRIDE
    sc = _dot_nt(q_raw, kcc_ref[0, 0])
    c_end = lax.broadcasted_iota(jnp.int32, (1, n_cmp), 1) * CMP_STRIDE + (CMP_BLOCK - 1)
    cmask = c_end <= qpos4
    scm = jnp.where(cmask, sc, NEG_INF)
    e = jnp.where(cmask, jnp.exp2(scm - jnp.max(scm, axis=-1, keepdims=True)), 0.0)
    pc = e / jnp.maximum(jnp.sum(e, axis=-1, keepdims=True), TINY)
    o_cmp = _dot(pc.astype(BF16), vcc_ref[0, 0])

    ranked = q0 + tq > SEL_TOPK * SEL_BLOCK
    if ranked:
        n_blk = seq // SEL_BLOCK
        psum = pc[0:tq] + pc[tq:2 * tq] + pc[2 * tq:3 * tq] + pc[3 * tq:4 * tq]
        hi = psum.astype(BF16)
        lo = (psum - hi.astype(F32)).astype(BF16)
        imp = _dot_nt(ovt_ref[...], hi) + _dot_nt(ovt_ref[...], lo)
        blk = lax.broadcasted_iota(jnp.int32, (n_blk, 1), 0)
        qlane = q0 + lax.broadcasted_iota(jnp.int32, (1, tq), 1)
        cur = _div_pow2(qlane, SEL_BLOCK)
        valid = blk * SEL_BLOCK <= qlane
        forced = (blk == 0) | (blk == cur) | (blk == cur - 1)
        score = jnp.where(valid, jnp.where(forced, FORCE_SCORE, imp), NEG_INF)
        rank = jnp.zeros((n_blk, tq), F32)
        for i in range((q0 + tq) // SEL_BLOCK):
            ci = score[i:i + 1, :]
            beats = (ci > score) | ((ci == score) & (blk > i))
            rank = rank + jnp.where(beats, 1.0, 0.0)
        sel_t = jnp.where((rank < SEL_TOPK) & valid, 1.0, 0.0)
        sel = jnp.concatenate([sel_t, jnp.zeros((LANES - n_blk, tq), F32)], axis=0).T
        sel_neg = jnp.where(sel > 0.5, 0.0, NEG_INF).astype(BF16)
        q_sel = jnp.concatenate([q_rot, jnp.concatenate([sel_neg] * GQA_GROUP, axis=0)], axis=1)

    def attend_pair(first, second):
        def by_halves(dot, lhs, rhs):
            half = lhs.shape[0] // 2
            return jnp.concatenate([dot(lhs[:half], rhs), dot(lhs[half:], rhs)], axis=0)

        def scores(k_ref, chunks, by_block):
            for c, bias in chunks:
                k = k_ref[0, c * kt:(c + 1) * kt, :]
                if by_block:
                    s = by_halves(_dot_nt, q_sel, jnp.concatenate([k, ext_ref[c * kt:(c + 1) * kt, :]], axis=1))
                else:
                    s = _dot_nt(q_rot, k)
                yield s if bias is None else s + jnp.concatenate([bias] * GQA_GROUP, axis=0)

        def row_max(ss):
            top = ss[0]
            for s in ss[1:]:
                top = jnp.maximum(top, s)
            return jnp.max(top, axis=-1, keepdims=True)

        def probs(ss, m):
            for s in ss:
                yield jnp.exp2(s - m)

        ones = jnp.ones((kt, HEAD_DIM), BF16)

        def weighted(v_ref, chunks, ps):
            for (c, _), p in zip(chunks, ps):
                yield _dot(p.astype(BF16), jnp.concatenate([v_ref[0, c * kt:(c + 1) * kt, :], ones], axis=1))

        def alternate(lead, follow):
            a, b = [], []
            for item in lead:
                a.append(item)
                nxt = next(follow, None)
                if nxt is not None:
                    b.append(nxt)
            b.extend(follow)
            return a, b

        (k1, v1, ch1), (k2, v2, ch2) = first, second
        s1 = list(scores(k1, ch1, ranked))
        m1 = row_max(s1)
        p1, s2 = alternate(probs(s1, m1), scores(k2, ch2, False))
        m2 = row_max(s2)
        pv1, p2 = alternate(weighted(v1, ch1, p1), probs(s2, m2))
        pv2 = list(weighted(v2, ch2, p2))
        def out(pv):
            tot = sum(pv[1:], pv[0])
            return tot[:, :HEAD_DIM] / tot[:, HEAD_DIM:]

        return out(pv1), out(pv2)

    def mask_bias(c, ok):
        kpos = c * kt + lax.broadcasted_iota(jnp.int32, (1, kt), 1)
        return jnp.where(ok(kpos), 0.0, NEG_INF)

    sel_chunks = [(c, mask_bias(c, lambda kpos: kpos <= qpos) if c == q else None) for c in range(q + 1)]

    win_chunks = []
    for c in range(q + 1):
        if c * kt + kt - 1 <= q0 - WINDOW:
            continue
        whole = c * kt + kt - 1 <= q0 and c * kt > q0 + tq - 1 - WINDOW
        win_chunks.append((c, None if whole else mask_bias(c, lambda kpos: (kpos <= qpos) & (kpos > qpos - WINDOW))))
    o_sel, o_win = attend_pair((ks_ref, vs_ref, sel_chunks), (kw_ref, vw_ref, win_chunks))

    gates = gates_ref[0]
    for g in range(GQA_GROUP):
        r = slice(g * tq, (g + 1) * tq)
        gate = lambda br: gates[:, br * GQA_GROUP + g:br * GQA_GROUP + g + 1]
        o = gate(0) * o_cmp[r] + gate(1) * o_sel[r] + gate(2) * o_win[r]
        o_ref[:, g * HEAD_DIM:(g + 1) * HEAD_DIM] = o.astype(BF16)


def _attn_prompt_body(*refs, tq, seq, with_win):
    if with_win:
        (*refs, sk_ref, sv_ref, nk_ref, nv_ref, o_ref, ok_ref, ov_ref) = refs
        _win_update_body(sk_ref, sv_ref, nk_ref, nv_ref, ok_ref, ov_ref)
        refs = (*refs, o_ref)
    qi = pl.program_id(2)
    for q in range(seq // tq):
        @pl.when(qi == q)
        def _(q=q):
            _attn_prompt_tile(q, *refs, tq=tq, seq=seq)


def _attn_prompt(q_raw, q_rot, kcc, vcc, ks_hm, vs_hm, kw_hm, vw_hm, gates, ovt, ex, batch, seq, win=None, tq=ATTN_TQ):
    nq = seq // tq
    n_cmp = seq // CMP_STRIDE
    qspec = pl.BlockSpec((tq, KV_DIM), lambda b, h, i: (b * nq + i, h))
    cspec = pl.BlockSpec((1, 1, n_cmp, HEAD_DIM), lambda b, h, i: (b, h, 0, 0))
    kvspec = pl.BlockSpec((1, seq, HEAD_DIM), lambda b, h, i: (h, b, 0))
    in_specs = [qspec, qspec, cspec, cspec, kvspec, kvspec, kvspec, kvspec,
                pl.BlockSpec((1, tq, LANES), lambda b, h, i: (h, b * nq + i, 0)),
                pl.BlockSpec(ovt.shape, lambda b, h, i: (0, 0)),
                pl.BlockSpec((seq, LANES), lambda b, h, i: (0, 0))]
    args = [q_raw, q_rot, kcc, vcc, ks_hm, vs_hm, kw_hm, vw_hm, gates, ovt, ex]
    out_specs = [qspec]
    out_shape = [jax.ShapeDtypeStruct((batch * seq, Q_DIM), BF16)]
    if win is not None:
        state_k = win[0]
        tb = state_k.shape[0] // (batch * N_KV_HEADS * nq)
        step = lambda b, h, i: ((b * N_KV_HEADS + h) * nq + i, 0, 0, 0)
        big = pl.BlockSpec((tb,) + state_k.shape[1:], step)
        one = pl.BlockSpec((tb, 1) + state_k.shape[2:], step)
        in_specs += [big, big, one, one]
        args += list(win)
        out_specs += [big, big]
        out_shape += [jax.ShapeDtypeStruct(state_k.shape, state_k.dtype)] * 2
    return pl.pallas_call(
        functools.partial(_attn_prompt_body, tq=tq, seq=seq, with_win=win is not None),
        grid=(batch, N_KV_HEADS, nq),
        in_specs=in_specs,
        out_specs=out_specs,
        out_shape=out_shape,
        compiler_params=_cparams("parallel", "parallel", "arbitrary"),
        name="attn_prompt",
    )(*args)


def _matmul_res_body(a_ref, w_ref, r_ref, o_ref):
    o_ref[...] = r_ref[...] + _dot(a_ref[...], w_ref[...])


def _matmul_residual(a, w, res, tm):
    m, k = a.shape
    n = w.shape[1]
    return pl.pallas_call(
        _matmul_res_body,
        grid=(m // tm,),
        in_specs=[pl.BlockSpec((tm, k), lambda i: (i, 0)), pl.BlockSpec((k, n), lambda i: (0, 0)),
                  pl.BlockSpec((tm, n), lambda i: (i, 0))],
        out_specs=pl.BlockSpec((tm, n), lambda i: (i, 0)),
        out_shape=jax.ShapeDtypeStruct((m, n), F32),
        compiler_params=_cparams("parallel"),
        name="out_proj",
    )(a, w, res)


def _attn_sample_body(pt_ref, *refs, n_pages, past):
    del pt_ref
    pages = refs[:4 * n_pages]
    (wink_ref, winv_ref, qraw_ref, qrot_ref, kse_ref, vse_ref, kwe_ref, vwe_ref, gates_ref,
     wpk_ref, pek_ref, w2k_ref, wpv_ref, pev_ref, w2v_ref, ov_ref, gs_ref, o_ref, s_ref) = refs[4 * n_pages:]
    ck, cv, sk, sv = (pages[i * n_pages:(i + 1) * n_pages] for i in range(4))
    n_sub = past // CMP_STRIDE
    n_rows = n_sub * N_KV_HEADS
    sub_pp = PAGE_SIZE // CMP_STRIDE
    page_rows = PAGE_SIZE * N_KV_HEADS

    def rows_of(pg):
        return lambda k: jnp.concatenate(
            [r[0, :, k * SUBLANES:(k + 1) * SUBLANES, :].reshape(sub_pp * SUBLANES, HEAD_DIM) for r in pg], axis=0)

    kcc, vcc = _compress([(rows_of(ck), wpk_ref, pek_ref, w2k_ref), (rows_of(cv), wpv_ref, pev_ref, w2v_ref)], n_sub)
    kcc, vcc = kcc.astype(BF16), vcc.astype(BF16)

    q_raw = qraw_ref[0]
    q_rot = qrot_ref[0]
    q_rot32 = q_rot.astype(F32)
    hrow = _div_pow2(lax.broadcasted_iota(jnp.int32, (N_HEADS, 1), 0), GQA_GROUP)

    def softmax_parts(s_old, s_new):
        m = jnp.maximum(jnp.max(s_old, axis=-1, keepdims=True), s_new)
        p_old = jnp.exp2(s_old - m)
        p_new = jnp.exp2(s_new - m)
        return p_old, p_new, jnp.sum(p_old, axis=-1, keepdims=True) + p_new

    col = lax.broadcasted_iota(jnp.int32, (1, n_rows), 1)
    cmask = ((col & (N_KV_HEADS - 1)) == hrow) & (
        _div_pow2(col, N_KV_HEADS) * CMP_STRIDE + (CMP_BLOCK - 1) <= past)
    scm = jnp.where(cmask, _dot_nt(q_raw, kcc), NEG_INF)
    e = jnp.where(cmask, jnp.exp2(scm - jnp.max(scm, axis=-1, keepdims=True)), 0.0)
    pc = e / jnp.maximum(jnp.sum(e, axis=-1, keepdims=True), TINY)
    o_cmp = _dot(pc.astype(BF16), vcc)

    imp16 = _split_dot(pc, ov_ref[...])
    hi = imp16.astype(BF16)
    lo = (imp16 - hi.astype(F32)).astype(BF16)
    imp = _dot(gs_ref[...], hi) + _dot(gs_ref[...], lo)
    n_blk = past // SEL_BLOCK + 1
    cur = past // SEL_BLOCK
    blk = lax.broadcasted_iota(jnp.int32, (N_HEADS, LANES), 1)
    valid = blk < n_blk
    forced = (blk == 0) | (blk == cur) | (blk == cur - 1)
    score = jnp.where(valid, jnp.where(forced, FORCE_SCORE, imp), NEG_INF)
    sel = jnp.where(_topk_select(score, valid, n_blk), 1.0, 0.0)

    pcol = lax.broadcasted_iota(jnp.int32, (1, page_rows), 1)
    head_ok = (pcol & (N_KV_HEADS - 1)) == hrow
    upper = pcol >= page_rows // 2
    for p in range(n_pages):
        s = _dot_nt(q_rot, sk[p][0].reshape(page_rows, HEAD_DIM).astype(BF16))
        chosen = jnp.where(upper, sel[:, 2 * p + 1:2 * p + 2], sel[:, 2 * p:2 * p + 1])
        s_ref[:, p * page_rows:(p + 1) * page_rows] = jnp.where(head_ok & (chosen > 0.5), s, NEG_INF)
    s_new = jnp.sum(q_rot32 * kse_ref[0], axis=-1, keepdims=True)
    p_old, p_new, denom = softmax_parts(s_ref[...], s_new)
    acc = p_new * vse_ref[0]
    for p in range(n_pages):
        acc = acc + _dot(p_old[:, p * page_rows:(p + 1) * page_rows].astype(BF16),
                         sv[p][0].reshape(page_rows, HEAD_DIM).astype(BF16))
    o_sel = acc / denom

    n_win = wink_ref.shape[1]
    wcol = lax.broadcasted_iota(jnp.int32, (1, n_win), 1)
    wmask = ((wcol & (N_KV_HEADS - 1)) == hrow) & (_div_pow2(wcol, N_KV_HEADS) >= n_win // N_KV_HEADS + 1 - WINDOW)
    sw = jnp.where(wmask, _dot_nt(q_rot, wink_ref[0].astype(BF16)), NEG_INF)
    sw_new = jnp.sum(q_rot32 * kwe_ref[0], axis=-1, keepdims=True)
    p_old, p_new, denom = softmax_parts(sw, sw_new)
    o_win = (p_new * vwe_ref[0] + _dot(p_old.astype(BF16), winv_ref[0].astype(BF16))) / denom

    gates = gates_ref[0]
    o = gates[:, 0:1] * o_cmp + gates[:, 1:2] * o_sel + gates[:, 2:3] * o_win
    o_ref[0] = o.astype(BF16)


def _attn_sample(page_table, caches, win_k, win_v, q_raw, q_rot, new_rows, gates, cmp_k_w, cmp_v_w, ov, gs):
    batch, n_pages = page_table.shape
    past = n_pages * PAGE_SIZE
    chunk = CMP_STRIDE * N_KV_HEADS
    page_specs = [pl.BlockSpec((1, PAGE_SIZE // CMP_STRIDE, chunk, HEAD_DIM), lambda b, pt, p=p: (pt[b, p], 0, 0, 0))
                  for p in range(n_pages)]
    per_b = lambda a: pl.BlockSpec((1,) + a.shape[1:], lambda b, pt: (b,) + (0,) * (a.ndim - 1))
    whole = lambda a: pl.BlockSpec(a.shape, lambda b, pt: (0,) * a.ndim)
    ins, specs = [], []
    for c in caches:
        ins += [c] * n_pages
        specs += page_specs
    for a in (win_k, win_v, q_raw, q_rot, *new_rows, gates):
        ins.append(a)
        specs.append(per_b(a))
    for a in (*cmp_k_w, *cmp_v_w, ov, gs):
        ins.append(a)
        specs.append(whole(a))
    return pl.pallas_call(
        functools.partial(_attn_sample_body, n_pages=n_pages, past=past),
        grid_spec=pltpu.PrefetchScalarGridSpec(
            num_scalar_prefetch=1, grid=(batch,), in_specs=specs,
            out_specs=pl.BlockSpec((1, N_HEADS, HEAD_DIM), lambda b, pt: (b, 0, 0)),
            scratch_shapes=[pltpu.VMEM((N_HEADS, past * N_KV_HEADS), F32)]),
        out_shape=jax.ShapeDtypeStruct((batch, N_HEADS, HEAD_DIM), BF16),
        compiler_params=_cparams("arbitrary"),
        name="attn_sample",
    )(page_table, *ins)


def _win_update_body(sk_ref, sv_ref, nk_ref, nv_ref, ok_ref, ov_ref):
    width = sk_ref.shape[1]
    for s_ref, n_ref, o_ref in ((sk_ref, nk_ref, ok_ref), (sv_ref, nv_ref, ov_ref)):
        o_ref[:, 0:width - 1] = s_ref[:, 1:width]
        o_ref[:, width - 1:width] = n_ref[...]


def _win_update(state_k, state_v, new_k, new_v, tb=4):
    batch, width = state_k.shape[:2]
    big = pl.BlockSpec((tb, width, N_KV_HEADS, HEAD_DIM), lambda i: (i, 0, 0, 0))
    one = pl.BlockSpec((tb, 1, N_KV_HEADS, HEAD_DIM), lambda i: (i, 0, 0, 0))
    sds = jax.ShapeDtypeStruct(state_k.shape, state_k.dtype)
    return pl.pallas_call(
        _win_update_body,
        grid=(batch // tb,),
        in_specs=[big, big, one, one],
        out_specs=[big, big],
        out_shape=[sds, sds],
        compiler_params=_cparams("parallel"),
        name="win_update",
    )(state_k, state_v, new_k, new_v)


def _rope_tables(pos):
    half = HEAD_DIM // 2
    inv_freq = jnp.power(ROPE_THETA, -jnp.arange(half, dtype=F32) / half)
    ang = pos.astype(F32)[:, None] * inv_freq[None, :]
    cos, sin = jnp.cos(ang), jnp.sin(ang)
    return jnp.concatenate([cos, cos], axis=1), jnp.concatenate([-sin, sin], axis=1)


def _overlap(n_sub, n_cols):
    n = jnp.arange(n_sub)[:, None]
    j = jnp.arange(LANES)[None, :]
    c_start, s_start = n * CMP_STRIDE, j * SEL_BLOCK
    hit = (c_start <= s_start + SEL_BLOCK - 1) & (c_start + CMP_BLOCK - 1 >= s_start)
    return (hit & (n < n_sub - 1) & (j < n_cols)).astype(BF16)


def kernel(x_prompt, x_sample, state_pool, cache_cmp_k, cache_cmp_v, cache_sel_k, cache_sel_v, state_win_k, state_win_v, page_table, norm_ffn1, norm_mix, norm_ffn2, norm_final, w_ffn_gate, w_ffn_up, w_ffn_down, w_pool, pool_scale, w_nsa_in, w_nsa_out, cmp_pe_k, cmp_w1_k, cmp_w2_k, cmp_pe_v, cmp_w1_v, cmp_w2_v):
    batch, seq, _ = x_prompt.shape
    dec_batch, dec_seq, _ = x_sample.shape
    past = page_table.shape[1] * PAGE_SIZE
    win_buf = state_win_k.shape[2]
    assert dec_seq == 1 and past % SEL_BLOCK == 0 and win_buf == WINDOW and seq % ATTN_TQ == 0
    assert norm_ffn1.shape[0] == 2 and w_pool.shape[0] == 1 and w_nsa_in.shape[0] == 1

    vec = lambda v: v.reshape(1, D_MODEL)
    wpool = w_pool[0].astype(BF16)
    n_main = Q_DIM + 6 * KV_DIM
    w_main = w_nsa_in[0].astype(BF16)
    wgt = w_nsa_in[0][:, n_main:].reshape(D_MODEL, N_KV_HEADS, GQA_GROUP, N_BRANCHES)
    wgt = jnp.transpose(wgt, (0, 1, 3, 2)).reshape(D_MODEL, N_HEADS * N_BRANCHES)
    wgt = jnp.pad(wgt, ((0, 0), (0, LANES - N_HEADS * N_BRANCHES))).astype(BF16)
    w_out = w_nsa_out[0].astype(BF16)
    cmp_k_w = _compress_weights(cmp_pe_k[0], cmp_w1_k[0], cmp_w2_k[0])
    cmp_v_w = _compress_weights(cmp_pe_v[0], cmp_w1_v[0], cmp_w2_v[0])
    ffn = lambda xp, xs, stages, gf=None: _ffn(xp, xs, stages, w_ffn_gate, w_ffn_up, w_ffn_down, g_final=gf)

    xp = x_prompt.reshape(batch * seq, D_MODEL)
    xs = x_sample.reshape(dec_batch, D_MODEL)
    xp, xs = ffn(xp, xs, [(0, 0, norm_ffn1[0])])

    xp, pool_p = _pool_prompt(xp, vec(norm_mix[0]), wpool, vec(pool_scale[0]), batch, seq)
    xs, h_s = _pool_sample(xs, jnp.transpose(state_pool[0], (1, 0, 2)), vec(norm_mix[0]), wpool, vec(pool_scale[0]))
    pool_s = jnp.concatenate([state_pool[0][:, 1:], h_s[:, None]], axis=1)
    xp, xs = ffn(xp, xs, [(0, 1, norm_ffn2[0]), (1, 0, norm_ffn1[1])])

    tm = 256
    cos_p, sin_p = _rope_tables(jnp.arange(seq))
    (q_raw, q_rot, kc_p, vc_p, ks_p, vs_p, kw_p, vw_p, ks_hm, vs_hm, kw_hm, vw_hm, gates_p) = _nsa_project(
        xp, vec(norm_mix[1]), w_main, wgt, cos_p, sin_p, seq // tm, tm)
    cos_s, sin_s = _rope_tables(jnp.full((dec_batch,), past))
    (q_raw_s, q_rot_s, kc_s, vc_s, ks_s, vs_s, kw_s, vw_s, _, _, _, _, gates_s) = _nsa_project(
        xs, vec(norm_mix[1]), w_main, wgt, cos_s, sin_s, 1, dec_batch)

    kcc, vcc = _compress_prompt(kc_p, vc_p, cmp_k_w, cmp_v_w, batch, seq)
    n_sub = seq // CMP_STRIDE
    expand = (jnp.arange(seq)[:, None] // SEL_BLOCK == jnp.arange(LANES)[None, :]).astype(BF16)
    ovt = jnp.transpose(_overlap(n_sub, seq // SEL_BLOCK))[:seq // SEL_BLOCK]
    new_row = lambda a: a.reshape(dec_batch, 1, N_KV_HEADS, HEAD_DIM)
    win = (state_win_k[0], state_win_v[0], new_row(kw_s), new_row(vw_s))
    attn_steps = batch * N_KV_HEADS * (seq // ATTN_TQ)
    if dec_batch % attn_steps == 0:
        o_p, win_k_s, win_v_s = _attn_prompt(q_raw, q_rot, kcc, vcc, ks_hm, vs_hm, kw_hm, vw_hm, gates_p, ovt, expand,
                                             batch, seq, win=win)
    else:
        (o_p,) = _attn_prompt(q_raw, q_rot, kcc, vcc, ks_hm, vs_hm, kw_hm, vw_hm, gates_p, ovt, expand, batch, seq)
        win_k_s, win_v_s = _win_update(*win)
    xp = _matmul_residual(o_p, w_out, xp, 2 * tm)

    per_head = lambda a: jnp.repeat(a.reshape(dec_batch, N_KV_HEADS, HEAD_DIM), GQA_GROUP, axis=1)
    g3 = gates_s[:, :, :N_BRANCHES * GQA_GROUP].reshape(N_KV_HEADS, dec_batch, N_BRANCHES, GQA_GROUP)
    g3 = jnp.transpose(g3, (1, 0, 3, 2)).reshape(dec_batch, N_HEADS, N_BRANCHES)
    chunk = CMP_STRIDE * N_KV_HEADS
    paged = lambda c: c[0].reshape(c.shape[1], PAGE_SIZE // CMP_STRIDE, chunk, HEAD_DIM)
    win_rows = lambda s: s[0].reshape(dec_batch, win_buf * N_KV_HEADS, HEAD_DIM)
    gs = (jnp.arange(N_HEADS)[:, None] // GQA_GROUP == jnp.arange(N_HEADS)[None, :] // GQA_GROUP).astype(BF16)
    n_sub_s = past // CMP_STRIDE
    ov_s = jnp.repeat(_overlap(n_sub_s, past // SEL_BLOCK + 1), N_KV_HEADS, axis=0)
    o_s = _attn_sample(page_table, [paged(c) for c in (cache_cmp_k, cache_cmp_v, cache_sel_k, cache_sel_v)],
                       win_rows(state_win_k), win_rows(state_win_v),
                       q_raw_s.reshape(dec_batch, N_HEADS, HEAD_DIM), q_rot_s.reshape(dec_batch, N_HEADS, HEAD_DIM),
                       [per_head(a) for a in (ks_s, vs_s, kw_s, vw_s)], g3, cmp_k_w, cmp_v_w, ov_s, gs)
    xs = _matmul_residual(o_s.reshape(dec_batch, Q_DIM), w_out, xs, dec_batch)

    yp, ys = ffn(xp, xs, [(1, 1, norm_ffn2[1])], vec(norm_final))
    y_prompt = yp.reshape(batch, seq, D_MODEL)
    y_sample = ys.reshape(dec_batch, 1, D_MODEL)

    kv5 = lambda a, b, s: a.reshape(1, b, s, N_KV_HEADS, HEAD_DIM)
    keep = min(WINDOW, seq)
    return (y_prompt, y_sample, pool_p[None], pool_s[None],
            kv5(kc_p, batch, seq), kv5(kc_s, dec_batch, 1), kv5(vc_p, batch, seq), kv5(vc_s, dec_batch, 1),
            kv5(ks_p, batch, seq), kv5(ks_s, dec_batch, 1), kv5(vs_p, batch, seq), kv5(vs_s, dec_batch, 1),
            kv5(kw_p, batch, seq)[:, :, seq - keep:], win_k_s[None],
            kv5(vw_p, batch, seq)[:, :, seq - keep:], win_v_s[None])
```

```python
import functools

import jax
import jax.numpy as jnp
from jax import lax
from jax.experimental import pallas as pl
from jax.experimental.pallas import tpu as pltpu

F32 = jnp.float32
BF16 = jnp.bfloat16

D_MODEL = 2048
D_FF = 5632
POOL_WINDOWS = (2, 4, 8, 16)
POOL_GROUP_DIM = D_MODEL // len(POOL_WINDOWS)
POOL_STATE_LEN = max(POOL_WINDOWS) - 1
N_HEADS = 16
HEAD_DIM = 128
N_KV_HEADS = 4
GQA_GROUP = N_HEADS // N_KV_HEADS
Q_DIM = N_HEADS * HEAD_DIM
KV_DIM = N_KV_HEADS * HEAD_DIM
N_BRANCHES = 3
CMP_BLOCK = 32
CMP_STRIDE = 16
SEL_BLOCK = 64
SEL_TOPK = 16
ATTN_TQ = 256
WINDOW = 512
PAGE_SIZE = 128
SCALE = HEAD_DIM ** -0.5
LOG2E = 1.4426950408889634
ROPE_THETA = 10000.0
RMS_EPS = 1e-6
NEG_INF = -1e30
FORCE_SCORE = 1e9
TINY = 1e-30

LANES = 128
SUBLANES = 8
VMEM_LIMIT = 56 * 1024 * 1024


def _cparams(*sem):
    return pltpu.CompilerParams(dimension_semantics=sem, vmem_limit_bytes=VMEM_LIMIT)


def _rms(x, g):
    y = x * lax.rsqrt(jnp.mean(x * x, axis=-1, keepdims=True) + RMS_EPS)
    return y * g


def _dot(a, b):
    return jnp.dot(a, b, preferred_element_type=F32)


def _div_pow2(x, n):
    return lax.shift_right_logical(x, n.bit_length() - 1)


def _dot_nt(a, b):
    return lax.dot_general(a, b, (((1,), (1,)), ((), ())), preferred_element_type=F32)


def _ffn_body(xp_ref, xs_ref, g_ref, wg_ref, wu_ref, wd_ref, *rest, final_norm, n_stage, nf, tm, ts):
    if final_norm:
        gf_ref, op_ref, os_ref, xn_ref = rest
    else:
        op_ref, os_ref, xn_ref = rest
    i, f = pl.program_id(0), pl.program_id(1)
    last = pl.num_programs(1) - 1

    for s in range(n_stage):
        @pl.when(f == s * nf)
        def _(s=s):
            x = xp_ref[...] if s == 0 else op_ref[...]
            xn_ref[0:tm] = _rms(x, g_ref[s:s + 1, :]).astype(BF16)
            if s == 0:
                op_ref[...] = x

        @pl.when((f == s * nf) & (i == 0))
        def _(s=s):
            x = xs_ref[...] if s == 0 else os_ref[...]
            xn_ref[tm:tm + ts] = _rms(x, g_ref[s:s + 1, :]).astype(BF16)
            if s == 0:
                os_ref[...] = x

    def half_step(rows):
        xn = xn_ref[0:rows]
        gate = _dot(xn, wg_ref[...].astype(BF16))
        up = _dot(xn, wu_ref[...].astype(BF16))
        h = (gate * jax.nn.sigmoid(gate)) * up
        return _dot((0.5 * h).astype(BF16), wd_ref[...].astype(BF16))

    @pl.when(i == 0)
    def _():
        y = half_step(tm + ts)
        op_ref[...] += y[0:tm]
        os_ref[...] += y[tm:tm + ts]

    @pl.when(i > 0)
    def _():
        op_ref[...] += half_step(tm)

    if final_norm:
        @pl.when(f == last)
        def _():
            op_ref[...] = _rms(op_ref[...], gf_ref[...])

        @pl.when((f == last) & (i == 0))
        def _():
            os_ref[...] = _rms(os_ref[...], gf_ref[...])


def _ffn(xp, xs, stages, wg, wu, wd, g_final=None, tm=1024, tf=256):
    m, ts = xp.shape[0], xs.shape[0]
    nf = D_FF // tf
    n_stage = len(stages)
    assert n_stage in (1, 2)
    final_norm = g_final is not None
    g = jnp.stack([s[2] for s in stages])
    row = lambda i, f: (i, 0)
    const = lambda i, f: (0, 0)
    pick = lambda f, k: stages[0][k] if n_stage == 1 else jnp.where(f >= nf, stages[1][k], stages[0][k])
    in_specs = [
        pl.BlockSpec((tm, D_MODEL), row),
        pl.BlockSpec((ts, D_MODEL), const),
        pl.BlockSpec((n_stage, D_MODEL), const),
        pl.BlockSpec((None, None, D_MODEL, tf), lambda i, f: (pick(f, 0), pick(f, 1), 0, f % nf)),
        pl.BlockSpec((None, None, D_MODEL, tf), lambda i, f: (pick(f, 0), pick(f, 1), 0, f % nf)),
        pl.BlockSpec((None, None, tf, D_MODEL), lambda i, f: (pick(f, 0), pick(f, 1), f % nf, 0)),
    ]
    args = [xp, xs, g, wg, wu, wd]
    if final_norm:
        in_specs.append(pl.BlockSpec((1, D_MODEL), const))
        args.append(g_final)
    return pl.pallas_call(
        functools.partial(_ffn_body, final_norm=final_norm, n_stage=n_stage, nf=nf, tm=tm, ts=ts),
        grid=(m // tm, n_stage * nf),
        in_specs=in_specs,
        out_specs=[pl.BlockSpec((tm, D_MODEL), row), pl.BlockSpec((ts, D_MODEL), const)],
        out_shape=[jax.ShapeDtypeStruct((m, D_MODEL), F32), jax.ShapeDtypeStruct((ts, D_MODEL), F32)],
        scratch_shapes=[pltpu.VMEM((tm + ts, D_MODEL), BF16)],
        compiler_params=_cparams("arbitrary", "arbitrary"),
        name="ffn",
    )(*args)


def _pool_groups(h_cols, x, w_ref, sc_ref, o_ref, window_sum, inv_cnt):
    for gi, w in enumerate(POOL_WINDOWS):
        c0 = gi * POOL_GROUP_DIM
        cols = slice(c0, c0 + POOL_GROUP_DIM)
        hc = h_cols(cols)
        d = window_sum(w, cols, hc) * inv_cnt(w) - hc
        y = _dot(d.astype(BF16), w_ref[gi])
        o_ref[:, cols] = x[:, cols] + y * sc_ref[:, cols]


def _pool_prompt_body(x_ref, g_ref, w_ref, sc_ref, o_ref, st_ref, hbuf, *, ts):
    s = pl.program_id(1)
    halo = 2 * SUBLANES

    @pl.when(s == 0)
    def _():
        hbuf[0:halo, :] = jnp.zeros((halo, D_MODEL), F32)

    @pl.when(s > 0)
    def _():
        hbuf[0:halo, :] = hbuf[ts:ts + halo, :]

    x = x_ref[...]
    hbuf[halo:halo + ts, :] = _rms(x, g_ref[...])
    pos = s * ts + lax.broadcasted_iota(jnp.int32, (ts, 1), 0)

    def window_sum(w, cols, hc):
        acc = hc
        for k in range(1, w):
            acc = acc + hbuf[halo - k:halo - k + ts, cols]
        return acc

    _pool_groups(lambda cols: hbuf[halo:halo + ts, cols], x, w_ref, sc_ref, o_ref, window_sum,
                 lambda w: 1.0 / jnp.minimum(w, pos + 1).astype(F32))

    @pl.when(s == pl.num_programs(1) - 1)
    def _():
        st_ref[0] = hbuf[halo + ts - POOL_STATE_LEN:halo + ts, :]


def _pool_prompt(x, g, w_pool, scale, batch, seq, ts=512):
    nt = seq // ts
    return pl.pallas_call(
        functools.partial(_pool_prompt_body, ts=ts),
        grid=(batch, nt),
        in_specs=[
            pl.BlockSpec((ts, D_MODEL), lambda b, s: (b * nt + s, 0)),
            pl.BlockSpec((1, D_MODEL), lambda b, s: (0, 0)),
            pl.BlockSpec((len(POOL_WINDOWS), POOL_GROUP_DIM, POOL_GROUP_DIM), lambda b, s: (0, 0, 0)),
            pl.BlockSpec((1, D_MODEL), lambda b, s: (0, 0)),
        ],
        out_specs=[
            pl.BlockSpec((ts, D_MODEL), lambda b, s: (b * nt + s, 0)),
            pl.BlockSpec((1, POOL_STATE_LEN, D_MODEL), lambda b, s: (b, 0, 0)),
        ],
        out_shape=[
            jax.ShapeDtypeStruct((batch * seq, D_MODEL), F32),
            jax.ShapeDtypeStruct((batch, POOL_STATE_LEN, D_MODEL), F32),
        ],
        scratch_shapes=[pltpu.VMEM((ts + 2 * SUBLANES, D_MODEL), F32)],
        compiler_params=_cparams("arbitrary", "arbitrary"),
        name="pool_prompt",
    )(x, g, w_pool, scale)


def _pool_sample_body(x_ref, st_ref, g_ref, w_ref, sc_ref, o_ref, h_ref):
    x = x_ref[...]
    h_ref[...] = _rms(x, g_ref[...])

    def window_sum(w, cols, hc):
        acc = hc
        for k in range(1, w):
            acc = acc + st_ref[POOL_STATE_LEN - k, :, cols]
        return acc

    _pool_groups(lambda cols: h_ref[:, cols], x, w_ref, sc_ref, o_ref, window_sum, lambda w: 1.0 / w)


def _pool_sample(x, state_t, g, w_pool, scale, tb=32):
    b = x.shape[0]
    return pl.pallas_call(
        _pool_sample_body,
        grid=(b // tb,),
        in_specs=[
            pl.BlockSpec((tb, D_MODEL), lambda i: (i, 0)),
            pl.BlockSpec((POOL_STATE_LEN, tb, D_MODEL), lambda i: (0, i, 0)),
            pl.BlockSpec((1, D_MODEL), lambda i: (0, 0)),
            pl.BlockSpec((len(POOL_WINDOWS), POOL_GROUP_DIM, POOL_GROUP_DIM), lambda i: (0, 0, 0)),
            pl.BlockSpec((1, D_MODEL), lambda i: (0, 0)),
        ],
        out_specs=[pl.BlockSpec((tb, D_MODEL), lambda i: (i, 0)), pl.BlockSpec((tb, D_MODEL), lambda i: (i, 0))],
        out_shape=[jax.ShapeDtypeStruct((b, D_MODEL), F32), jax.ShapeDtypeStruct((b, D_MODEL), F32)],
        compiler_params=_cparams("parallel"),
        name="pool_sample",
    )(x, state_t, g, w_pool, scale)


def _proj_body(x_ref, g_ref, w_ref, wg_ref, cos_ref, sin_ref,
               qraw_ref, qrot_ref, kc_ref, vc_ref, ks_ref, vs_ref, kw_ref, vw_ref,
               kshm_ref, vshm_ref, kwhm_ref, vwhm_ref, gates_ref, *, tm):
    xn = _rms(x_ref[...], g_ref[...]).astype(BF16)
    gall = jax.nn.sigmoid(_dot(xn, wg_ref[...]))
    per_head = N_BRANCHES * GQA_GROUP
    for kvh in range(N_KV_HEADS):
        gates_ref[kvh] = gall if kvh == 0 else pltpu.roll(gall, LANES - kvh * per_head, 1)

    cos, sin = cos_ref[...], sin_ref[...]
    rope = lambda s: s * cos + pltpu.roll(s, HEAD_DIM // 2, 1) * sin
    tile = lambda j: _dot(xn, w_ref[:, j * KV_DIM:(j + 1) * KV_DIM])
    heads = lambda a: [a[:, h * HEAD_DIM:(h + 1) * HEAD_DIM] for h in range(N_KV_HEADS)]

    nq = Q_DIM // KV_DIM
    for j in range(nq):
        zs = tile(j) * (SCALE * LOG2E)
        qraw_ref[:, j * KV_DIM:(j + 1) * KV_DIM] = zs.astype(BF16)
        for h, s in enumerate(heads(zs)):
            c0 = j * KV_DIM + h * HEAD_DIM
            qrot_ref[:, c0:c0 + HEAD_DIM] = rope(s).astype(BF16)

    kinds = [(kc_ref, None, False), (vc_ref, None, False), (ks_ref, kshm_ref, True), (vs_ref, vshm_ref, False),
             (kw_ref, kwhm_ref, True), (vw_ref, vwhm_ref, False)]
    for off, (ref, hm_ref, rot) in enumerate(kinds):
        for h, s in enumerate(heads(tile(nq + off))):
            s = rope(s) if rot else s
            ref[pl.ds(h, tm, stride=N_KV_HEADS), :] = s
            if hm_ref is not None:
                hm_ref[h] = s.astype(BF16)


def _nsa_project(x, g, w_main, w_gates, cos, sin, rope_period_tiles, tm):
    m = x.shape[0]
    row = lambda i: (i, 0)
    const = lambda i: (0, 0)
    kv_sds = jax.ShapeDtypeStruct((m * N_KV_HEADS, HEAD_DIM), F32)
    hm_sds = jax.ShapeDtypeStruct((N_KV_HEADS, m, HEAD_DIM), BF16)
    q_sds = jax.ShapeDtypeStruct((m, Q_DIM), BF16)
    kv_spec = pl.BlockSpec((tm * N_KV_HEADS, HEAD_DIM), row)
    hm_spec = pl.BlockSpec((N_KV_HEADS, tm, HEAD_DIM), lambda i: (0, i, 0))
    q_spec = pl.BlockSpec((tm, Q_DIM), row)
    return pl.pallas_call(
        functools.partial(_proj_body, tm=tm),
        grid=(m // tm,),
        in_specs=[
            pl.BlockSpec((tm, D_MODEL), row),
            pl.BlockSpec((1, D_MODEL), const),
            pl.BlockSpec(w_main.shape, const, pipeline_mode=pl.Buffered(1)),
            pl.BlockSpec(w_gates.shape, const, pipeline_mode=pl.Buffered(1)),
            pl.BlockSpec((tm, HEAD_DIM), lambda i: (i % rope_period_tiles, 0)),
            pl.BlockSpec((tm, HEAD_DIM), lambda i: (i % rope_period_tiles, 0)),
        ],
        out_specs=[q_spec, q_spec] + [kv_spec] * 6 + [hm_spec] * 4
        + [pl.BlockSpec((N_KV_HEADS, tm, LANES), lambda i: (0, i, 0))],
        out_shape=[q_sds, q_sds] + [kv_sds] * 6 + [hm_sds] * 4
        + [jax.ShapeDtypeStruct((N_KV_HEADS, m, LANES), F32)],
        compiler_params=_cparams("parallel"),
        name="nsa_project",
    )(x, g, w_main, w_gates, cos, sin)


N_PAIR = CMP_STRIDE // 2


def _compress(streams, n_sub):
    n_rows = n_sub * N_KV_HEADS
    low = lax.broadcasted_iota(jnp.int32, (1, SUBLANES, 1), 1) < N_KV_HEADS

    def repack(v):
        pairs = lambda a: a.reshape(n_sub // 2, 2, SUBLANES, HEAD_DIM)
        v4, r4 = pairs(v), pairs(pltpu.roll(v, N_KV_HEADS, 0))
        ev = jnp.where(low, v4[:, 0], r4[:, 1]).reshape(n_rows, HEAD_DIM)
        od = jnp.where(low, r4[:, 1], v4[:, 1]).reshape(n_rows, HEAD_DIM)
        return jnp.concatenate([ev, od], axis=1).astype(BF16)

    accs = [jnp.zeros((n_rows, 2 * HEAD_DIM), F32) for _ in streams]
    biases = [jnp.zeros((SUBLANES, 2 * HEAD_DIM), F32) for _ in streams]
    for k in range(N_PAIR):
        for i, (get_rows, wpair_ref, pepair_ref, _) in enumerate(streams):
            accs[i] = accs[i] + _dot(repack(get_rows(k)), wpair_ref[k])
            biases[i] = biases[i] + _dot(pepair_ref[k].astype(BF16), wpair_ref[k])
    outs = []
    for acc, bias, (_, _, _, w2_ref) in zip(accs, biases, streams):
        first = acc[:, :HEAD_DIM] + bias[0:1, :HEAD_DIM]
        second = acc[:, HEAD_DIM:] + bias[1:2, HEAD_DIM:]
        pre = first + pltpu.roll(second, n_rows - N_KV_HEADS, 0)
        outs.append(_dot(jax.nn.gelu(pre, approximate=True).astype(BF16), w2_ref[...]))
    return outs


def _compress_body(xk_ref, xv_ref, wpk_ref, pek_ref, w2k_ref, wpv_ref, pev_ref, w2v_ref, ok_ref, ov_ref, res_ref, *,
                   n_sub):
    rows = lambda x_ref: lambda k: x_ref[:, k * SUBLANES:(k + 1) * SUBLANES, :].reshape(n_sub * SUBLANES, HEAD_DIM)
    outs = _compress([(rows(xk_ref), wpk_ref, pek_ref, w2k_ref), (rows(xv_ref), wpv_ref, pev_ref, w2v_ref)], n_sub)
    for res, o_ref in zip(outs, (ok_ref, ov_ref)):
        res_ref[...] = res
        for h in range(N_KV_HEADS):
            o_ref[0, h] = res_ref[pl.ds(h, n_sub, stride=N_KV_HEADS), :].astype(BF16)


def _compress_prompt(rows_k, rows_v, cmp_k_w, cmp_v_w, batch, seq):
    n_sub = seq // CMP_STRIDE
    chunk = CMP_STRIDE * N_KV_HEADS
    x3 = lambda rows: rows.reshape(batch * n_sub, chunk, HEAD_DIM)
    xspec = pl.BlockSpec((n_sub, chunk, HEAD_DIM), lambda b: (b, 0, 0))
    wspecs = [pl.BlockSpec((N_PAIR, 2 * HEAD_DIM, 2 * HEAD_DIM), lambda b: (0, 0, 0)),
              pl.BlockSpec((N_PAIR, SUBLANES, 2 * HEAD_DIM), lambda b: (0, 0, 0)),
              pl.BlockSpec((HEAD_DIM, HEAD_DIM), lambda b: (0, 0))]
    ospec = pl.BlockSpec((1, N_KV_HEADS, n_sub, HEAD_DIM), lambda b: (b, 0, 0, 0))
    osds = jax.ShapeDtypeStruct((batch, N_KV_HEADS, n_sub, HEAD_DIM), BF16)
    return pl.pallas_call(
        functools.partial(_compress_body, n_sub=n_sub),
        grid=(batch,),
        in_specs=[xspec, xspec] + wspecs + wspecs,
        out_specs=[ospec, ospec],
        out_shape=[osds, osds],
        scratch_shapes=[pltpu.VMEM((n_sub * N_KV_HEADS, HEAD_DIM), F32)],
        compiler_params=_cparams("parallel"),
        name="compress_prompt",
    )(x3(rows_k), x3(rows_v), *cmp_k_w, *cmp_v_w)


def _compress_weights(pe, w1, w2):
    w1r = w1.reshape(2, N_PAIR, 2, HEAD_DIM, HEAD_DIM)
    wpair = jnp.transpose(w1r, (1, 2, 3, 0, 4)).reshape(N_PAIR, 2 * HEAD_DIM, 2 * HEAD_DIM).astype(BF16)
    per = pe.reshape(2, N_PAIR, 2 * HEAD_DIM)
    pepair = jnp.zeros((N_PAIR, SUBLANES, 2 * HEAD_DIM), F32).at[:, 0:2, :].set(jnp.transpose(per, (1, 0, 2)))
    return wpair, pepair, w2.astype(BF16)


def _topk_select(score, valid, n_blk):
    lane = lax.broadcasted_iota(jnp.int32, score.shape, 1)
    rank = jnp.zeros(score.shape, F32)
    for i in range(n_blk):
        ci = score[:, i:i + 1]
        beats = (ci > score) | ((ci == score) & (lane > i))
        rank = rank + jnp.where(beats, 1.0, 0.0)
    return (rank < SEL_TOPK) & valid


def _split_dot(p, w):
    hi = p.astype(BF16)
    lo = (p - hi.astype(F32)).astype(BF16)
    return _dot(hi, w) + _dot(lo, w)


def _attn_prompt_tile(q, qraw_ref, qrot_ref, kcc_ref, vcc_ref, ks_ref, vs_ref, kw_ref, vw_ref, gates_ref, ovt_ref,
                      ext_ref, o_ref, *, tq, seq):
    q0 = q * tq
    kt = tq
    qpos = q0 + lax.broadcasted_iota(jnp.int32, (tq, 1), 0)
    qpos4 = jnp.concatenate([qpos] * GQA_GROUP, axis=0)
    stack = lambda ref: jnp.concatenate([ref[:, g * HEAD_DIM:(g + 1) * HEAD_DIM] for g in range(GQA_GROUP)], axis=0)
    q_raw = stack(qraw_ref)
    q_rot = stack(qrot_ref)

    n_cmp = seq // CMP_STRIDE
    sc = _dot_nt(q_raw, kcc_ref[0, 0])
    c_end = lax.broadcasted_iota(jnp.int32, (1, n_cmp), 1) * CMP_STRIDE + (CMP_BLOCK - 1)
    cmask = c_end <= qpos4
    scm = jnp.where(cmask, sc, NEG_INF)
    e = jnp.where(cmask, jnp.exp2(scm - jnp.max(scm, axis=-1, keepdims=True)), 0.0)
    pc = e / jnp.maximum(jnp.sum(e, axis=-1, keepdims=True), TINY)
    o_cmp = _dot(pc.astype(BF16), vcc_ref[0, 0])

    ranked = q0 + tq > SEL_TOPK * SEL_BLOCK
    if ranked:
        n_blk = seq // SEL_BLOCK
        psum = pc[0:tq] + pc[tq:2 * tq] + pc[2 * tq:3 * tq] + pc[3 * tq:4 * tq]
        hi = psum.astype(BF16)
        lo = (psum - hi.astype(F32)).astype(BF16)
        imp = _dot_nt(ovt_ref[...], hi) + _dot_nt(ovt_ref[...], lo)
        blk = lax.broadcasted_iota(jnp.int32, (n_blk, 1), 0)
        qlane = q0 + lax.broadcasted_iota(jnp.int32, (1, tq), 1)
        cur = _div_pow2(qlane, SEL_BLOCK)
        valid = blk * SEL_BLOCK <= qlane
        forced = (blk == 0) | (blk == cur) | (blk == cur - 1)
        score = jnp.where(valid, jnp.where(forced, FORCE_SCORE, imp), NEG_INF)
        rank = jnp.zeros((n_blk, tq), F32)
        for i in range((q0 + tq) // SEL_BLOCK):
            ci = score[i:i + 1, :]
            beats = (ci > score) | ((ci == score) & (blk > i))
            rank = rank + jnp.where(beats, 1.0, 0.0)
        sel_t = jnp.where((rank < SEL_TOPK) & valid, 1.0, 0.0)
        sel = jnp.concatenate([sel_t, jnp.zeros((LANES - n_blk, tq), F32)], axis=0).T
        sel_neg = jnp.where(sel > 0.5, 0.0, NEG_INF).astype(BF16)
        q_sel = jnp.concatenate([q_rot, jnp.concatenate([sel_neg] * GQA_GROUP, axis=0)], axis=1)

    def attend_pair(first, second):
        def by_halves(dot, lhs, rhs):
            half = lhs.shape[0] // 2
            return jnp.concatenate([dot(lhs[:half], rhs), dot(lhs[half:], rhs)], axis=0)

        def scores(k_ref, chunks, by_block):
            for c, bias in chunks:
                k = k_ref[0, c * kt:(c + 1) * kt, :]
                if by_block:
                    s = by_halves(_dot_nt, q_sel, jnp.concatenate([k, ext_ref[c * kt:(c + 1) * kt, :]], axis=1))
                else:
                    s = _dot_nt(q_rot, k)
                yield s if bias is None else s + jnp.concatenate([bias] * GQA_GROUP, axis=0)

        def row_max(ss):
            top = ss[0]
            for s in ss[1:]:
                top = jnp.maximum(top, s)
            return jnp.max(top, axis=-1, keepdims=True)

        def probs(ss, m):
            for s in ss:
                yield jnp.exp2(s - m)

        ones = jnp.ones((kt, HEAD_DIM), BF16)

        def weighted(v_ref, chunks, ps):
            for (c, _), p in zip(chunks, ps):
                yield _dot(p.astype(BF16), jnp.concatenate([v_ref[0, c * kt:(c + 1) * kt, :], ones], axis=1))

        def alternate(lead, follow):
            a, b = [], []
            for item in lead:
                a.append(item)
                nxt = next(follow, None)
                if nxt is not None:
                    b.append(nxt)
            b.extend(follow)
            return a, b

        (k1, v1, ch1), (k2, v2, ch2) = first, second
        s1 = list(scores(k1, ch1, False))
        m1 = row_max(s1)
        p1, s2 = alternate(probs(s1, m1), scores(k2, ch2, ranked))
        m2 = row_max(s2)
        pv1, p2 = alternate(weighted(v1, ch1, p1), probs(s2, m2))
        pv2 = list(weighted(v2, ch2, p2))
        def out(pv):
            tot = sum(pv[1:], pv[0])
            return tot[:, :HEAD_DIM] / tot[:, HEAD_DIM:]

        return out(pv1), out(pv2)

    def mask_bias(c, ok):
        kpos = c * kt + lax.broadcasted_iota(jnp.int32, (1, kt), 1)
        return jnp.where(ok(kpos), 0.0, NEG_INF)

    sel_chunks = [(c, mask_bias(c, lambda kpos: kpos <= qpos) if c == q else None) for c in range(q + 1)]

    win_chunks = []
    for c in range(q + 1):
        if c * kt + kt - 1 <= q0 - WINDOW:
            continue
        whole = c * kt + kt - 1 <= q0 and c * kt > q0 + tq - 1 - WINDOW
        win_chunks.append((c, None if whole else mask_bias(c, lambda kpos: (kpos <= qpos) & (kpos > qpos - WINDOW))))
    o_win, o_sel = attend_pair((kw_ref, vw_ref, win_chunks), (ks_ref, vs_ref, sel_chunks))

    gates = gates_ref[0]
    for g in range(GQA_GROUP):
        r = slice(g * tq, (g + 1) * tq)
        gate = lambda br: gates[:, br * GQA_GROUP + g:br * GQA_GROUP + g + 1]
        o = gate(0) * o_cmp[r] + gate(1) * o_sel[r] + gate(2) * o_win[r]
        o_ref[:, g * HEAD_DIM:(g + 1) * HEAD_DIM] = o.astype(BF16)


def _attn_prompt_body(*refs, tq, seq, with_win):
    if with_win:
        (*refs, sk_ref, sv_ref, nk_ref, nv_ref, o_ref, ok_ref, ov_ref) = refs
        _win_update_body(sk_ref, sv_ref, nk_ref, nv_ref, ok_ref, ov_ref)
        refs = (*refs, o_ref)
    qi = pl.program_id(2)
    for q in range(seq // tq):
        @pl.when(qi == q)
        def _(q=q):
            _attn_prompt_tile(q, *refs, tq=tq, seq=seq)


def _attn_prompt(q_raw, q_rot, kcc, vcc, ks_hm, vs_hm, kw_hm, vw_hm, gates, ovt, ex, batch, seq, win=None, tq=ATTN_TQ):
    nq = seq // tq
    n_cmp = seq // CMP_STRIDE
    qspec = pl.BlockSpec((tq, KV_DIM), lambda b, h, i: (b * nq + i, h))
    cspec = pl.BlockSpec((1, 1, n_cmp, HEAD_DIM), lambda b, h, i: (b, h, 0, 0))
    kvspec = pl.BlockSpec((1, seq, HEAD_DIM), lambda b, h, i: (h, b, 0))
    in_specs = [qspec, qspec, cspec, cspec, kvspec, kvspec, kvspec, kvspec,
                pl.BlockSpec((1, tq, LANES), lambda b, h, i: (h, b * nq + i, 0)),
                pl.BlockSpec(ovt.shape, lambda b, h, i: (0, 0)),
                pl.BlockSpec((seq, LANES), lambda b, h, i: (0, 0))]
    args = [q_raw, q_rot, kcc, vcc, ks_hm, vs_hm, kw_hm, vw_hm, gates, ovt, ex]
    out_specs = [qspec]
    out_shape = [jax.ShapeDtypeStruct((batch * seq, Q_DIM), BF16)]
    if win is not None:
        state_k = win[0]
        tb = state_k.shape[0] // (batch * N_KV_HEADS * nq)
        step = lambda b, h, i: ((b * N_KV_HEADS + h) * nq + i, 0, 0, 0)
        big = pl.BlockSpec((tb,) + state_k.shape[1:], step)
        one = pl.BlockSpec((tb, 1) + state_k.shape[2:], step)
        in_specs += [big, big, one, one]
        args += list(win)
        out_specs += [big, big]
        out_shape += [jax.ShapeDtypeStruct(state_k.shape, state_k.dtype)] * 2
    return pl.pallas_call(
        functools.partial(_attn_prompt_body, tq=tq, seq=seq, with_win=win is not None),
        grid=(batch, N_KV_HEADS, nq),
        in_specs=in_specs,
        out_specs=out_specs,
        out_shape=out_shape,
        compiler_params=_cparams("parallel", "parallel", "arbitrary"),
        name="attn_prompt",
    )(*args)


def _matmul_res_body(a_ref, w_ref, r_ref, o_ref):
    o_ref[...] = r_ref[...] + _dot(a_ref[...], w_ref[...])


def _matmul_residual(a, w, res, tm):
    m, k = a.shape
    n = w.shape[1]
    return pl.pallas_call(
        _matmul_res_body,
        grid=(m // tm,),
        in_specs=[pl.BlockSpec((tm, k), lambda i: (i, 0)), pl.BlockSpec((k, n), lambda i: (0, 0)),
                  pl.BlockSpec((tm, n), lambda i: (i, 0))],
        out_specs=pl.BlockSpec((tm, n), lambda i: (i, 0)),
        out_shape=jax.ShapeDtypeStruct((m, n), F32),
        compiler_params=_cparams("parallel"),
        name="out_proj",
    )(a, w, res)


def _attn_sample_body(pt_ref, *refs, n_pages, past):
    del pt_ref
    pages = refs[:4 * n_pages]
    (wink_ref, winv_ref, qraw_ref, qrot_ref, kse_ref, vse_ref, kwe_ref, vwe_ref, gates_ref,
     wpk_ref, pek_ref, w2k_ref, wpv_ref, pev_ref, w2v_ref, ov_ref, gs_ref, o_ref, s_ref) = refs[4 * n_pages:]
    ck, cv, sk, sv = (pages[i * n_pages:(i + 1) * n_pages] for i in range(4))
    n_sub = past // CMP_STRIDE
    n_rows = n_sub * N_KV_HEADS
    sub_pp = PAGE_SIZE // CMP_STRIDE
    page_rows = PAGE_SIZE * N_KV_HEADS

    def rows_of(pg):
        return lambda k: jnp.concatenate(
            [r[0, :, k * SUBLANES:(k + 1) * SUBLANES, :].reshape(sub_pp * SUBLANES, HEAD_DIM) for r in pg], axis=0)

    kcc, vcc = _compress([(rows_of(ck), wpk_ref, pek_ref, w2k_ref), (rows_of(cv), wpv_ref, pev_ref, w2v_ref)], n_sub)
    kcc, vcc = kcc.astype(BF16), vcc.astype(BF16)

    q_raw = qraw_ref[0]
    q_rot = qrot_ref[0]
    q_rot32 = q_rot.astype(F32)
    hrow = _div_pow2(lax.broadcasted_iota(jnp.int32, (N_HEADS, 1), 0), GQA_GROUP)

    def softmax_parts(s_old, s_new):
        m = jnp.maximum(jnp.max(s_old, axis=-1, keepdims=True), s_new)
        p_old = jnp.exp2(s_old - m)
        p_new = jnp.exp2(s_new - m)
        return p_old, p_new, jnp.sum(p_old, axis=-1, keepdims=True) + p_new

    col = lax.broadcasted_iota(jnp.int32, (1, n_rows), 1)
    cmask = ((col & (N_KV_HEADS - 1)) == hrow) & (
        _div_pow2(col, N_KV_HEADS) * CMP_STRIDE + (CMP_BLOCK - 1) <= past)
    scm = jnp.where(cmask, _dot_nt(q_raw, kcc), NEG_INF)
    e = jnp.where(cmask, jnp.exp2(scm - jnp.max(scm, axis=-1, keepdims=True)), 0.0)
    pc = e / jnp.maximum(jnp.sum(e, axis=-1, keepdims=True), TINY)
    o_cmp = _dot(pc.astype(BF16), vcc)

    imp16 = _split_dot(pc, ov_ref[...])
    hi = imp16.astype(BF16)
    lo = (imp16 - hi.astype(F32)).astype(BF16)
    imp = _dot(gs_ref[...], hi) + _dot(gs_ref[...], lo)
    n_blk = past // SEL_BLOCK + 1
    cur = past // SEL_BLOCK
    blk = lax.broadcasted_iota(jnp.int32, (N_HEADS, LANES), 1)
    valid = blk < n_blk
    forced = (blk == 0) | (blk == cur) | (blk == cur - 1)
    score = jnp.where(valid, jnp.where(forced, FORCE_SCORE, imp), NEG_INF)
    sel = jnp.where(_topk_select(score, valid, n_blk), 1.0, 0.0)

    pcol = lax.broadcasted_iota(jnp.int32, (1, page_rows), 1)
    head_ok = (pcol & (N_KV_HEADS - 1)) == hrow
    upper = pcol >= page_rows // 2
    for p in range(n_pages):
        s = _dot_nt(q_rot, sk[p][0].reshape(page_rows, HEAD_DIM).astype(BF16))
        chosen = jnp.where(upper, sel[:, 2 * p + 1:2 * p + 2], sel[:, 2 * p:2 * p + 1])
        s_ref[:, p * page_rows:(p + 1) * page_rows] = jnp.where(head_ok & (chosen > 0.5), s, NEG_INF)
    s_new = jnp.sum(q_rot32 * kse_ref[0], axis=-1, keepdims=True)
    p_old, p_new, denom = softmax_parts(s_ref[...], s_new)
    acc = p_new * vse_ref[0]
    for p in range(n_pages):
        acc = acc + _dot(p_old[:, p * page_rows:(p + 1) * page_rows].astype(BF16),
                         sv[p][0].reshape(page_rows, HEAD_DIM).astype(BF16))
    o_sel = acc / denom

    n_win = wink_ref.shape[1]
    wcol = lax.broadcasted_iota(jnp.int32, (1, n_win), 1)
    wmask = ((wcol & (N_KV_HEADS - 1)) == hrow) & (_div_pow2(wcol, N_KV_HEADS) >= n_win // N_KV_HEADS + 1 - WINDOW)
    sw = jnp.where(wmask, _dot_nt(q_rot, wink_ref[0].astype(BF16)), NEG_INF)
    sw_new = jnp.sum(q_rot32 * kwe_ref[0], axis=-1, keepdims=True)
    p_old, p_new, denom = softmax_parts(sw, sw_new)
    o_win = (p_new * vwe_ref[0] + _dot(p_old.astype(BF16), winv_ref[0].astype(BF16))) / denom

    gates = gates_ref[0]
    o = gates[:, 0:1] * o_cmp + gates[:, 1:2] * o_sel + gates[:, 2:3] * o_win
    o_ref[0] = o.astype(BF16)


def _attn_sample(page_table, caches, win_k, win_v, q_raw, q_rot, new_rows, gates, cmp_k_w, cmp_v_w, ov, gs):
    batch, n_pages = page_table.shape
    past = n_pages * PAGE_SIZE
    chunk = CMP_STRIDE * N_KV_HEADS
    page_specs = [pl.BlockSpec((1, PAGE_SIZE // CMP_STRIDE, chunk, HEAD_DIM), lambda b, pt, p=p: (pt[b, p], 0, 0, 0))
                  for p in range(n_pages)]
    per_b = lambda a: pl.BlockSpec((1,) + a.shape[1:], lambda b, pt: (b,) + (0,) * (a.ndim - 1))
    whole = lambda a: pl.BlockSpec(a.shape, lambda b, pt: (0,) * a.ndim)
    ins, specs = [], []
    for c in caches:
        ins += [c] * n_pages
        specs += page_specs
    for a in (win_k, win_v, q_raw, q_rot, *new_rows, gates):
        ins.append(a)
        specs.append(per_b(a))
    for a in (*cmp_k_w, *cmp_v_w, ov, gs):
        ins.append(a)
        specs.append(whole(a))
    return pl.pallas_call(
        functools.partial(_attn_sample_body, n_pages=n_pages, past=past),
        grid_spec=pltpu.PrefetchScalarGridSpec(
            num_scalar_prefetch=1, grid=(batch,), in_specs=specs,
            out_specs=pl.BlockSpec((1, N_HEADS, HEAD_DIM), lambda b, pt: (b, 0, 0)),
            scratch_shapes=[pltpu.VMEM((N_HEADS, past * N_KV_HEADS), F32)]),
        out_shape=jax.ShapeDtypeStruct((batch, N_HEADS, HEAD_DIM), BF16),
        compiler_params=_cparams("arbitrary"),
        name="attn_sample",
    )(page_table, *ins)


def _win_update_body(sk_ref, sv_ref, nk_ref, nv_ref, ok_ref, ov_ref):
    width = sk_ref.shape[1]
    for s_ref, n_ref, o_ref in ((sk_ref, nk_ref, ok_ref), (sv_ref, nv_ref, ov_ref)):
        o_ref[:, 0:width - 1] = s_ref[:, 1:width]
        o_ref[:, width - 1:width] = n_ref[...]


def _win_update(state_k, state_v, new_k, new_v, tb=4):
    batch, width = state_k.shape[:2]
    big = pl.BlockSpec((tb, width, N_KV_HEADS, HEAD_DIM), lambda i: (i, 0, 0, 0))
    one = pl.BlockSpec((tb, 1, N_KV_HEADS, HEAD_DIM), lambda i: (i, 0, 0, 0))
    sds = jax.ShapeDtypeStruct(state_k.shape, state_k.dtype)
    return pl.pallas_call(
        _win_update_body,
        grid=(batch // tb,),
        in_specs=[big, big, one, one],
        out_specs=[big, big],
        out_shape=[sds, sds],
        compiler_params=_cparams("parallel"),
        name="win_update",
    )(state_k, state_v, new_k, new_v)


def _rope_tables(pos):
    half = HEAD_DIM // 2
    inv_freq = jnp.power(ROPE_THETA, -jnp.arange(half, dtype=F32) / half)
    ang = pos.astype(F32)[:, None] * inv_freq[None, :]
    cos, sin = jnp.cos(ang), jnp.sin(ang)
    return jnp.concatenate([cos, cos], axis=1), jnp.concatenate([-sin, sin], axis=1)


def _overlap(n_sub, n_cols):
    n = jnp.arange(n_sub)[:, None]
    j = jnp.arange(LANES)[None, :]
    c_start, s_start = n * CMP_STRIDE, j * SEL_BLOCK
    hit = (c_start <= s_start + SEL_BLOCK - 1) & (c_start + CMP_BLOCK - 1 >= s_start)
    return (hit & (n < n_sub - 1) & (j < n_cols)).astype(BF16)


def kernel(x_prompt, x_sample, state_pool, cache_cmp_k, cache_cmp_v, cache_sel_k, cache_sel_v, state_win_k, state_win_v, page_table, norm_ffn1, norm_mix, norm_ffn2, norm_final, w_ffn_gate, w_ffn_up, w_ffn_down, w_pool, pool_scale, w_nsa_in, w_nsa_out, cmp_pe_k, cmp_w1_k, cmp_w2_k, cmp_pe_v, cmp_w1_v, cmp_w2_v):
    batch, seq, _ = x_prompt.shape
    dec_batch, dec_seq, _ = x_sample.shape
    past = page_table.shape[1] * PAGE_SIZE
    win_buf = state_win_k.shape[2]
    assert dec_seq == 1 and past % SEL_BLOCK == 0 and win_buf == WINDOW and seq % ATTN_TQ == 0
    assert norm_ffn1.shape[0] == 2 and w_pool.shape[0] == 1 and w_nsa_in.shape[0] == 1

    vec = lambda v: v.reshape(1, D_MODEL)
    wpool = w_pool[0].astype(BF16)
    n_main = Q_DIM + 6 * KV_DIM
    w_main = w_nsa_in[0].astype(BF16)
    wgt = w_nsa_in[0][:, n_main:].reshape(D_MODEL, N_KV_HEADS, GQA_GROUP, N_BRANCHES)
    wgt = jnp.transpose(wgt, (0, 1, 3, 2)).reshape(D_MODEL, N_HEADS * N_BRANCHES)
    wgt = jnp.pad(wgt, ((0, 0), (0, LANES - N_HEADS * N_BRANCHES))).astype(BF16)
    w_out = w_nsa_out[0].astype(BF16)
    cmp_k_w = _compress_weights(cmp_pe_k[0], cmp_w1_k[0], cmp_w2_k[0])
    cmp_v_w = _compress_weights(cmp_pe_v[0], cmp_w1_v[0], cmp_w2_v[0])
    ffn = lambda xp, xs, stages, gf=None: _ffn(xp, xs, stages, w_ffn_gate, w_ffn_up, w_ffn_down, g_final=gf)

    xp = x_prompt.reshape(batch * seq, D_MODEL)
    xs = x_sample.reshape(dec_batch, D_MODEL)
    xp, xs = ffn(xp, xs, [(0, 0, norm_ffn1[0])])

    xp, pool_p = _pool_prompt(xp, vec(norm_mix[0]), wpool, vec(pool_scale[0]), batch, seq)
    xs, h_s = _pool_sample(xs, jnp.transpose(state_pool[0], (1, 0, 2)), vec(norm_mix[0]), wpool, vec(pool_scale[0]))
    pool_s = jnp.concatenate([state_pool[0][:, 1:], h_s[:, None]], axis=1)
    xp, xs = ffn(xp, xs, [(0, 1, norm_ffn2[0]), (1, 0, norm_ffn1[1])])

    tm = 256
    cos_p, sin_p = _rope_tables(jnp.arange(seq))
    (q_raw, q_rot, kc_p, vc_p, ks_p, vs_p, kw_p, vw_p, ks_hm, vs_hm, kw_hm, vw_hm, gates_p) = _nsa_project(
        xp, vec(norm_mix[1]), w_main, wgt, cos_p, sin_p, seq // tm, tm)
    cos_s, sin_s = _rope_tables(jnp.full((dec_batch,), past))
    (q_raw_s, q_rot_s, kc_s, vc_s, ks_s, vs_s, kw_s, vw_s, _, _, _, _, gates_s) = _nsa_project(
        xs, vec(norm_mix[1]), w_main, wgt, cos_s, sin_s, 1, dec_batch)

    kcc, vcc = _compress_prompt(kc_p, vc_p, cmp_k_w, cmp_v_w, batch, seq)
    n_sub = seq // CMP_STRIDE
    expand = (jnp.arange(seq)[:, None] // SEL_BLOCK == jnp.arange(LANES)[None, :]).astype(BF16)
    ovt = jnp.transpose(_overlap(n_sub, seq // SEL_BLOCK))[:seq // SEL_BLOCK]
    new_row = lambda a: a.reshape(dec_batch, 1, N_KV_HEADS, HEAD_DIM)
    win = (state_win_k[0], state_win_v[0], new_row(kw_s), new_row(vw_s))
    attn_steps = batch * N_KV_HEADS * (seq // ATTN_TQ)
    if dec_batch % attn_steps == 0:
        o_p, win_k_s, win_v_s = _attn_prompt(q_raw, q_rot, kcc, vcc, ks_hm, vs_hm, kw_hm, vw_hm, gates_p, ovt, expand,
                                             batch, seq, win=win)
    else:
        (o_p,) = _attn_prompt(q_raw, q_rot, kcc, vcc, ks_hm, vs_hm, kw_hm, vw_hm, gates_p, ovt, expand, batch, seq)
        win_k_s, win_v_s = _win_update(*win)
    xp = _matmul_residual(o_p, w_out, xp, 2 * tm)

    per_head = lambda a: jnp.repeat(a.reshape(dec_batch, N_KV_HEADS, HEAD_DIM), GQA_GROUP, axis=1)
    g3 = gates_s[:, :, :N_BRANCHES * GQA_GROUP].reshape(N_KV_HEADS, dec_batch, N_BRANCHES, GQA_GROUP)
    g3 = jnp.transpose(g3, (1, 0, 3, 2)).reshape(dec_batch, N_HEADS, N_BRANCHES)
    chunk = CMP_STRIDE * N_KV_HEADS
    paged = lambda c: c[0].reshape(c.shape[1], PAGE_SIZE // CMP_STRIDE, chunk, HEAD_DIM)
    win_rows = lambda s: s[0].reshape(dec_batch, win_buf * N_KV_HEADS, HEAD_DIM)
    gs = (jnp.arange(N_HEADS)[:, None] // GQA_GROUP == jnp.arange(N_HEADS)[None, :] // GQA_GROUP).astype(BF16)
    n_sub_s = past // CMP_STRIDE
    ov_s = jnp.repeat(_overlap(n_sub_s, past // SEL_BLOCK + 1), N_KV_HEADS, axis=0)
    o_s = _attn_sample(page_table, [paged(c) for c in (cache_cmp_k, cache_cmp_v, cache_sel_k, cache_sel_v)],
                       win_rows(state_win_k), win_rows(state_win_v),
                       q_raw_s.reshape(dec_batch, N_HEADS, HEAD_DIM), q_rot_s.reshape(dec_batch, N_HEADS, HEAD_DIM),
                       [per_head(a) for a in (ks_s, vs_s, kw_s, vw_s)], g3, cmp_k_w, cmp_v_w, ov_s, gs)
    xs = _matmul_residual(o_s.reshape(dec_batch, Q_DIM), w_out, xs, dec_batch)

    yp, ys = ffn(xp, xs, [(1, 1, norm_ffn2[1])], vec(norm_final))
    y_prompt = yp.reshape(batch, seq, D_MODEL)
    y_sample = ys.reshape(dec_batch, 1, D_MODEL)

    kv5 = lambda a, b, s: a.reshape(1, b, s, N_KV_HEADS, HEAD_DIM)
    keep = min(WINDOW, seq)
    return (y_prompt, y_sample, pool_p[None], pool_s[None],
            kv5(kc_p, batch, seq), kv5(kc_s, dec_batch, 1), kv5(vc_p, batch, seq), kv5(vc_s, dec_batch, 1),
            kv5(ks_p, batch, seq), kv5(ks_s, dec_batch, 1), kv5(vs_p, batch, seq), kv5(vs_s, dec_batch, 1),
            kv5(kw_p, batch, seq)[:, :, seq - keep:], win_k_s[None],
            kv5(vw_p, batch, seq)[:, :, seq - keep:], win_v_s[None])
```
